```python
import jax, jax.numpy as jnp
from jax import lax
import numpy as np

D_MODEL = 1024
BATCH = 1
SEQ = 16384
DEPTH = 1
DEC_BATCH = 128
DEC_SEQ = 1
PAST_LEN = 8192
PAGE_SIZE = 128

MIX_WIDTH = D_MODEL
ATTN_WIDTH = MIX_WIDTH // 2
GMLP_WIDTH = MIX_WIDTH - ATTN_WIDTH
HEAD_DIM = 64
N_ATTN_HEADS = ATTN_WIDTH // HEAD_DIM
N_GMLP_GROUPS = 8
GMLP_GROUP_DIM = GMLP_WIDTH // N_GMLP_GROUPS
GMLP_CHUNK = 128
MOBA_BLOCK = 256
MOBA_TOPK = 3
Q_BLOCK = 128
PEER_HEADS = 8
PEER_N_KEYS = 128
PEER_N_EXPERTS = PEER_N_KEYS * PEER_N_KEYS
PEER_QDIM = 256
PEER_HALF = PEER_QDIM // 2
PEER_TOPK = 16
PEER_TOKEN_BLOCK = 256
IN_COLS = 3 * ATTN_WIDTH + 2 * GMLP_WIDTH
EPS = 1e-6
NEG = -1e30

kernel_name = "hymba_moba_gmlp_peer_step"


def _rmsnorm(x, g):
    xf = x.astype(jnp.float32)
    xf = xf * lax.rsqrt(jnp.mean(xf * xf, axis=-1, keepdims=True) + EPS)
    return (xf * g.astype(jnp.float32)).astype(x.dtype)


def _alibi_slopes():
    return jnp.exp2(-8.0 * jnp.arange(1, N_ATTN_HEADS + 1, dtype=jnp.float32) / N_ATTN_HEADS)


def _project(x, norm1_g, w_in, q_norm_g, k_norm_g, gmlp_v_norm_g):
    b, l, _ = x.shape
    h = _rmsnorm(x, norm1_g) @ w_in
    q, k, v, gu, gv = jnp.split(h, [ATTN_WIDTH, 2 * ATTN_WIDTH, 3 * ATTN_WIDTH, 3 * ATTN_WIDTH + GMLP_WIDTH], axis=-1)
    hd = (b, l, N_ATTN_HEADS, HEAD_DIM)
    q = _rmsnorm(q.reshape(hd), q_norm_g)
    k = _rmsnorm(k.reshape(hd), k_norm_g)
    v = v.reshape(hd)
    gu = jax.nn.gelu(gu)
    gv = _rmsnorm(jax.nn.gelu(gv).reshape(b, l, N_GMLP_GROUPS, GMLP_GROUP_DIM),
                  gmlp_v_norm_g.reshape(N_GMLP_GROUPS, GMLP_GROUP_DIM))
    return q, k, v, gu, gv


def _gmlp_mix(gu, gv, gmlp_ws, gmlp_b):
    b, l = gv.shape[:2]
    rows = min(l, GMLP_CHUNK)
    n = -(-l // rows)
    vv = jnp.pad(gv, ((0, 0), (0, n * rows - l), (0, 0), (0, 0))).reshape(b, n, rows, N_GMLP_GROUPS, GMLP_GROUP_DIM)
    w = jnp.tril(gmlp_ws[:, :rows, :rows])
    mixed = jnp.einsum('gts,bcsgd->bctgd', w, vv) + gmlp_b[:, :rows].T[None, None, :, :, None]
    return gu * mixed.reshape(b, n * rows, GMLP_WIDTH)[:, :l]


def _pad_to_topk(m):
    return jnp.pad(m, ((0, 0), (0, max(0, MOBA_TOPK - m.shape[1])), (0, 0), (0, 0)))


def _select_blocks(q, k_mean, n_past_full):
    gate = jnp.einsum('bqhd,bnhd->bqhn', q, k_mean).astype(jnp.float32)
    blk = jnp.arange(k_mean.shape[1])
    gate = jnp.where(blk < n_past_full, gate, NEG)
    _, idx = lax.top_k(gate, MOBA_TOPK)
    valid = idx < n_past_full
    return jnp.where(valid, idx, 0), valid


def _sel_positions(idx, valid):
    shp = idx.shape[:3] + (MOBA_TOPK * MOBA_BLOCK,)
    s_sel = (idx[..., None] * MOBA_BLOCK + jnp.arange(MOBA_BLOCK)).reshape(shp)
    valid_sel = jnp.broadcast_to(valid[..., None], idx.shape + (MOBA_BLOCK,)).reshape(shp)
    return s_sel, valid_sel


def _attend(q, t_pos, k_sel, v_sel, s_sel, valid_sel, k_loc, v_loc, s_loc):
    scale = HEAD_DIM ** -0.5
    slopes = _alibi_slopes()[:, None]
    t = t_pos.astype(jnp.float32)
    ls = jnp.einsum('bqhd,bqhnd->bqhn', q, k_sel).astype(jnp.float32) * scale \
        - slopes * (t[:, None, None] - s_sel.astype(jnp.float32))
    ls = jnp.where(valid_sel, ls, NEG)
    ll = jnp.einsum('bqhd,bkhd->bqhk', q, k_loc).astype(jnp.float32) * scale \
        - slopes * (t[:, None, None] - s_loc.astype(jnp.float32)[None, None, :])
    ll = jnp.where(s_loc[None, None, :] <= t_pos[:, None, None], ll, NEG)
    p = jax.nn.softmax(jnp.concatenate([ls, ll], axis=-1), axis=-1)
    ns = ls.shape[-1]
    return (jnp.einsum('bqhn,bqhnd->bqhd', p[..., :ns].astype(v_sel.dtype), v_sel)
            + jnp.einsum('bqhk,bkhd->bqhd', p[..., ns:].astype(v_loc.dtype), v_loc))


def _moba_prompt(q, k, v):
    b, s = q.shape[:2]
    nb = -(-s // MOBA_BLOCK)
    pad = ((0, 0), (0, nb * MOBA_BLOCK - s), (0, 0), (0, 0))
    kp, vp = jnp.pad(k, pad), jnp.pad(v, pad)
    kb = kp.reshape(b, nb, MOBA_BLOCK, N_ATTN_HEADS, HEAD_DIM)
    vb = vp.reshape(b, nb, MOBA_BLOCK, N_ATTN_HEADS, HEAD_DIM)
    k_mean = _pad_to_topk(jnp.mean(kb, axis=2))
    bi = jnp.arange(b)[:, None, None, None]
    hi = jnp.arange(N_ATTN_HEADS)[None, None, :, None]
    sel_shape = (b, Q_BLOCK, N_ATTN_HEADS, MOBA_TOPK * MOBA_BLOCK, HEAD_DIM)

    def one_block(qb):
        start = qb * Q_BLOCK
        qq = lax.dynamic_slice_in_dim(q, start, Q_BLOCK, axis=1)
        t_pos = start + jnp.arange(Q_BLOCK)
        own = start // MOBA_BLOCK
        idx, valid = _select_blocks(qq, k_mean, own)
        k_sel = kb[bi, idx, :, hi, :].reshape(sel_shape)
        v_sel = vb[bi, idx, :, hi, :].reshape(sel_shape)
        s_sel, valid_sel = _sel_positions(idx, valid)
        k_loc = lax.dynamic_slice_in_dim(kp, own * MOBA_BLOCK, MOBA_BLOCK, axis=1)
        v_loc = lax.dynamic_slice_in_dim(vp, own * MOBA_BLOCK, MOBA_BLOCK, axis=1)
        s_loc = own * MOBA_BLOCK + jnp.arange(MOBA_BLOCK)
        return _attend(qq, t_pos, k_sel, v_sel, s_sel, valid_sel, k_loc, v_loc, s_loc)

    out = lax.map(one_block, jnp.arange(s // Q_BLOCK))
    return jnp.moveaxis(out, 0, 1).reshape(b, s, ATTN_WIDTH)


def _moba_sample(q, k, v, cache_k, cache_v, page_table):
    db, ds = q.shape[:2]
    n_pages = PAST_LEN // PAGE_SIZE
    ppb = MOBA_BLOCK // PAGE_SIZE
    n_full = PAST_LEN // MOBA_BLOCK
    own_start = n_full * MOBA_BLOCK
    n_own_pages = (PAST_LEN - own_start) // PAGE_SIZE
    page_mean = jnp.mean(cache_k, axis=1)
    pm = page_mean[page_table[:, :n_full * ppb]]
    k_mean = _pad_to_topk(jnp.mean(pm.reshape(db, n_full, ppb, N_ATTN_HEADS, HEAD_DIM), axis=2))
    t_pos = PAST_LEN + jnp.arange(ds)
    idx, valid = _select_blocks(q, k_mean, n_full)
    di = jnp.arange(db)[:, None, None, None, None]
    hi = jnp.arange(N_ATTN_HEADS)[None, None, :, None, None]
    phys = page_table[di, idx[..., None] * ppb + jnp.arange(ppb)]
    sel_shape = (db, ds, N_ATTN_HEADS, MOBA_TOPK * MOBA_BLOCK, HEAD_DIM)
    k_sel = cache_k[phys, :, hi, :].reshape(sel_shape)
    v_sel = cache_v[phys, :, hi, :].reshape(sel_shape)
    s_sel, valid_sel = _sel_positions(idx, valid)
    own_phys = page_table[:, n_pages - n_own_pages:]
    k_own = cache_k[own_phys].reshape(db, n_own_pages * PAGE_SIZE, N_ATTN_HEADS, HEAD_DIM)
    v_own = cache_v[own_phys].reshape(db, n_own_pages * PAGE_SIZE, N_ATTN_HEADS, HEAD_DIM)
    k_loc = jnp.concatenate([k_own, k], axis=1)
    v_loc = jnp.concatenate([v_own, v], axis=1)
    s_loc = jnp.arange(own_start, PAST_LEN + ds)
    out = _attend(q, t_pos, k_sel, v_sel, s_sel, valid_sel, k_loc, v_loc, s_loc)
    return out.reshape(db, ds, ATTN_WIDTH)


def _peer(x, peer_wq, peer_keys, peer_u, peer_v):
    n = x.shape[0]
    blk = min(n, PEER_TOKEN_BLOCK)
    nblk = -(-n // blk)
    xp = jnp.pad(x, ((0, nblk * blk - n), (0, 0))).reshape(nblk, blk, D_MODEL)

    def one(xb):
        q = (xb @ peer_wq).reshape(blk, PEER_HEADS, 2, PEER_HALF)
        s = jnp.einsum('thpk,hpnk->thpn', q, peer_keys).astype(jnp.float32)
        sv, si = lax.top_k(s, PEER_TOPK)
        cand = (sv[:, :, 0, :, None] + sv[:, :, 1, None, :]).reshape(blk, PEER_HEADS, PEER_TOPK * PEER_TOPK)
        fv, fi = lax.top_k(cand, PEER_TOPK)
        i1 = jnp.take_along_axis(si[:, :, 0], fi // PEER_TOPK, axis=-1)
        i2 = jnp.take_along_axis(si[:, :, 1], fi % PEER_TOPK, axis=-1)
        expert = i1 * PEER_N_KEYS + i2
        g = jax.nn.softmax(fv, axis=-1)
        a = jax.nn.gelu(jnp.einsum('td,thkd->thk', xb, peer_u[expert]).astype(jnp.float32))
        return jnp.einsum('thk,thkd->td', (g * a).astype(xb.dtype), peer_v[expert])

    return lax.map(one, xp).reshape(nblk * blk, D_MODEL)[:n]


def _layer_out(x, attn_o, gmlp_o, attn_out_norm_g, gmlp_out_norm_g, w_out, norm2_g, peer_wq, peer_keys, peer_u, peer_v):
    mix = jnp.concatenate([_rmsnorm(attn_o, attn_out_norm_g), _rmsnorm(gmlp_o, gmlp_out_norm_g)], axis=-1)
    h = x + mix @ w_out
    b, l, d = h.shape
    return h + _peer(_rmsnorm(h, norm2_g).reshape(b * l, d), peer_wq, peer_keys, peer_u, peer_v).reshape(b, l, d)


def setup_inputs(seed: int = 0) -> dict:
    key = jax.random.key(seed)
    ks = jax.random.split(key, 24)
    f32 = jnp.float32
    n_pages = PAST_LEN // PAGE_SIZE
    n_used = DEC_BATCH * n_pages
    n_pool = n_used + n_used // 4

    def nrm(k, shape, scale):
        return jax.random.normal(k, shape, f32) * scale

    def gain(k, shape):
        return 1.0 + 0.02 * jax.random.normal(k, shape, f32)

    return {
        "x_prompt": nrm(ks[0], (BATCH, SEQ, D_MODEL), 1.0),
        "x_sample": nrm(ks[1], (DEC_BATCH, DEC_SEQ, D_MODEL), 1.0),
        "cache_k": nrm(ks[2], (n_pool, PAGE_SIZE, N_ATTN_HEADS, HEAD_DIM), 1.0),
        "cache_v": nrm(ks[3], (n_pool, PAGE_SIZE, N_ATTN_HEADS, HEAD_DIM), 1.0),
        "page_table": jax.random.permutation(ks[4], n_pool)[:n_used].reshape(DEC_BATCH, n_pages).astype(jnp.int32),
        "norm1_g": gain(ks[5], (D_MODEL,)),
        "w_in": nrm(ks[6], (D_MODEL, IN_COLS), D_MODEL ** -0.5),
        "q_norm_g": gain(ks[7], (HEAD_DIM,)),
        "k_norm_g": gain(ks[8], (HEAD_DIM,)),
        "gmlp_v_norm_g": gain(ks[9], (GMLP_WIDTH,)),
        "gmlp_ws": nrm(ks[10], (N_GMLP_GROUPS, GMLP_CHUNK, GMLP_CHUNK), GMLP_CHUNK ** -0.5),
        "gmlp_b": gain(ks[11], (N_GMLP_GROUPS, GMLP_CHUNK)),
        "attn_out_norm_g": gain(ks[12], (ATTN_WIDTH,)),
        "gmlp_out_norm_g": gain(ks[13], (GMLP_WIDTH,)),
        "w_out": nrm(ks[14], (MIX_WIDTH, D_MODEL), MIX_WIDTH ** -0.5),
        "norm2_g": gain(ks[15], (D_MODEL,)),
        "peer_wq": nrm(ks[16], (D_MODEL, PEER_HEADS * PEER_QDIM), D_MODEL ** -0.5),
        "peer_keys": nrm(ks[17], (PEER_HEADS, 2, PEER_N_KEYS, PEER_HALF), PEER_HALF ** -0.5),
        "peer_u": nrm(ks[18], (PEER_N_EXPERTS, D_MODEL), D_MODEL ** -0.5),
        "peer_v": nrm(ks[19], (PEER_N_EXPERTS, D_MODEL), PEER_HEADS ** -0.5),
    }


def reference(x_prompt, x_sample, cache_k, cache_v, page_table, norm1_g, w_in, q_norm_g, k_norm_g,
              gmlp_v_norm_g, gmlp_ws, gmlp_b, attn_out_norm_g, gmlp_out_norm_g, w_out, norm2_g,
              peer_wq, peer_keys, peer_u, peer_v):
    y_p, y_s = x_prompt, x_sample
    for _ in range(DEPTH):
        q, k_p, v_p, gu, gv = _project(y_p, norm1_g, w_in, q_norm_g, k_norm_g, gmlp_v_norm_g)
        attn = _moba_prompt(q, k_p, v_p)
        gm = _gmlp_mix(gu, gv, gmlp_ws, gmlp_b)
        b, l = gv.shape[:2]
        last_start = ((l - 1) // GMLP_CHUNK) * GMLP_CHUNK
        gv_p = gv[:, last_start:].reshape(b, l - last_start, GMLP_WIDTH)
        y_p = _layer_out(y_p, attn, gm, attn_out_norm_g, gmlp_out_norm_g, w_out, norm2_g,
                         peer_wq, peer_keys, peer_u, peer_v)
        q, k_s, v_s, gu, gv = _project(y_s, norm1_g, w_in, q_norm_g, k_norm_g, gmlp_v_norm_g)
        attn = _moba_sample(q, k_s, v_s, cache_k, cache_v, page_table)
        gm = _gmlp_mix(gu, gv, gmlp_ws, gmlp_b)
        gv_s = gv.reshape(gv.shape[0], gv.shape[1], GMLP_WIDTH)
        y_s = _layer_out(y_s, attn, gm, attn_out_norm_g, gmlp_out_norm_g, w_out, norm2_g,
                         peer_wq, peer_keys, peer_u, peer_v)
    return (y_p, y_s, k_p, v_p, gv_p, k_s, v_s, gv_s)
```

```python
import functools
import math

import jax
import jax.numpy as jnp
import numpy as np
from jax import lax
from jax.experimental import pallas as pl
from jax.experimental.pallas import tpu as pltpu

F32 = jnp.float32
BF16 = jnp.bfloat16

D_MODEL = 1024
N_HEADS = 8
HEAD_DIM = 64
ATTN_W = N_HEADS * HEAD_DIM
GMLP_W = 512
N_GROUPS = 8
GROUP_DIM = GMLP_W // N_GROUPS
GMLP_CHUNK = 128
MOBA_BLOCK = 256
MOBA_TOPK = 3
PAGE = 128
PEER_HEADS = 8
PEER_KEYS = 128
PEER_HALF = 128
PEER_TOPK = 16
N_EXPERTS = PEER_KEYS * PEER_KEYS
EPS = 1e-6
NEG = -1e30
QK_SCALE = HEAD_DIM ** -0.5
GELU_C = math.sqrt(2.0 / math.pi)

LANES = 128
VMEM_LIMIT = 56 * 1024 * 1024

_CAND_ROWS = tuple((a, PEER_TOPK // (a + 1)) for a in range(8))


def _dot(a, b):
    return jnp.dot(a, b, preferred_element_type=F32)


def _dot_nt(a, b):
    return lax.dot_general(a, b, (((1,), (1,)), ((), ())), preferred_element_type=F32)


def _split(a):
    hi = a.astype(BF16)
    lo = (a - hi.astype(F32)).astype(BF16)
    return hi, lo


def _dot3(a, b):
    ah, al = _split(a)
    bh, bl = _split(b)
    return _dot(ah, bh) + (_dot(ah, bl) + _dot(al, bh))


def _dot2_exact_rhs(a, b_bf16):
    ah, al = _split(a)
    return _dot(ah, b_bf16) + _dot(al, b_bf16)


def _rms(x, axis=-1):
    return x * lax.rsqrt(jnp.mean(x * x, axis=axis, keepdims=True) + EPS)


def _gelu(x):
    return x * (0.5 * (1.0 + jnp.tanh(GELU_C * (x + 0.044715 * (x * x * x)))))


def _iota(shape, dim, dtype=jnp.int32):
    return lax.broadcasted_iota(dtype, shape, dim)


def _params(*sem):
    return pltpu.CompilerParams(dimension_semantics=sem, vmem_limit_bytes=VMEM_LIMIT)


def _full(shape):
    nd = len(shape)
    return pl.BlockSpec(shape, lambda *_: (0,) * nd)


_KP0, _KN0, _V0, _GU0, _GV0, _NAT_COLS = 0, 1024, 1536, 2048, 2560, 3072


def _gmlp_norm(gvr, bones_ref, gvg_ref):
    hi, lo = _split(gvr * gvr)
    ss = _dot(hi, bones_ref[...]) + _dot(lo, bones_ref[...])
    return gvr * lax.rsqrt(ss * (1.0 / GROUP_DIM) + EPS) * gvg_ref[...]


def _proj_prompt_kernel(x_ref, g1_ref, wnat_ref, wt_ref, qg_ref, kgp_ref, kgn_ref, gvg_ref, bones_ref,
                        wcat_ref, gbias_ref, gog_ref,
                        k_ref, v_ref, kaug_ref, qaug_ref, vt_ref, gm_ref, gvl_ref,
                        kmean_s, wtril_s):
    i = pl.program_id(0)
    tm = x_ref.shape[0]
    nblk = kmean_s.shape[1]

    @pl.when(i == 0)
    def _init():
        row = _iota((GMLP_CHUNK, N_GROUPS * GMLP_CHUNK), 0)
        col = _iota((GMLP_CHUNK, N_GROUPS * GMLP_CHUNK), 1) & (GMLP_CHUNK - 1)
        wtril_s[...] = jnp.where(col <= row, wcat_ref[...], 0.0).astype(BF16)
        kmean_s[...] = jnp.zeros(kmean_s.shape, F32)

    xn = (_rms(x_ref[...]) * g1_ref[...]).astype(BF16)
    hn = _dot(xn, wnat_ref[...])
    ht = _dot_nt(wt_ref[...], xn)

    lane = _iota((tm, LANES), 1)
    rowf = _iota((tm, LANES), 0).astype(F32)
    blk_row = _iota((nblk, tm), 0)
    blk_rowf = blk_row.astype(F32)
    const_rows = jnp.where(_iota((LANES, tm), 0) < 2, 1.0, 0.0)
    i_f = i.astype(F32)

    rs_heads = []
    for h in range(N_HEADS):
        kp = hn[:, _KP0 + LANES * h:_KP0 + LANES * (h + 1)]
        rs = lax.rsqrt(jnp.sum(kp * kp, axis=-1, keepdims=True) * (1.0 / HEAD_DIM) + EPS)
        rs_heads.append(rs)
        kn = kp * rs * kgp_ref[...]
        kmean_s[h, pl.ds(i, 1), :] = jnp.mean(kn, axis=0, keepdims=True)

        qt = ht[HEAD_DIM * h:HEAD_DIM * (h + 1), :]
        qn = qt * lax.rsqrt(jnp.sum(qt * qt, axis=0, keepdims=True) * (1.0 / HEAD_DIM) + EPS) * qg_ref[...]
        gate = _dot3(kmean_s[h], jnp.concatenate([qn, jnp.zeros_like(qn)], axis=0))
        cur = jnp.where(blk_row < i, gate, NEG)
        sel = jnp.zeros_like(cur)
        for _ in range(MOBA_TOPK):
            m = jnp.max(cur, axis=0, keepdims=True)
            first = jnp.min(jnp.where(cur == m, blk_rowf, float(nblk)), axis=0, keepdims=True)
            hit = blk_rowf == first
            sel = jnp.where(hit, 1.0, sel)
            cur = jnp.where(hit, -jnp.inf, cur)
        keep = jnp.where(blk_row < i, sel, jnp.where(blk_row == i, 1.0, 0.0))
        sel_bias = jnp.where(keep > 0.0, 0.0, NEG)
        pieces = [qn * QK_SCALE, sel_bias]
        if nblk < HEAD_DIM:
            pieces.append(jnp.zeros((HEAD_DIM - nblk, tm), F32))
        qaug_ref[h] = jnp.concatenate(pieces + [const_rows], axis=0).astype(BF16)

        slope = 2.0 ** -(h + 1)
        k_lo = jnp.where(lane == HEAD_DIM + i, 1.0, kn)
        k_hi = jnp.where(lane == 0, slope * rowf, jnp.where(lane == 1, (slope * MOBA_BLOCK) * i_f, 0.0))
        kaug_ref[h] = jnp.concatenate([k_lo, k_hi], axis=1).astype(BF16)
        vt_ref[h] = ht[ATTN_W + HEAD_DIM * h:ATTN_W + HEAD_DIM * (h + 1), :].astype(BF16)

    cols = []
    for c in range(ATTN_W // LANES):
        sc = jnp.where(lane < HEAD_DIM, rs_heads[2 * c], rs_heads[2 * c + 1])
        cols.append(hn[:, _KN0 + LANES * c:_KN0 + LANES * (c + 1)] * sc)
    k_ref[...] = jnp.concatenate(cols, axis=1) * kgn_ref[...]
    v_ref[...] = hn[:, _V0:_V0 + ATTN_W]

    gu = _gelu(hn[:, _GU0:_GU0 + GMLP_W])
    gvn = _gmlp_norm(_gelu(hn[:, _GV0:_GV0 + GMLP_W]), bones_ref, gvg_ref)
    gvl_ref[...] = gvn[tm - GMLP_CHUNK:, :]
    group_of_lane = _iota((1, GMLP_W), 1) >> 6
    outs = []
    for c in range(tm // GMLP_CHUNK):
        gc = gvn[GMLP_CHUNK * c:GMLP_CHUNK * (c + 1), :]
        stacked = jnp.concatenate([jnp.where(group_of_lane == g, gc, 0.0) for g in range(N_GROUPS)], axis=0)
        mixed = _dot(wtril_s[...], stacked.astype(BF16)) + gbias_ref[...]
        outs.append(gu[GMLP_CHUNK * c:GMLP_CHUNK * (c + 1), :] * mixed)
    gm_ref[...] = (_rms(jnp.concatenate(outs, axis=0)) * gog_ref[...]).astype(BF16)


def _proj_prompt(x, w, tm=MOBA_BLOCK):
    t = x.shape[0]
    nblk = t // tm
    assert nblk <= HEAD_DIM
    row_tile = lambda width: pl.BlockSpec((tm, width), lambda i: (i, 0))
    in_specs = [row_tile(D_MODEL), _full((1, D_MODEL)), _full((D_MODEL, _NAT_COLS)), _full((D_MODEL, D_MODEL)),
                _full((HEAD_DIM, 1)), _full((1, LANES)), _full((1, ATTN_W)), _full((1, GMLP_W)),
                _full((GMLP_W, GMLP_W)), _full((GMLP_CHUNK, N_GROUPS * GMLP_CHUNK)), _full((GMLP_CHUNK, GMLP_W)),
                _full((1, GMLP_W))]
    out_shape = (jax.ShapeDtypeStruct((t, ATTN_W), F32), jax.ShapeDtypeStruct((t, ATTN_W), F32),
                 jax.ShapeDtypeStruct((N_HEADS, t, 2 * LANES), BF16),
                 jax.ShapeDtypeStruct((N_HEADS, 2 * LANES, t), BF16),
                 jax.ShapeDtypeStruct((N_HEADS, HEAD_DIM, t), BF16),
                 jax.ShapeDtypeStruct((t, GMLP_W), BF16), jax.ShapeDtypeStruct((GMLP_CHUNK, GMLP_W), F32))
    out_specs = (row_tile(ATTN_W), row_tile(ATTN_W),
                 pl.BlockSpec((N_HEADS, tm, 2 * LANES), lambda i: (0, i, 0)),
                 pl.BlockSpec((N_HEADS, 2 * LANES, tm), lambda i: (0, 0, i)),
                 pl.BlockSpec((N_HEADS, HEAD_DIM, tm), lambda i: (0, 0, i)),
                 row_tile(GMLP_W), _full((GMLP_CHUNK, GMLP_W)))
    return pl.pallas_call(
        _proj_prompt_kernel, grid=(nblk,), in_specs=in_specs, out_specs=out_specs, out_shape=out_shape,
        scratch_shapes=[pltpu.VMEM((N_HEADS, nblk, LANES), F32),
                        pltpu.VMEM((GMLP_CHUNK, N_GROUPS * GMLP_CHUNK), BF16)],
        compiler_params=_params("arbitrary"), name="proj_prompt",
    )(x, w["g1"], w["wnat"], w["wt"], w["qg_col"], w["kg_pad"], w["kg_nat"], w["gvg"], w["bones"],
      w["wcat"], w["gbias"], w["gog"])


def _moba_kernel(kaug_ref, qaug_ref, vt_ref, out_ref):
    j = pl.program_id(1)
    tq = qaug_ref.shape[1]
    qa = qaug_ref[...]

    def scores(n):
        start = pl.multiple_of(n * MOBA_BLOCK, MOBA_BLOCK)
        return _dot(kaug_ref[pl.ds(start, MOBA_BLOCK), :], qa), vt_ref[:, pl.ds(start, MOBA_BLOCK)]

    s, vb = scores(j)
    s = jnp.where(_iota((MOBA_BLOCK, tq), 0) <= _iota((MOBA_BLOCK, tq), 1), s, NEG)
    m = jnp.max(s, axis=0, keepdims=True)
    p = jnp.exp(s - m)
    l = jnp.sum(p, axis=0, keepdims=True)
    acc = _dot(vb, p.astype(BF16))

    def body(n, carry):
        m, l, acc = carry
        s, vb = scores(n)
        m_new = jnp.maximum(m, jnp.max(s, axis=0, keepdims=True))
        alpha = jnp.exp(m - m_new)
        p = jnp.exp(s - m_new)
        l = l * alpha + jnp.sum(p, axis=0, keepdims=True)
        acc = acc * alpha + _dot(vb, p.astype(BF16))
        return m_new, l, acc

    m, l, acc = lax.fori_loop(0, j, body, (m, l, acc))
    out_ref[...] = acc / l


def _moba_prompt(kaug, qaug, vt):
    t = kaug.shape[1]
    tq = MOBA_BLOCK
    return pl.pallas_call(
        _moba_kernel, grid=(N_HEADS, t // tq),
        in_specs=[pl.BlockSpec((None, t, 2 * LANES), lambda h, j: (h, 0, 0)),
                  pl.BlockSpec((None, 2 * LANES, tq), lambda h, j: (h, 0, j)),
                  pl.BlockSpec((None, HEAD_DIM, t), lambda h, j: (h, 0, 0))],
        out_specs=pl.BlockSpec((HEAD_DIM, tq), lambda h, j: (h, j)),
        out_shape=jax.ShapeDtypeStruct((ATTN_W, t), F32),
        compiler_params=_params("parallel", "arbitrary"), name="moba_prompt",
    )(kaug, qaug, vt)


def _cand_index_table():
    rows = []
    for a, nb in _CAND_ROWS:
        for c0 in range(0, max(nb, 8), 8):
            rows.append([a * PEER_TOPK + c0 + b for b in range(8)])
    rows.append([(8 + a) * PEER_TOPK for a in range(8)])
    flat = np.asarray(rows, np.float32).reshape(-1, 1)
    return jnp.asarray(np.broadcast_to(flat, (flat.shape[0], LANES)).copy())


_N_CAND = 8 * (sum(max(nb, 8) // 8 for _, nb in _CAND_ROWS) + 1)


def _extract_top(cur, idx, n, on_hit):
    big = float(1 << 20)
    for r in range(n):
        m = jnp.max(cur, axis=0, keepdims=True)
        first = jnp.min(jnp.where(cur == m, idx, big), axis=0, keepdims=True)
        hit = idx == first
        cur = jnp.where(hit, -jnp.inf, cur)
        on_hit(r, m, hit)
    return cur


def _route_kernel(x_ref, attn_ref, gm_ref, aog_ref, woa_ref, wog_ref, g2_ref, wqt_ref, keys_ref, cidx_ref,
                  h_ref, xn2_ref, m1_ref, c1_ref, r2_ref, e2_ref,
                  s_s, rank_s, sv_s, na_s, *, attn_transposed):
    tt = x_ref.shape[0]
    at = attn_ref[...]
    if attn_transposed:
        at = at.T
    an = (_rms(at) * aog_ref[...]).astype(BF16)
    h = x_ref[...] + _dot(an, woa_ref[...]) + _dot(gm_ref[...], wog_ref[...])
    h_ref[...] = h
    xn2 = (_rms(h) * g2_ref[...]).astype(BF16)
    xn2_ref[...] = xn2
    qt = _dot_nt(wqt_ref[...], xn2)
    for hp in range(2 * PEER_HEADS):
        s_s[hp] = _dot3(keys_ref[hp], qt[PEER_HALF * hp:PEER_HALF * (hp + 1), :])

    key_idx = _iota((PEER_KEYS, LANES), 0).astype(F32)
    row8 = _iota((8, LANES), 0)
    for sl in range(tt // LANES):
        lanes = slice(LANES * sl, LANES * (sl + 1))

        def rank_half(hp, carry):
            rank = [jnp.full((PEER_KEYS, LANES), float(PEER_TOPK), F32)]

            def on_hit(r, m, hit):
                rank[0] = jnp.where(hit, float(r), rank[0])
                sv_s[hp, r:r + 1, lanes] = m

            _extract_top(s_s[hp, :, lanes], key_idx, PEER_TOPK, on_hit)
            rank_s[hp, :, lanes] = rank[0]
            return carry

        lax.fori_loop(0, 2 * PEER_HEADS, rank_half, 0)

        def combine(hd, carry):
            sv1 = sv_s[2 * hd, :, lanes]
            sv2 = sv_s[2 * hd + 1, :, lanes]
            pieces = []
            for a, nb in _CAND_ROWS:
                for c0 in range(0, max(nb, 8), 8):
                    piece = sv1[a:a + 1, :] + sv2[c0:c0 + 8, :]
                    pieces.append(piece if nb >= 8 else jnp.where(row8 < nb, piece, -jnp.inf))
            pieces.append(sv1[8:16, :] + sv2[0:1, :])
            cand = jnp.concatenate(pieces, axis=0)
            chosen = [jnp.zeros_like(cand)]

            def on_hit(r, m, hit):
                chosen[0] = jnp.where(hit, 1.0, chosen[0])

            _extract_top(cand, cidx_ref[...], PEER_TOPK, on_hit)
            chosen = chosen[0]
            top = sv1[0:1, :] + sv2[0:1, :]
            z = jnp.sum(jnp.where(chosen > 0.0, jnp.exp(cand - top), 0.0), axis=0, keepdims=True)
            row = 0
            for a, nb in _CAND_ROWS:
                nrows = max(nb, 8)
                na_s[a:a + 1, :] = jnp.sum(chosen[row:row + nrows, :], axis=0, keepdims=True)
                row += nrows
            na_s[8:16, :] = chosen[row:row + 8, :]
            na = na_s[...]
            rank1 = rank_s[2 * hd, :, lanes]
            m1 = jnp.zeros((PEER_KEYS, LANES), F32)
            for a in range(PEER_TOPK):
                m1 = jnp.where(rank1 == float(a), na[a:a + 1, :], m1)
            m1_ref[hd, :, lanes] = m1
            c1_ref[hd, :, lanes] = jnp.exp(s_s[2 * hd, :, lanes] - sv1[0:1, :]) / z
            r2_ref[hd, :, lanes] = rank_s[2 * hd + 1, :, lanes].astype(BF16)
            e2_ref[hd, :, lanes] = jnp.exp(s_s[2 * hd + 1, :, lanes] - sv2[0:1, :]).astype(BF16)
            return carry

        lax.fori_loop(0, PEER_HEADS, combine, 0)


def _route(x, attn, gm, w, *, tt, attn_transposed):
    t = x.shape[0]
    row_tile = lambda width: pl.BlockSpec((tt, width), lambda i: (i, 0))
    attn_spec = pl.BlockSpec((ATTN_W, tt), lambda i: (0, i)) if attn_transposed else row_tile(ATTN_W)
    head_tile = pl.BlockSpec((PEER_HEADS, PEER_KEYS, tt), lambda i: (0, 0, i))
    in_specs = [row_tile(D_MODEL), attn_spec, row_tile(GMLP_W), _full((1, ATTN_W)),
                _full((ATTN_W, D_MODEL)), _full((GMLP_W, D_MODEL)), _full((1, D_MODEL)),
                _full((2 * PEER_HEADS * PEER_HALF, D_MODEL)), _full((2 * PEER_HEADS, PEER_KEYS, PEER_HALF)),
                _full((_N_CAND, LANES))]
    stat = lambda dt: jax.ShapeDtypeStruct((PEER_HEADS, PEER_KEYS, t), dt)
    out_shape = (jax.ShapeDtypeStruct((t, D_MODEL), F32), jax.ShapeDtypeStruct((t, D_MODEL), BF16),
                 stat(F32), stat(F32), stat(BF16), stat(BF16))
    out_specs = (row_tile(D_MODEL), row_tile(D_MODEL), head_tile, head_tile, head_tile, head_tile)
    return pl.pallas_call(
        functools.partial(_route_kernel, attn_transposed=attn_transposed),
        grid=(t // tt,), in_specs=in_specs, out_specs=out_specs, out_shape=out_shape,
        scratch_shapes=[pltpu.VMEM((2 * PEER_HEADS, PEER_KEYS, tt), F32),
                        pltpu.VMEM((2 * PEER_HEADS, PEER_KEYS, tt), F32),
                        pltpu.VMEM((2 * PEER_HEADS, PEER_TOPK, tt), F32),
                        pltpu.VMEM((PEER_TOPK, LANES), F32)],
        compiler_params=_params("parallel"), name="route",
    )(x, attn, gm, w["aog"], w["wo_attn"], w["wo_gmlp"], w["g2"], w["wqt"], w["keys"], w["cand_idx"])


def _peer_kernel(xn2_ref, u_ref, vt_ref, m1_ref, c1_ref, r2_ref, e2_ref, h_ref, y_ref, acc_s, *, keys_per_step):
    c = pl.program_id(1)

    @pl.when(c == 0)
    def _zero():
        acc_s[...] = jnp.zeros(acc_s.shape, F32)

    ga = _gelu(_dot_nt(u_ref[...], xn2_ref[...]))
    parts = []
    for ii in range(keys_per_step):
        i = c * keys_per_step + ii
        g = None
        for hd in range(PEER_HEADS):
            partners = m1_ref[hd, pl.ds(i, 1), :].astype(BF16)
            weight = c1_ref[hd, pl.ds(i, 1), :].astype(BF16)
            term = jnp.where(r2_ref[hd] < partners, e2_ref[hd], jnp.zeros((), BF16)) * weight
            g = term if g is None else g + term
        parts.append(g * ga[PEER_KEYS * ii:PEER_KEYS * (ii + 1), :].astype(BF16))
    acc_s[...] += _dot(vt_ref[...], jnp.concatenate(parts, axis=0))

    @pl.when(c == pl.num_programs(1) - 1)
    def _finish():
        y_ref[...] = h_ref[...] + acc_s[...].T


def _peer(xn2, h, m1, c1, r2, e2, w, *, tt, keys_per_step=2):
    t = xn2.shape[0]
    ne = keys_per_step * PEER_KEYS
    head_tile = pl.BlockSpec((PEER_HEADS, PEER_KEYS, tt), lambda j, c: (0, 0, j))
    row_tile = pl.BlockSpec((tt, D_MODEL), lambda j, c: (j, 0))
    return pl.pallas_call(
        functools.partial(_peer_kernel, keys_per_step=keys_per_step),
        grid=(t // tt, N_EXPERTS // ne),
        in_specs=[row_tile, pl.BlockSpec((ne, D_MODEL), lambda j, c: (c, 0)),
                  pl.BlockSpec((D_MODEL, ne), lambda j, c: (0, c)),
                  head_tile, head_tile, head_tile, head_tile, row_tile],
        out_specs=row_tile, out_shape=jax.ShapeDtypeStruct((t, D_MODEL), F32),
        scratch_shapes=[pltpu.VMEM((D_MODEL, tt), F32)],
        compiler_params=_params("parallel", "arbitrary"), name="peer",
    )(xn2, w["u"], w["vt"], m1, c1, r2, e2, h)


def _proj_sample_kernel(x_ref, g1_ref, win_ref, bones_ref, qgn_ref, kgn_ref, gvg_ref, wsc_ref, bsc_ref, gog_ref,
                        tile8_ref, k_ref, v_ref, gv_ref, gm_ref, qexp_ref, selfl_ref):
    nb = x_ref.shape[0]
    xn = (_rms(x_ref[...]) * g1_ref[...]).astype(BF16)
    hn = _dot(xn, win_ref[...])

    def head_norm(z, g_ref):
        hi, lo = _split(z * z)
        ss = _dot(hi, bones_ref[...]) + _dot(lo, bones_ref[...])
        return z * lax.rsqrt(ss * (1.0 / HEAD_DIM) + EPS) * g_ref[...]

    qn = head_norm(hn[:, 0:ATTN_W], qgn_ref)
    kn = head_norm(hn[:, ATTN_W:2 * ATTN_W], kgn_ref)
    k_ref[...] = kn
    v_ref[...] = hn[:, 2 * ATTN_W:3 * ATTN_W]
    gu = _gelu(hn[:, 3 * ATTN_W:3 * ATTN_W + GMLP_W])
    gvn = _gmlp_norm(_gelu(hn[:, 3 * ATTN_W + GMLP_W:]), bones_ref, gvg_ref)
    gv_ref[...] = gvn
    gm_ref[...] = (_rms(gu * (wsc_ref[...] * gvn + bsc_ref[...])) * gog_ref[...]).astype(BF16)
    selfl_ref[...] = _dot3(qn * kn * QK_SCALE, tile8_ref[...])
    head_of_feature = _iota((LANES, ATTN_W), 1) >> 6
    head_of_lane = _iota((LANES, ATTN_W), 0) & (N_HEADS - 1)
    own = head_of_feature == head_of_lane
    qs = qn * QK_SCALE
    for d in range(nb):
        qexp_ref[d] = jnp.where(own, qs[d:d + 1, :], 0.0).astype(BF16)


def _proj_sample(x, w):
    nb = x.shape[0]
    out_shape = (jax.ShapeDtypeStruct((nb, ATTN_W), F32), jax.ShapeDtypeStruct((nb, ATTN_W), F32),
                 jax.ShapeDtypeStruct((nb, GMLP_W), F32), jax.ShapeDtypeStruct((nb, GMLP_W), BF16),
                 jax.ShapeDtypeStruct((nb, LANES, ATTN_W), BF16), jax.ShapeDtypeStruct((nb, LANES), F32))
    return pl.pallas_call(
        _proj_sample_kernel, out_shape=out_shape, compiler_params=_params(), name="proj_sample",
    )(x, w["g1"], w["win"], w["bones"], w["qg_nat"], w["kg_nat"], w["gvg"], w["ws_one"], w["b_one"], w["gog"],
      w["tile8"])


_PAGES_PER_STEP = 16


def _lane_group_allreduce(x, op, stride):
    shift = stride
    while shift < LANES:
        x = op(x, pltpu.roll(x, shift, axis=1))
        shift *= 2
    return x


def _sample_scores_kernel(pt_ref, qexp_ref, selfl_ref, *refs, n_steps, past_len):
    k_refs = refs[:_PAGES_PER_STEP]
    p_ref, linv_ref, pself_ref, sel_ref, wbd_s, sc_s = refs[_PAGES_PER_STEP:]
    c = pl.program_id(1)
    slot_of_lane = _iota((ATTN_W, LANES), 1) >> 3

    @pl.when(c == 0)
    def _weights():
        q = qexp_ref[...].T
        for r in range(_PAGES_PER_STEP):
            wbd_s[r] = jnp.where(slot_of_lane == r, q, jnp.zeros((), BF16))

    acc = None
    for r in range(_PAGES_PER_STEP):
        part = _dot(k_refs[r][...].astype(BF16), wbd_s[r])
        acc = part if acc is None else acc + part
    sc_s[c] = acc

    @pl.when(c == n_steps - 1)
    def _select():
        lane = _iota((1, LANES), 1)
        slot = lane >> 3
        head_slope = jnp.exp2(-((lane & (N_HEADS - 1)) + 1).astype(F32))
        s = sc_s[...]
        page_sum = jnp.sum(s, axis=1)
        blk_sum = page_sum + pltpu.roll(page_sum, LANES - N_HEADS, axis=1)
        gate = blk_sum * (1.0 / MOBA_BLOCK)
        blocks_per_step = _PAGES_PER_STEP // 2
        step_row = _iota((n_steps, LANES), 0)
        blk_idx = (step_row * blocks_per_step + (_iota((n_steps, LANES), 1) >> 4)).astype(F32)
        even_slot = (_iota((n_steps, LANES), 1) & N_HEADS) == 0
        cur = jnp.where(even_slot, gate, -jnp.inf)
        sel = jnp.zeros_like(cur)
        n_blocks = n_steps * blocks_per_step
        for _ in range(MOBA_TOPK):
            m = _lane_group_allreduce(jnp.max(cur, axis=0, keepdims=True), jnp.maximum, 2 * N_HEADS)
            m = jnp.maximum(m, pltpu.roll(m, N_HEADS, axis=1))
            cand = jnp.where(cur == m, blk_idx, float(n_blocks))
            first = _lane_group_allreduce(jnp.min(cand, axis=0, keepdims=True), jnp.minimum, 2 * N_HEADS)
            first = jnp.minimum(first, pltpu.roll(first, N_HEADS, axis=1))
            hit = blk_idx == first
            sel = jnp.where(hit, 1.0, sel)
            cur = jnp.where(hit, -jnp.inf, cur)
        pos = (_iota((n_steps, PAGE, LANES), 0) * _PAGES_PER_STEP + (_iota((n_steps, PAGE, LANES), 2) >> 3)) * PAGE \
            + _iota((n_steps, PAGE, LANES), 1)
        logit = s - head_slope[None] * (float(past_len) - pos.astype(F32))
        logit = jnp.where(sel[:, None, :] > 0.0, logit, NEG)
        self_logit = selfl_ref[...]
        mx = jnp.max(jnp.max(logit, axis=1), axis=0, keepdims=True)
        mx = _lane_group_allreduce(mx, jnp.maximum, N_HEADS)
        mx = jnp.maximum(mx, self_logit)
        p = jnp.exp(logit - mx[None])
        p_self = jnp.exp(self_logit - mx)
        tot = jnp.sum(jnp.sum(p, axis=1), axis=0, keepdims=True)
        tot = _lane_group_allreduce(tot, jnp.add, N_HEADS) + p_self
        for cc in range(n_steps):
            p_ref[cc] = p[cc].T.astype(BF16)
        linv_ref[...] = 1.0 / tot
        pself_ref[...] = p_self
        sel_ref[...] = sel


def _sample_scores(page_table, cache_k, qexp, selfl, past_len):
    nb, n_pages = page_table.shape
    n_steps = n_pages // _PAGES_PER_STEP
    page_spec = lambda r: pl.BlockSpec(
        (None, PAGE, ATTN_W), lambda d, c, pt, r=r: (pt[d, c * _PAGES_PER_STEP + r], 0, 0))
    per_seq = lambda shape: pl.BlockSpec((None,) + shape, lambda d, c, pt: (d,) + (0,) * len(shape))
    grid_spec = pltpu.PrefetchScalarGridSpec(
        num_scalar_prefetch=1, grid=(nb, n_steps),
        in_specs=[per_seq((LANES, ATTN_W)), per_seq((1, LANES))] + [page_spec(r) for r in range(_PAGES_PER_STEP)],
        out_specs=(per_seq((n_steps, LANES, PAGE)), per_seq((1, LANES)), per_seq((1, LANES)),
                   per_seq((n_steps, LANES))),
        scratch_shapes=[pltpu.VMEM((_PAGES_PER_STEP, ATTN_W, LANES), BF16),
                        pltpu.VMEM((n_steps, PAGE, LANES), F32)])
    out_shape = (jax.ShapeDtypeStruct((nb, n_steps, LANES, PAGE), BF16), jax.ShapeDtypeStruct((nb, 1, LANES), F32),
                 jax.ShapeDtypeStruct((nb, 1, LANES), F32), jax.ShapeDtypeStruct((nb, n_steps, LANES), F32))
    return pl.pallas_call(
        functools.partial(_sample_scores_kernel, n_steps=n_steps, past_len=past_len),
        grid_spec=grid_spec, out_shape=out_shape,
        compiler_params=_params("parallel", "arbitrary"), name="sample_scores",
    )(page_table, qexp, selfl.reshape(nb, 1, LANES), *([cache_k] * _PAGES_PER_STEP))


def _sample_values_kernel(pt_ref, fl_ref, p_ref, linv_ref, pself_ref, vnew_ref, spread_ref, *refs, n_steps):
    v_refs = refs[:_PAGES_PER_STEP]
    out_ref, acc_s = refs[_PAGES_PER_STEP:]
    d = pl.program_id(0)
    c = pl.program_id(1)

    @pl.when(c == 0)
    def _zero():
        acc_s[...] = jnp.zeros(acc_s.shape, F32)

    for r in range(_PAGES_PER_STEP):
        @pl.when(fl_ref[d, c * _PAGES_PER_STEP + r] > 0)
        def _page(r=r):
            res = _dot(p_ref[...], v_refs[r][...].astype(BF16))
            acc_s[...] += res[N_HEADS * r:N_HEADS * (r + 1), :]

    @pl.when(c == n_steps - 1)
    def _finish():
        own = (_iota((N_HEADS, ATTN_W), 1) >> 6) == _iota((N_HEADS, ATTN_W), 0)
        ctx = jnp.sum(jnp.where(own, acc_s[...], 0.0), axis=0, keepdims=True)
        p_self = _dot3(jnp.broadcast_to(pself_ref[...], (N_HEADS, LANES)), spread_ref[...])[0:1, :]
        linv = _dot3(jnp.broadcast_to(linv_ref[...], (N_HEADS, LANES)), spread_ref[...])[0:1, :]
        out_ref[...] = (ctx + p_self * vnew_ref[...]) * linv


def _sample_values(page_table_eff, flags, cache_v, p, linv, pself, v_new, spread):
    nb, n_pages = flags.shape
    n_steps = n_pages // _PAGES_PER_STEP
    page_spec = lambda r: pl.BlockSpec(
        (None, PAGE, ATTN_W), lambda d, c, pt, fl, r=r: (pt[d, c * _PAGES_PER_STEP + r], 0, 0))
    per_seq = lambda shape: pl.BlockSpec((None,) + shape, lambda d, c, pt, fl: (d,) + (0,) * len(shape))
    grid_spec = pltpu.PrefetchScalarGridSpec(
        num_scalar_prefetch=2, grid=(nb, n_steps),
        in_specs=[pl.BlockSpec((None, None, LANES, PAGE), lambda d, c, pt, fl: (d, c, 0, 0)),
                  per_seq((1, LANES)), per_seq((1, LANES)), per_seq((1, ATTN_W)),
                  pl.BlockSpec((LANES, ATTN_W), lambda d, c, pt, fl: (0, 0))]
        + [page_spec(r) for r in range(_PAGES_PER_STEP)],
        out_specs=per_seq((1, ATTN_W)),
        scratch_shapes=[pltpu.VMEM((N_HEADS, ATTN_W), F32)])
    return pl.pallas_call(
        functools.partial(_sample_values_kernel, n_steps=n_steps),
        grid_spec=grid_spec, out_shape=jax.ShapeDtypeStruct((nb, 1, ATTN_W), F32),
        compiler_params=_params("parallel", "arbitrary"), name="sample_values",
    )(page_table_eff, flags, p, linv, pself, v_new.reshape(nb, 1, ATTN_W), spread, *([cache_v] * _PAGES_PER_STEP))


def _prepare(norm1_g, w_in, q_norm_g, k_norm_g, gmlp_v_norm_g, gmlp_ws, gmlp_b, attn_out_norm_g, gmlp_out_norm_g,
             w_out, norm2_g, peer_wq, peer_keys, peer_u, peer_v):
    wq, wk, wv, wgu, wgv = jnp.split(w_in, [ATTN_W, 2 * ATTN_W, 3 * ATTN_W, 3 * ATTN_W + GMLP_W], axis=1)
    wk_pad = jnp.pad(wk.reshape(D_MODEL, N_HEADS, HEAD_DIM), ((0, 0), (0, 0), (0, LANES - HEAD_DIM)))
    feat = np.arange(ATTN_W)
    lanes = np.arange(LANES)
    w = {
        "g1": norm1_g.reshape(1, D_MODEL),
        "win": w_in.astype(BF16),
        "wnat": jnp.concatenate([wk_pad.reshape(D_MODEL, N_HEADS * LANES), wk, wv, wgu, wgv], axis=1).astype(BF16),
        "wt": jnp.concatenate([wq, wv], axis=1).T.astype(BF16),
        "qg_col": q_norm_g.reshape(HEAD_DIM, 1),
        "kg_pad": jnp.pad(k_norm_g, (0, LANES - HEAD_DIM)).reshape(1, LANES),
        "qg_nat": jnp.tile(q_norm_g, N_HEADS).reshape(1, ATTN_W),
        "kg_nat": jnp.tile(k_norm_g, N_HEADS).reshape(1, ATTN_W),
        "gvg": gmlp_v_norm_g.reshape(1, GMLP_W),
        "bones": jnp.asarray((feat[:, None] // GROUP_DIM) == (feat[None, :] // GROUP_DIM), BF16),
        "wcat": jnp.transpose(gmlp_ws, (1, 0, 2)).reshape(GMLP_CHUNK, N_GROUPS * GMLP_CHUNK),
        "gbias": jnp.repeat(gmlp_b.T, GROUP_DIM, axis=1),
        "ws_one": jnp.repeat(gmlp_ws[:, 0, 0], GROUP_DIM).reshape(1, GMLP_W),
        "b_one": jnp.repeat(gmlp_b[:, 0], GROUP_DIM).reshape(1, GMLP_W),
        "gog": gmlp_out_norm_g.reshape(1, GMLP_W),
        "aog": attn_out_norm_g.reshape(1, ATTN_W),
        "wo_attn": w_out[:ATTN_W].astype(BF16),
        "wo_gmlp": w_out[ATTN_W:].astype(BF16),
        "g2": norm2_g.reshape(1, D_MODEL),
        "wqt": peer_wq.T.astype(BF16),
        "keys": peer_keys.reshape(2 * PEER_HEADS, PEER_KEYS, PEER_HALF),
        "u": peer_u.astype(BF16),
        "vt": peer_v.T.astype(BF16),
        "cand_idx": _cand_index_table(),
        "tile8": jnp.asarray((feat[:, None] // HEAD_DIM) == (lanes[None, :] % N_HEADS), F32),
        "spread": jnp.asarray(lanes[:, None] == (feat[None, :] // HEAD_DIM), F32),
    }
    return w


def _fill_forward_pages(page_table, flags):
    nb, n_pages = page_table.shape
    n_steps = n_pages // _PAGES_PER_STEP
    pt = page_table.reshape(nb * n_steps, _PAGES_PER_STEP)
    fl = flags.reshape(nb * n_steps, _PAGES_PER_STEP)
    step = jnp.arange(nb * n_steps, dtype=jnp.int32)[:, None]
    last = lax.cummax(jnp.where(fl > 0, step, -1), axis=0)
    eff = jnp.take_along_axis(pt, jnp.maximum(last, 0), axis=0)
    return eff.reshape(nb, n_pages)


def _layer_tail(x, attn, gm, w, *, route_tile, peer_tile, attn_transposed):
    h, xn2, m1, c1, r2, e2 = _route(x, attn, gm, w, tt=route_tile, attn_transposed=attn_transposed)
    return _peer(xn2, h, m1, c1, r2, e2, w, tt=peer_tile)


def kernel(x_prompt, x_sample, cache_k, cache_v, page_table, norm1_g, w_in, q_norm_g, k_norm_g, gmlp_v_norm_g,
           gmlp_ws, gmlp_b, attn_out_norm_g, gmlp_out_norm_g, w_out, norm2_g, peer_wq, peer_keys, peer_u, peer_v):
    w = _prepare(norm1_g, w_in, q_norm_g, k_norm_g, gmlp_v_norm_g, gmlp_ws, gmlp_b, attn_out_norm_g,
                 gmlp_out_norm_g, w_out, norm2_g, peer_wq, peer_keys, peer_u, peer_v)
    b, t, _ = x_prompt.shape
    assert b == 1
    xp = x_prompt.reshape(t, D_MODEL)
    k_p, v_p, kaug, qaug, vt, gm_p, gv_last = _proj_prompt(xp, w)
    attn_t = _moba_prompt(kaug, qaug, vt)
    y_p = _layer_tail(xp, attn_t, gm_p, w, route_tile=256, peer_tile=512, attn_transposed=True)

    nb, ds, _ = x_sample.shape
    assert ds == 1
    xs = x_sample.reshape(nb, D_MODEL)
    n_pool = cache_k.shape[0]
    past_len = page_table.shape[1] * PAGE
    assert past_len % MOBA_BLOCK == 0
    k_s, v_s, gv_s, gm_s, qexp, selfl = _proj_sample(xs, w)
    p, linv, pself, sel = _sample_scores(page_table, cache_k.reshape(n_pool, PAGE, ATTN_W), qexp, selfl, past_len)
    page_flags = (sel.reshape(nb, -1, N_HEADS).max(axis=-1) > 0.0).astype(jnp.int32)
    pages_eff = _fill_forward_pages(page_table, page_flags)
    attn_s = _sample_values(pages_eff, page_flags, cache_v.reshape(n_pool, PAGE, ATTN_W), p, linv, pself, v_s,
                            w["spread"]).reshape(nb, ATTN_W)
    y_s = _layer_tail(xs, attn_s, gm_s, w, route_tile=nb, peer_tile=nb, attn_transposed=False)

    return (y_p.reshape(1, t, D_MODEL), y_s.reshape(nb, 1, D_MODEL),
            k_p.reshape(1, t, N_HEADS, HEAD_DIM), v_p.reshape(1, t, N_HEADS, HEAD_DIM),
            gv_last.reshape(1, GMLP_CHUNK, GMLP_W),
            k_s.reshape(nb, 1, N_HEADS, HEAD_DIM), v_s.reshape(nb, 1, N_HEADS, HEAD_DIM),
            gv_s.reshape(nb, 1, GMLP_W))
```

```python
import functools
import math

import jax
import jax.numpy as jnp
import numpy as np
from jax import lax
from jax.experimental import pallas as pl
from jax.experimental.pallas import tpu as pltpu

F32 = jnp.float32
BF16 = jnp.bfloat16

D_MODEL = 1024
N_HEADS = 8
HEAD_DIM = 64
ATTN_W = N_HEADS * HEAD_DIM
GMLP_W = 512
N_GROUPS = 8
GROUP_DIM = GMLP_W // N_GROUPS
GMLP_CHUNK = 128
MOBA_BLOCK = 256
MOBA_TOPK = 3
PAGE = 128
PEER_HEADS = 8
PEER_KEYS = 128
PEER_HALF = 128
PEER_TOPK = 16
N_EXPERTS = PEER_KEYS * PEER_KEYS
EPS = 1e-6
NEG = -1e30
QK_SCALE = HEAD_DIM ** -0.5
GELU_C = math.sqrt(2.0 / math.pi)

LANES = 128
VMEM_LIMIT = 56 * 1024 * 1024

_CAND_ROWS = tuple((a, PEER_TOPK // (a + 1)) for a in range(8))


def _dot(a, b):
    return jnp.dot(a, b, preferred_element_type=F32)


def _dot_nt(a, b):
    return lax.dot_general(a, b, (((1,), (1,)), ((), ())), preferred_element_type=F32)


def _split(a):
    hi = a.astype(BF16)
    lo = (a - hi.astype(F32)).astype(BF16)
    return hi, lo


def _dot3(a, b):
    ah, al = _split(a)
    bh, bl = _split(b)
    return _dot(ah, bh) + (_dot(ah, bl) + _dot(al, bh))


def _rms(x, axis=-1):
    return x * lax.rsqrt(jnp.mean(x * x, axis=axis, keepdims=True) + EPS)


def _gelu(x):
    return x * (0.5 * (1.0 + jnp.tanh(GELU_C * (x + 0.044715 * (x * x * x)))))


def _iota(shape, dim, dtype=jnp.int32):
    return lax.broadcasted_iota(dtype, shape, dim)


def _params(*sem):
    return pltpu.CompilerParams(dimension_semantics=sem, vmem_limit_bytes=VMEM_LIMIT)


def _full(shape):
    nd = len(shape)
    return pl.BlockSpec(shape, lambda *_: (0,) * nd)


_KP0, _KN0, _V0, _GU0, _GV0, _NAT_COLS = 0, 1024, 1536, 2048, 2560, 3072


def _gmlp_norm(gvr, bones_ref, gvg_ref):
    hi, lo = _split(gvr * gvr)
    ss = _dot(hi, bones_ref[...]) + _dot(lo, bones_ref[...])
    return gvr * lax.rsqrt(ss * (1.0 / GROUP_DIM) + EPS) * gvg_ref[...]


def _proj_prompt_kernel(x_ref, g1_ref, wnat_ref, wt_ref, qg_ref, kgp_ref, kgn_ref, gvg_ref, bones_ref,
                        wcat_ref, gbias_ref, gog_ref,
                        k_ref, v_ref, kaug_ref, qaug_ref, vt_ref, gm_ref, gvl_ref,
                        kmean_s, wtril_s):
    i = pl.program_id(0)
    tm = x_ref.shape[0]
    nblk = kmean_s.shape[1]

    @pl.when(i == 0)
    def _init():
        row = _iota((GMLP_CHUNK, N_GROUPS * GMLP_CHUNK), 0)
        col = _iota((GMLP_CHUNK, N_GROUPS * GMLP_CHUNK), 1) & (GMLP_CHUNK - 1)
        wtril_s[...] = jnp.where(col <= row, wcat_ref[...], 0.0).astype(BF16)
        kmean_s[...] = jnp.zeros(kmean_s.shape, F32)

    xn = (_rms(x_ref[...]) * g1_ref[...]).astype(BF16)
    hn = _dot(xn, wnat_ref[...])
    ht = _dot_nt(wt_ref[...], xn)

    lane = _iota((tm, LANES), 1)
    rowf = _iota((tm, LANES), 0).astype(F32)
    blk_row = _iota((nblk, tm), 0)
    blk_rowf = blk_row.astype(F32)
    const_rows = jnp.where(_iota((LANES, tm), 0) < 2, 1.0, 0.0)
    i_f = i.astype(F32)

    rs_heads = []
    for h in range(N_HEADS):
        kp = hn[:, _KP0 + LANES * h:_KP0 + LANES * (h + 1)]
        rs = lax.rsqrt(jnp.sum(kp * kp, axis=-1, keepdims=True) * (1.0 / HEAD_DIM) + EPS)
        rs_heads.append(rs)
        kn = kp * rs * kgp_ref[...]
        kmean_s[h, pl.ds(i, 1), :] = jnp.mean(kn, axis=0, keepdims=True)

        qt = ht[HEAD_DIM * h:HEAD_DIM * (h + 1), :]
        qn = qt * lax.rsqrt(jnp.sum(qt * qt, axis=0, keepdims=True) * (1.0 / HEAD_DIM) + EPS) * qg_ref[...]
        gate = _dot3(kmean_s[h], jnp.concatenate([qn, jnp.zeros_like(qn)], axis=0))
        cur = jnp.where(blk_row < i, gate, NEG)
        sel = jnp.zeros_like(cur)
        for _ in range(MOBA_TOPK):
            m = jnp.max(cur, axis=0, keepdims=True)
            first = jnp.min(jnp.where(cur == m, blk_rowf, float(nblk)), axis=0, keepdims=True)
            hit = blk_rowf == first
            sel = jnp.where(hit, 1.0, sel)
            cur = jnp.where(hit, -jnp.inf, cur)
        keep = jnp.where(blk_row < i, sel, jnp.where(blk_row == i, 1.0, 0.0))
        sel_bias = jnp.where(keep > 0.0, 0.0, NEG)
        pieces = [qn * QK_SCALE, sel_bias]
        if nblk < HEAD_DIM:
            pieces.append(jnp.zeros((HEAD_DIM - nblk, tm), F32))
        qaug_ref[h] = jnp.concatenate(pieces + [const_rows], axis=0).astype(BF16)

        slope = 2.0 ** -(h + 1)
        k_lo = jnp.where(lane == HEAD_DIM + i, 1.0, kn)
        k_hi = jnp.where(lane == 0, slope * rowf, jnp.where(lane == 1, (slope * MOBA_BLOCK) * i_f, 0.0))
        kaug_ref[h] = jnp.concatenate([k_lo, k_hi], axis=1).astype(BF16)
        vt_ref[h] = ht[ATTN_W + HEAD_DIM * h:ATTN_W + HEAD_DIM * (h + 1), :].astype(BF16)

    cols = []
    for c in range(ATTN_W // LANES):
        sc = jnp.where(lane < HEAD_DIM, rs_heads[2 * c], rs_heads[2 * c + 1])
        cols.append(hn[:, _KN0 + LANES * c:_KN0 + LANES * (c + 1)] * sc)
    k_ref[...] = jnp.concatenate(cols, axis=1) * kgn_ref[...]
    v_ref[...] = hn[:, _V0:_V0 + ATTN_W]

    gu = _gelu(hn[:, _GU0:_GU0 + GMLP_W])
    gvn = _gmlp_norm(_gelu(hn[:, _GV0:_GV0 + GMLP_W]), bones_ref, gvg_ref)
    gvl_ref[...] = gvn[tm - GMLP_CHUNK:, :]
    group_of_lane = _iota((1, GMLP_W), 1) >> 6
    outs = []
    for c in range(tm // GMLP_CHUNK):
        gc = gvn[GMLP_CHUNK * c:GMLP_CHUNK * (c + 1), :]
        stacked = jnp.concatenate([jnp.where(group_of_lane == g, gc, 0.0) for g in range(N_GROUPS)], axis=0)
        mixed = _dot(wtril_s[...], stacked.astype(BF16)) + gbias_ref[...]
        outs.append(gu[GMLP_CHUNK * c:GMLP_CHUNK * (c + 1), :] * mixed)
    gm_ref[...] = (_rms(jnp.concatenate(outs, axis=0)) * gog_ref[...]).astype(BF16)


def _proj_prompt(x, w, tm=MOBA_BLOCK):
    t = x.shape[0]
    nblk = t // tm
    assert nblk <= HEAD_DIM
    row_tile = lambda width: pl.BlockSpec((tm, width), lambda i: (i, 0))
    in_specs = [row_tile(D_MODEL), _full((1, D_MODEL)), _full((D_MODEL, _NAT_COLS)), _full((D_MODEL, D_MODEL)),
                _full((HEAD_DIM, 1)), _full((1, LANES)), _full((1, ATTN_W)), _full((1, GMLP_W)),
                _full((GMLP_W, GMLP_W)), _full((GMLP_CHUNK, N_GROUPS * GMLP_CHUNK)), _full((GMLP_CHUNK, GMLP_W)),
                _full((1, GMLP_W))]
    out_shape = (jax.ShapeDtypeStruct((t, ATTN_W), F32), jax.ShapeDtypeStruct((t, ATTN_W), F32),
                 jax.ShapeDtypeStruct((N_HEADS, t, 2 * LANES), BF16),
                 jax.ShapeDtypeStruct((N_HEADS, 2 * LANES, t), BF16),
                 jax.ShapeDtypeStruct((N_HEADS, HEAD_DIM, t), BF16),
                 jax.ShapeDtypeStruct((t, GMLP_W), BF16), jax.ShapeDtypeStruct((GMLP_CHUNK, GMLP_W), F32))
    out_specs = (row_tile(ATTN_W), row_tile(ATTN_W),
                 pl.BlockSpec((N_HEADS, tm, 2 * LANES), lambda i: (0, i, 0)),
                 pl.BlockSpec((N_HEADS, 2 * LANES, tm), lambda i: (0, 0, i)),
                 pl.BlockSpec((N_HEADS, HEAD_DIM, tm), lambda i: (0, 0, i)),
                 row_tile(GMLP_W), _full((GMLP_CHUNK, GMLP_W)))
    return pl.pallas_call(
        _proj_prompt_kernel, grid=(nblk,), in_specs=in_specs, out_specs=out_specs, out_shape=out_shape,
        scratch_shapes=[pltpu.VMEM((N_HEADS, nblk, LANES), F32),
                        pltpu.VMEM((GMLP_CHUNK, N_GROUPS * GMLP_CHUNK), BF16)],
        compiler_params=_params("arbitrary"), name="proj_prompt",
    )(x, w["g1"], w["wnat"], w["wt"], w["qg_col"], w["kg_pad"], w["kg_nat"], w["gvg"], w["bones"],
      w["wcat"], w["gbias"], w["gog"])


_MOBA_HEADS_PER_STEP = 2


def _moba_kernel(kaug_ref, qaug_ref, vt_ref, out_ref, s_even, s_odd):
    j = pl.program_id(1)
    tq = qaug_ref.shape[2]
    heads = range(_MOBA_HEADS_PER_STEP)

    def block_start(n):
        return pl.multiple_of(n * MOBA_BLOCK, MOBA_BLOCK)

    def produce(buf, n):
        for hh in heads:
            buf[hh] = _dot(kaug_ref[hh, pl.ds(block_start(n), MOBA_BLOCK), :], qaug_ref[hh])

    def fold(state, s, hh, n):
        m, l, acc = state
        m_new = jnp.maximum(m, jnp.max(s, axis=0, keepdims=True))
        alpha = jnp.exp(m - m_new)
        p = jnp.exp(s - m_new)
        l = l * alpha + jnp.sum(p, axis=0, keepdims=True)
        acc = acc * alpha + _dot(vt_ref[hh, :, pl.ds(block_start(n), MOBA_BLOCK)], p.astype(BF16))
        return m_new, l, acc

    def body(i, states):
        produce(s_odd, 2 * i + 1)
        states = tuple(fold(states[hh], s_even[hh], hh, 2 * i) for hh in heads)
        produce(s_even, 2 * i + 2)
        return tuple(fold(states[hh], s_odd[hh], hh, 2 * i + 1) for hh in heads)

    init = tuple((jnp.full((1, tq), -jnp.inf, F32), jnp.zeros((1, tq), F32), jnp.zeros((HEAD_DIM, tq), F32))
                 for _ in heads)
    produce(s_even, 0)
    states = lax.fori_loop(0, j // 2, body, init)

    causal = _iota((MOBA_BLOCK, tq), 0) <= _iota((MOBA_BLOCK, tq), 1)
    j_odd = (j & 1) == 1
    produce(s_odd, j)
    for hh in heads:
        first = jnp.where(j_odd, s_even[hh], jnp.where(causal, s_even[hh], NEG))
        state = fold(states[hh], first, hh, 2 * (j // 2))
        second = jnp.where(j_odd, jnp.where(causal, s_odd[hh], NEG), NEG)
        _, l, acc = fold(state, second, hh, j)
        out_ref[HEAD_DIM * hh:HEAD_DIM * (hh + 1), :] = acc / l


def _moba_prompt(kaug, qaug, vt):
    t = kaug.shape[1]
    tq = MOBA_BLOCK
    hs = _MOBA_HEADS_PER_STEP
    once = pl.Buffered(1)
    return pl.pallas_call(
        _moba_kernel, grid=(N_HEADS // hs, t // tq),
        in_specs=[pl.BlockSpec((hs, t, 2 * LANES), lambda g, j: (g, 0, 0), pipeline_mode=once),
                  pl.BlockSpec((hs, 2 * LANES, tq), lambda g, j: (g, 0, j)),
                  pl.BlockSpec((hs, HEAD_DIM, t), lambda g, j: (g, 0, 0), pipeline_mode=once)],
        out_specs=pl.BlockSpec((hs * HEAD_DIM, tq), lambda g, j: (g, j)),
        out_shape=jax.ShapeDtypeStruct((ATTN_W, t), F32),
        scratch_shapes=[pltpu.VMEM((hs, MOBA_BLOCK, tq), F32), pltpu.VMEM((hs, MOBA_BLOCK, tq), F32)],
        compiler_params=_params("parallel", "arbitrary"), name="moba_prompt",
    )(kaug, qaug, vt)


def _cand_index_table():
    rows = []
    for a, nb in _CAND_ROWS:
        for c0 in range(0, max(nb, 8), 8):
            rows.append([a * PEER_TOPK + c0 + b for b in range(8)])
    rows.append([(8 + a) * PEER_TOPK for a in range(8)])
    flat = np.asarray(rows, np.float32).reshape(-1, 1)
    return jnp.asarray(np.broadcast_to(flat, (flat.shape[0], LANES)).copy())


_N_CAND = 8 * (sum(max(nb, 8) // 8 for _, nb in _CAND_ROWS) + 1)


def _extract_top(cur, idx, n, on_hit):
    big = float(1 << 20)
    for r in range(n):
        m = jnp.max(cur, axis=0, keepdims=True)
        first = jnp.min(jnp.where(cur == m, idx, big), axis=0, keepdims=True)
        hit = idx == first
        cur = jnp.where(hit, -jnp.inf, cur)
        on_hit(r, m, hit)
    return cur


def _route_kernel(x_ref, attn_ref, gm_ref, aog_ref, woa_ref, wog_ref, g2_ref, wqt_ref, keys_ref, cidx_ref,
                  h_ref, xn2_ref, m1_ref, c1_ref, r2_ref, e2_ref,
                  s_s, rank_s, sv_s, na_s, *, attn_transposed):
    tt = x_ref.shape[0]
    at = attn_ref[...]
    if attn_transposed:
        at = at.T
    an = (_rms(at) * aog_ref[...]).astype(BF16)
    h = x_ref[...] + _dot(an, woa_ref[...]) + _dot(gm_ref[...], wog_ref[...])
    h_ref[...] = h
    xn2 = (_rms(h) * g2_ref[...]).astype(BF16)
    xn2_ref[...] = xn2
    qt = _dot_nt(wqt_ref[...], xn2)
    for hp in range(2 * PEER_HEADS):
        s_s[hp] = _dot3(keys_ref[hp], qt[PEER_HALF * hp:PEER_HALF * (hp + 1), :])

    key_idx = _iota((PEER_KEYS, LANES), 0).astype(F32)
    row8 = _iota((8, LANES), 0)

    def rank_distinct(hp, lanes):
        cur = s_s[hp, :, lanes]
        rank = jnp.full((PEER_KEYS, LANES), float(PEER_TOPK), F32)
        for r in range(PEER_TOPK):
            m = jnp.max(cur, axis=0, keepdims=True)
            hit = cur == m
            cur = jnp.where(hit, -jnp.inf, cur)
            rank = jnp.where(hit, float(r), rank)
            sv_s[hp, r:r + 1, lanes] = m
        rank_s[hp, :, lanes] = rank
        return jnp.sum(jnp.where(rank < float(PEER_TOPK), 1.0, 0.0), axis=0, keepdims=True)

    def rank_exact(hp, lanes):
        rank = [jnp.full((PEER_KEYS, LANES), float(PEER_TOPK), F32)]

        def on_hit(r, m, hit):
            rank[0] = jnp.where(hit, float(r), rank[0])
            sv_s[hp, r:r + 1, lanes] = m

        _extract_top(s_s[hp, :, lanes], key_idx, PEER_TOPK, on_hit)
        rank_s[hp, :, lanes] = rank[0]

    def per_head(hd, carry):
        ranked = []
        for sl in range(tt // LANES):
            lanes = slice(LANES * sl, LANES * (sl + 1))
            for hp in (2 * hd, 2 * hd + 1):
                ranked.append((hp, lanes, rank_distinct(hp, lanes)))
        for hp, lanes, n_ranked in ranked:
            @pl.when(jnp.max(jnp.abs(n_ranked - float(PEER_TOPK))) > 0.0)
            def _redo(hp=hp, lanes=lanes):
                rank_exact(hp, lanes)
        for sl in range(tt // LANES):
            lanes = slice(LANES * sl, LANES * (sl + 1))
            sv1 = sv_s[2 * hd, :, lanes]
            sv2 = sv_s[2 * hd + 1, :, lanes]
            pieces = []
            for a, nb in _CAND_ROWS:
                for c0 in range(0, max(nb, 8), 8):
                    piece = sv1[a:a + 1, :] + sv2[c0:c0 + 8, :]
                    pieces.append(piece if nb >= 8 else jnp.where(row8 < nb, piece, -jnp.inf))
            pieces.append(sv1[8:16, :] + sv2[0:1, :])
            cand = jnp.concatenate(pieces, axis=0)
            chosen = [jnp.zeros_like(cand)]

            def on_hit(r, m, hit):
                chosen[0] = jnp.where(hit, 1.0, chosen[0])

            _extract_top(cand, cidx_ref[...], PEER_TOPK, on_hit)
            chosen = chosen[0]
            top = sv1[0:1, :] + sv2[0:1, :]
            z = jnp.sum(jnp.where(chosen > 0.0, jnp.exp(cand - top), 0.0), axis=0, keepdims=True)
            row = 0
            for a, nb in _CAND_ROWS:
                nrows = max(nb, 8)
                na_s[sl, a:a + 1, :] = jnp.sum(chosen[row:row + nrows, :], axis=0, keepdims=True)
                row += nrows
            na_s[sl, 8:16, :] = chosen[row:row + 8, :]
            na = na_s[sl]
            rank1 = rank_s[2 * hd, :, lanes]
            m1 = jnp.zeros((PEER_KEYS, LANES), F32)
            for a in range(PEER_TOPK):
                m1 = jnp.where(rank1 == float(a), na[a:a + 1, :], m1)
            m1_ref[hd, :, lanes] = m1
            c1_ref[hd, :, lanes] = jnp.exp(s_s[2 * hd, :, lanes] - sv1[0:1, :]) / z
            r2_ref[hd, :, lanes] = rank_s[2 * hd + 1, :, lanes].astype(BF16)
            e2_ref[hd, :, lanes] = jnp.exp(s_s[2 * hd + 1, :, lanes] - sv2[0:1, :]).astype(BF16)
        return carry

    lax.fori_loop(0, PEER_HEADS, per_head, 0)


def _route(x, attn, gm, w, *, tt, attn_transposed):
    t = x.shape[0]
    row_tile = lambda width: pl.BlockSpec((tt, width), lambda i: (i, 0))
    attn_spec = pl.BlockSpec((ATTN_W, tt), lambda i: (0, i)) if attn_transposed else row_tile(ATTN_W)
    head_tile = pl.BlockSpec((PEER_HEADS, PEER_KEYS, tt), lambda i: (0, 0, i))
    in_specs = [row_tile(D_MODEL), attn_spec, row_tile(GMLP_W), _full((1, ATTN_W)),
                _full((ATTN_W, D_MODEL)), _full((GMLP_W, D_MODEL)), _full((1, D_MODEL)),
                _full((2 * PEER_HEADS * PEER_HALF, D_MODEL)), _full((2 * PEER_HEADS, PEER_KEYS, PEER_HALF)),
                _full((_N_CAND, LANES))]
    stat = lambda dt: jax.ShapeDtypeStruct((PEER_HEADS, PEER_KEYS, t), dt)
    out_shape = (jax.ShapeDtypeStruct((t, D_MODEL), F32), jax.ShapeDtypeStruct((t, D_MODEL), BF16),
                 stat(F32), stat(F32), stat(BF16), stat(BF16))
    out_specs = (row_tile(D_MODEL), row_tile(D_MODEL), head_tile, head_tile, head_tile, head_tile)
    return pl.pallas_call(
        functools.partial(_route_kernel, attn_transposed=attn_transposed),
        grid=(t // tt,), in_specs=in_specs, out_specs=out_specs, out_shape=out_shape,
        scratch_shapes=[pltpu.VMEM((2 * PEER_HEADS, PEER_KEYS, tt), F32),
                        pltpu.VMEM((2 * PEER_HEADS, PEER_KEYS, tt), F32),
                        pltpu.VMEM((2 * PEER_HEADS, PEER_TOPK, tt), F32),
                        pltpu.VMEM((tt // LANES, PEER_TOPK, LANES), F32)],
        compiler_params=_params("parallel"), name="route",
    )(x, attn, gm, w["aog"], w["wo_attn"], w["wo_gmlp"], w["g2"], w["wqt"], w["keys"], w["cand_idx"])


def _peer_kernel(xn2_ref, u_ref, vt_ref, m1_ref, c1_ref, r2_ref, e2_ref, h_ref, y_ref, acc_s, act_even_s, act_odd_s,
                 *, keys_per_step):
    c = pl.program_id(1)
    n_blocks = pl.num_programs(1) - 1

    @pl.when(c == 0)
    def _zero():
        acc_s[...] = jnp.zeros(acc_s.shape, F32)
        act_odd_s[...] = jnp.zeros(act_odd_s.shape, BF16)

    def step(act_write, act_read):
        act_write[...] = _gelu(_dot_nt(u_ref[...], xn2_ref[...])).astype(BF16)
        first_key = jnp.maximum(c - 1, 0) * keys_per_step
        parts = []
        for ii in range(keys_per_step):
            i = first_key + ii
            g = None
            for hd in range(PEER_HEADS):
                partners = m1_ref[hd, pl.ds(i, 1), :].astype(BF16)
                weight = c1_ref[hd, pl.ds(i, 1), :].astype(BF16)
                term = jnp.where(r2_ref[hd] < partners, e2_ref[hd], jnp.zeros((), BF16)) * weight
                g = term if g is None else g + term
            parts.append(g * act_read[PEER_KEYS * ii:PEER_KEYS * (ii + 1), :])
        acc_s[...] += _dot(vt_ref[...], jnp.concatenate(parts, axis=0))

    @pl.when((c & 1) == 0)
    def _even():
        step(act_even_s, act_odd_s)

    @pl.when((c & 1) == 1)
    def _odd():
        step(act_odd_s, act_even_s)

    @pl.when(c == n_blocks)
    def _finish():
        y_ref[...] = h_ref[...] + acc_s[...].T


def _peer(xn2, h, m1, c1, r2, e2, w, *, tt, keys_per_step=2):
    t = xn2.shape[0]
    ne = keys_per_step * PEER_KEYS
    n_blocks = N_EXPERTS // ne
    head_tile = pl.BlockSpec((PEER_HEADS, PEER_KEYS, tt), lambda j, c: (0, 0, j))
    row_tile = pl.BlockSpec((tt, D_MODEL), lambda j, c: (j, 0))
    return pl.pallas_call(
        functools.partial(_peer_kernel, keys_per_step=keys_per_step),
        grid=(t // tt, n_blocks + 1),
        in_specs=[row_tile, pl.BlockSpec((ne, D_MODEL), lambda j, c: (jnp.minimum(c, n_blocks - 1), 0)),
                  pl.BlockSpec((D_MODEL, ne), lambda j, c: (0, jnp.maximum(c - 1, 0))),
                  head_tile, head_tile, head_tile, head_tile, row_tile],
        out_specs=row_tile, out_shape=jax.ShapeDtypeStruct((t, D_MODEL), F32),
        scratch_shapes=[pltpu.VMEM((D_MODEL, tt), F32), pltpu.VMEM((ne, tt), BF16), pltpu.VMEM((ne, tt), BF16)],
        compiler_params=_params("parallel", "arbitrary"), name="peer",
    )(xn2, w["u"], w["vt"], m1, c1, r2, e2, h)


def _proj_sample_kernel(x_ref, g1_ref, win_ref, bones_ref, qgn_ref, kgn_ref, gvg_ref, wsc_ref, bsc_ref, gog_ref,
                        k_ref, v_ref, gv_ref, gm_ref, q_ref):
    xn = (_rms(x_ref[...]) * g1_ref[...]).astype(BF16)
    hn = _dot(xn, win_ref[...])

    def head_norm(z, g_ref):
        hi, lo = _split(z * z)
        ss = _dot(hi, bones_ref[...]) + _dot(lo, bones_ref[...])
        return z * lax.rsqrt(ss * (1.0 / HEAD_DIM) + EPS) * g_ref[...]

    qn = head_norm(hn[:, 0:ATTN_W], qgn_ref)
    kn = head_norm(hn[:, ATTN_W:2 * ATTN_W], kgn_ref)
    k_ref[...] = kn
    v_ref[...] = hn[:, 2 * ATTN_W:3 * ATTN_W]
    gu = _gelu(hn[:, 3 * ATTN_W:3 * ATTN_W + GMLP_W])
    gvn = _gmlp_norm(_gelu(hn[:, 3 * ATTN_W + GMLP_W:]), bones_ref, gvg_ref)
    gv_ref[...] = gvn
    gm_ref[...] = (_rms(gu * (wsc_ref[...] * gvn + bsc_ref[...])) * gog_ref[...]).astype(BF16)
    q_ref[...] = qn * QK_SCALE


def _proj_sample(x, w):
    nb = x.shape[0]
    out_shape = (jax.ShapeDtypeStruct((nb, ATTN_W), F32), jax.ShapeDtypeStruct((nb, ATTN_W), F32),
                 jax.ShapeDtypeStruct((nb, GMLP_W), F32), jax.ShapeDtypeStruct((nb, GMLP_W), BF16),
                 jax.ShapeDtypeStruct((nb, ATTN_W), F32))
    return pl.pallas_call(
        _proj_sample_kernel, out_shape=out_shape, compiler_params=_params(), name="proj_sample",
    )(x, w["g1"], w["win"], w["bones"], w["qg_nat"], w["kg_nat"], w["gvg"], w["ws_one"], w["b_one"], w["gog"])


_PAGES_PER_STEP = 16
_PAGE_BLOCK = (None, PAGE, N_HEADS, HEAD_DIM)


def _sample_select_kernel(pt_ref, q_ref, *refs, n_steps):
    k_refs = refs[:_PAGES_PER_STEP]
    sel_ref, gate_s = refs[_PAGES_PER_STEP:]
    c = pl.program_id(1)
    lane = _iota((N_HEADS, LANES), 1)
    blocks_per_step = _PAGES_PER_STEP // 2

    @pl.when(c == 0)
    def _init():
        gate_s[...] = jnp.zeros(gate_s.shape, F32)

    q = q_ref[...]
    gates = gate_s[...]
    for b in range(blocks_per_step):
        ksum = jnp.sum(k_refs[2 * b][...], axis=0) + jnp.sum(k_refs[2 * b + 1][...], axis=0)
        gate = jnp.sum(ksum * q, axis=-1, keepdims=True) * (1.0 / MOBA_BLOCK)
        gates = jnp.where(lane == c * blocks_per_step + b, gate, gates)
    gate_s[...] = gates

    @pl.when(c == n_steps - 1)
    def _select():
        n_blocks = n_steps * blocks_per_step
        lane_f = lane.astype(F32)
        cur = jnp.where(lane < n_blocks, gates, -jnp.inf)
        sel = jnp.zeros_like(cur)
        for _ in range(MOBA_TOPK):
            m = jnp.max(cur, axis=-1, keepdims=True)
            first = jnp.min(jnp.where(cur == m, lane_f, float(LANES)), axis=-1, keepdims=True)
            hit = lane_f == first
            sel = jnp.where(hit, 1.0, sel)
            cur = jnp.where(hit, -jnp.inf, cur)
        sel_ref[...] = sel


def _sample_select(page_table, cache_k, q):
    nb, n_pages = page_table.shape
    n_steps = n_pages // _PAGES_PER_STEP
    assert n_pages // 2 <= LANES
    page_spec = lambda r: pl.BlockSpec(
        _PAGE_BLOCK, lambda d, c, pt, r=r: (pt[d, c * _PAGES_PER_STEP + r], 0, 0, 0))
    per_seq = lambda shape: pl.BlockSpec((None,) + shape, lambda d, c, pt: (d,) + (0,) * len(shape))
    grid_spec = pltpu.PrefetchScalarGridSpec(
        num_scalar_prefetch=1, grid=(nb, n_steps),
        in_specs=[per_seq((N_HEADS, HEAD_DIM))] + [page_spec(r) for r in range(_PAGES_PER_STEP)],
        out_specs=per_seq((N_HEADS, LANES)),
        scratch_shapes=[pltpu.VMEM((N_HEADS, LANES), F32)])
    return pl.pallas_call(
        functools.partial(_sample_select_kernel, n_steps=n_steps),
        grid_spec=grid_spec, out_shape=jax.ShapeDtypeStruct((nb, N_HEADS, LANES), F32),
        compiler_params=_params("parallel", "arbitrary"), name="sample_select",
    )(page_table, q, *([cache_k] * _PAGES_PER_STEP))


def _sample_attend_kernel(pt_ref, bits_ref, q_ref, knew_ref, vnew_ref, *refs, n_steps, past_len):
    k_refs = refs[:_PAGES_PER_STEP]
    v_refs = refs[_PAGES_PER_STEP:2 * _PAGES_PER_STEP]
    out_ref, m_s, l_s, acc_s = refs[2 * _PAGES_PER_STEP:]
    d = pl.program_id(0)
    c = pl.program_id(1)
    q = q_ref[...]

    @pl.when(c == 0)
    def _self_term():
        m_s[...] = jnp.broadcast_to(jnp.sum(q * knew_ref[...], axis=-1, keepdims=True), m_s.shape)
        l_s[...] = jnp.ones(l_s.shape, F32)
        acc_s[...] = vnew_ref[...]

    head = _iota((1, N_HEADS, 1), 1)
    slopes = jnp.exp2(-(head + 1).astype(F32))
    row = _iota((PAGE, N_HEADS, 1), 0).astype(F32)
    pages_per_block = MOBA_BLOCK // PAGE
    in_block = [slopes * (row + float(PAGE * g)) for g in range(pages_per_block)]
    ones = jnp.ones((HEAD_DIM, HEAD_DIM), BF16)
    for b in range(_PAGES_PER_STEP // pages_per_block):
        first_page = c * _PAGES_PER_STEP + b * pages_per_block
        bits = bits_ref[d, first_page]

        @pl.when(bits != 0)
        def _block(b=b, first_page=first_page, bits=bits):
            chose = ((bits >> head) & 1) == 1
            offset = jnp.where(chose, slopes * (float(past_len) - (first_page * PAGE).astype(F32)), -NEG)
            rels = []
            for g in range(pages_per_block):
                k = k_refs[b * pages_per_block + g][...]
                qk = (k * q[None]).reshape(PAGE * N_HEADS, HEAD_DIM).astype(BF16)
                rels.append(_dot(qk, ones).reshape(PAGE, N_HEADS, HEAD_DIM) + in_block[g])
            m_old = m_s[...][None]
            top = functools.reduce(jnp.maximum, [jnp.max(rel, axis=0, keepdims=True) for rel in rels])
            m_new = jnp.maximum(m_old, top - offset)
            alpha = jnp.exp(m_old - m_new)
            shift = m_new + offset
            l_new = l_s[...][None] * alpha
            acc_new = acc_s[...][None] * alpha
            for g, rel in enumerate(rels):
                p = jnp.exp(rel - shift)
                l_new = l_new + jnp.sum(p, axis=0, keepdims=True)
                acc_new = acc_new + jnp.sum(p * v_refs[b * pages_per_block + g][...], axis=0, keepdims=True)
            m_s[...] = m_new[0]
            l_s[...] = l_new[0]
            acc_s[...] = acc_new[0]

    @pl.when(c == n_steps - 1)
    def _finish():
        out_ref[...] = acc_s[...] / l_s[...]


def _sample_attend(pages_eff, head_bits, cache_k, cache_v, q, k_new, v_new, past_len):
    nb, n_pages = head_bits.shape
    n_steps = n_pages // _PAGES_PER_STEP
    page_spec = lambda r: pl.BlockSpec(
        _PAGE_BLOCK, lambda d, c, pt, hb, r=r: (pt[d, c * _PAGES_PER_STEP + r], 0, 0, 0))
    per_seq = pl.BlockSpec((None, N_HEADS, HEAD_DIM), lambda d, c, pt, hb: (d, 0, 0))
    pages = [page_spec(r) for r in range(_PAGES_PER_STEP)]
    grid_spec = pltpu.PrefetchScalarGridSpec(
        num_scalar_prefetch=2, grid=(nb, n_steps),
        in_specs=[per_seq, per_seq, per_seq] + pages + pages,
        out_specs=per_seq,
        scratch_shapes=[pltpu.VMEM((N_HEADS, HEAD_DIM), F32)] * 3)
    return pl.pallas_call(
        functools.partial(_sample_attend_kernel, n_steps=n_steps, past_len=past_len),
        grid_spec=grid_spec, out_shape=jax.ShapeDtypeStruct((nb, N_HEADS, HEAD_DIM), F32),
        compiler_params=_params("parallel", "arbitrary"), name="sample_attend",
    )(pages_eff, head_bits, q, k_new, v_new, *([cache_k] * _PAGES_PER_STEP), *([cache_v] * _PAGES_PER_STEP))


def _prepare(norm1_g, w_in, q_norm_g, k_norm_g, gmlp_v_norm_g, gmlp_ws, gmlp_b, attn_out_norm_g, gmlp_out_norm_g,
             w_out, norm2_g, peer_wq, peer_keys, peer_u, peer_v):
    wq, wk, wv, wgu, wgv = jnp.split(w_in, [ATTN_W, 2 * ATTN_W, 3 * ATTN_W, 3 * ATTN_W + GMLP_W], axis=1)
    wk_pad = jnp.pad(wk.reshape(D_MODEL, N_HEADS, HEAD_DIM), ((0, 0), (0, 0), (0, LANES - HEAD_DIM)))
    feat = np.arange(ATTN_W)
    w = {
        "g1": norm1_g.reshape(1, D_MODEL),
        "win": w_in.astype(BF16),
        "wnat": jnp.concatenate([wk_pad.reshape(D_MODEL, N_HEADS * LANES), wk, wv, wgu, wgv], axis=1).astype(BF16),
        "wt": jnp.concatenate([wq, wv], axis=1).T.astype(BF16),
        "qg_col": q_norm_g.reshape(HEAD_DIM, 1),
        "kg_pad": jnp.pad(k_norm_g, (0, LANES - HEAD_DIM)).reshape(1, LANES),
        "qg_nat": jnp.tile(q_norm_g, N_HEADS).reshape(1, ATTN_W),
        "kg_nat": jnp.tile(k_norm_g, N_HEADS).reshape(1, ATTN_W),
        "gvg": gmlp_v_norm_g.reshape(1, GMLP_W),
        "bones": jnp.asarray((feat[:, None] // GROUP_DIM) == (feat[None, :] // GROUP_DIM), BF16),
        "wcat": jnp.transpose(gmlp_ws, (1, 0, 2)).reshape(GMLP_CHUNK, N_GROUPS * GMLP_CHUNK),
        "gbias": jnp.repeat(gmlp_b.T, GROUP_DIM, axis=1),
        "ws_one": jnp.repeat(gmlp_ws[:, 0, 0], GROUP_DIM).reshape(1, GMLP_W),
        "b_one": jnp.repeat(gmlp_b[:, 0], GROUP_DIM).reshape(1, GMLP_W),
        "gog": gmlp_out_norm_g.reshape(1, GMLP_W),
        "aog": attn_out_norm_g.reshape(1, ATTN_W),
        "wo_attn": w_out[:ATTN_W].astype(BF16),
        "wo_gmlp": w_out[ATTN_W:].astype(BF16),
        "g2": norm2_g.reshape(1, D_MODEL),
        "wqt": peer_wq.T.astype(BF16),
        "keys": peer_keys.reshape(2 * PEER_HEADS, PEER_KEYS, PEER_HALF),
        "u": peer_u.astype(BF16),
        "vt": peer_v.T.astype(BF16),
        "cand_idx": _cand_index_table(),
    }
    return w


def _fill_forward_pages(page_table, flags):
    nb, n_pages = page_table.shape
    n_steps = n_pages // _PAGES_PER_STEP
    pt = page_table.reshape(nb * n_steps, _PAGES_PER_STEP)
    fl = flags.reshape(nb * n_steps, _PAGES_PER_STEP)
    step = jnp.arange(nb * n_steps, dtype=jnp.int32)[:, None]
    last = lax.cummax(jnp.where(fl > 0, step, -1), axis=0)
    eff = jnp.take_along_axis(pt, jnp.maximum(last, 0), axis=0)
    return eff.reshape(nb, n_pages)


def _layer_tail(x, attn, gm, w, *, route_tile, peer_tile, attn_transposed):
    h, xn2, m1, c1, r2, e2 = _route(x, attn, gm, w, tt=route_tile, attn_transposed=attn_transposed)
    return _peer(xn2, h, m1, c1, r2, e2, w, tt=peer_tile)


def kernel(x_prompt, x_sample, cache_k, cache_v, page_table, norm1_g, w_in, q_norm_g, k_norm_g, gmlp_v_norm_g,
           gmlp_ws, gmlp_b, attn_out_norm_g, gmlp_out_norm_g, w_out, norm2_g, peer_wq, peer_keys, peer_u, peer_v):
    w = _prepare(norm1_g, w_in, q_norm_g, k_norm_g, gmlp_v_norm_g, gmlp_ws, gmlp_b, attn_out_norm_g,
                 gmlp_out_norm_g, w_out, norm2_g, peer_wq, peer_keys, peer_u, peer_v)
    b, t, _ = x_prompt.shape
    assert b == 1
    xp = x_prompt.reshape(t, D_MODEL)
    k_p, v_p, kaug, qaug, vt, gm_p, gv_last = _proj_prompt(xp, w)
    attn_t = _moba_prompt(kaug, qaug, vt)
    y_p = _layer_tail(xp, attn_t, gm_p, w, route_tile=256, peer_tile=512, attn_transposed=True)

    nb, ds, _ = x_sample.shape
    assert ds == 1
    xs = x_sample.reshape(nb, D_MODEL)
    past_len = page_table.shape[1] * PAGE
    assert past_len % MOBA_BLOCK == 0
    k_s, v_s, gv_s, gm_s, q_s = _proj_sample(xs, w)
    heads = lambda a: a.reshape(nb, N_HEADS, HEAD_DIM)
    sel = _sample_select(page_table, cache_k, heads(q_s))
    n_blocks = past_len // MOBA_BLOCK
    chosen = (sel[:, :, :n_blocks] > 0.0).astype(jnp.int32)
    block_bits = jnp.sum(chosen << jnp.arange(N_HEADS, dtype=jnp.int32)[None, :, None], axis=1)
    head_bits = jnp.repeat(block_bits, MOBA_BLOCK // PAGE, axis=1)
    pages_eff = _fill_forward_pages(page_table, head_bits)
    attn_s = _sample_attend(pages_eff, head_bits, cache_k, cache_v, heads(q_s), heads(k_s), heads(v_s), past_len)
    y_s = _layer_tail(xs, attn_s.reshape(nb, ATTN_W), gm_s, w, route_tile=nb, peer_tile=nb, attn_transposed=False)

    return (y_p.reshape(1, t, D_MODEL), y_s.reshape(nb, 1, D_MODEL),
            k_p.reshape(1, t, N_HEADS, HEAD_DIM), v_p.reshape(1, t, N_HEADS, HEAD_DIM),
            gv_last.reshape(1, GMLP_CHUNK, GMLP_W),
            k_s.reshape(nb, 1, N_HEADS, HEAD_DIM), v_s.reshape(nb, 1, N_HEADS, HEAD_DIM),
            gv_s.reshape(nb, 1, GMLP_W))
```

```python
import functools
import math

import jax
import jax.numpy as jnp
import numpy as np
from jax import lax
from jax.experimental import pallas as pl
from jax.experimental.pallas import tpu as pltpu

F32 = jnp.float32
BF16 = jnp.bfloat16

D_MODEL = 1024
N_HEADS = 8
HEAD_DIM = 64
ATTN_W = N_HEADS * HEAD_DIM
GMLP_W = 512
N_GROUPS = 8
GROUP_DIM = GMLP_W // N_GROUPS
GMLP_CHUNK = 128
MOBA_BLOCK = 256
MOBA_TOPK = 3
PAGE = 128
PEER_HEADS = 8
PEER_KEYS = 128
PEER_HALF = 128
PEER_TOPK = 16
N_EXPERTS = PEER_KEYS * PEER_KEYS
EPS = 1e-6
NEG = -1e30
QK_SCALE = HEAD_DIM ** -0.5
GELU_C = math.sqrt(2.0 / math.pi)

LANES = 128
VMEM_LIMIT = 56 * 1024 * 1024

_CAND_ROWS = tuple((a, PEER_TOPK // (a + 1)) for a in range(8))


def _dot(a, b):
    return jnp.dot(a, b, preferred_element_type=F32)


def _dot_nt(a, b):
    return lax.dot_general(a, b, (((1,), (1,)), ((), ())), preferred_element_type=F32)


def _split(a):
    hi = a.astype(BF16)
    lo = (a - hi.astype(F32)).astype(BF16)
    return hi, lo


def _dot3(a, b):
    ah, al = _split(a)
    bh, bl = _split(b)
    return _dot(ah, bh) + (_dot(ah, bl) + _dot(al, bh))


def _rms(x, axis=-1):
    return x * lax.rsqrt(jnp.mean(x * x, axis=axis, keepdims=True) + EPS)


def _gelu(x):
    return x * (0.5 * (1.0 + jnp.tanh(GELU_C * (x + 0.044715 * (x * x * x)))))


def _iota(shape, dim, dtype=jnp.int32):
    return lax.broadcasted_iota(dtype, shape, dim)


def _params(*sem):
    return pltpu.CompilerParams(dimension_semantics=sem, vmem_limit_bytes=VMEM_LIMIT)


def _full(shape):
    nd = len(shape)
    return pl.BlockSpec(shape, lambda *_: (0,) * nd)


_KP0, _KN0, _V0, _GU0, _GV0, _NAT_COLS = 0, 1024, 1536, 2048, 2560, 3072


def _gmlp_norm(gvr, bones_ref, gvg_ref):
    hi, lo = _split(gvr * gvr)
    ss = _dot(hi, bones_ref[...]) + _dot(lo, bones_ref[...])
    return gvr * lax.rsqrt(ss * (1.0 / GROUP_DIM) + EPS) * gvg_ref[...]


def _proj_prompt_kernel(x_ref, g1_ref, wnat_ref, wt_ref, qg_ref, kgp_ref, kgn_ref, gvg_ref, bones_ref,
                        wcat_ref, gbias_ref, gog_ref,
                        k_ref, v_ref, kaug_ref, qaug_ref, vt_ref, gm_ref, gvl_ref,
                        kmean_s, wtril_s):
    i = pl.program_id(0)
    tm = x_ref.shape[0]
    nblk = kmean_s.shape[1]

    @pl.when(i == 0)
    def _init():
        row = _iota((GMLP_CHUNK, N_GROUPS * GMLP_CHUNK), 0)
        col = _iota((GMLP_CHUNK, N_GROUPS * GMLP_CHUNK), 1) & (GMLP_CHUNK - 1)
        wtril_s[...] = jnp.where(col <= row, wcat_ref[...], 0.0).astype(BF16)
        kmean_s[...] = jnp.zeros(kmean_s.shape, F32)

    xn = (_rms(x_ref[...]) * g1_ref[...]).astype(BF16)
    hn = _dot(xn, wnat_ref[...])
    ht = _dot_nt(wt_ref[...], xn)

    lane = _iota((tm, LANES), 1)
    rowf = _iota((tm, LANES), 0).astype(F32)
    blk_row = _iota((nblk, tm), 0)
    blk_rowf = blk_row.astype(F32)
    const_rows = jnp.where(_iota((LANES, tm), 0) < 2, 1.0, 0.0)
    i_f = i.astype(F32)

    rs_heads = []
    for h in range(N_HEADS):
        kp = hn[:, _KP0 + LANES * h:_KP0 + LANES * (h + 1)]
        rs = lax.rsqrt(jnp.sum(kp * kp, axis=-1, keepdims=True) * (1.0 / HEAD_DIM) + EPS)
        rs_heads.append(rs)
        kn = kp * rs * kgp_ref[...]
        kmean_s[h, pl.ds(i, 1), :] = jnp.mean(kn, axis=0, keepdims=True)

        qt = ht[HEAD_DIM * h:HEAD_DIM * (h + 1), :]
        qn = qt * lax.rsqrt(jnp.sum(qt * qt, axis=0, keepdims=True) * (1.0 / HEAD_DIM) + EPS) * qg_ref[...]
        gate = _dot3(kmean_s[h], jnp.concatenate([qn, jnp.zeros_like(qn)], axis=0))
        cur = jnp.where(blk_row < i, gate, NEG)
        sel = jnp.zeros_like(cur)
        for _ in range(MOBA_TOPK):
            m = jnp.max(cur, axis=0, keepdims=True)
            first = jnp.min(jnp.where(cur == m, blk_rowf, float(nblk)), axis=0, keepdims=True)
            hit = blk_rowf == first
            sel = jnp.where(hit, 1.0, sel)
            cur = jnp.where(hit, -jnp.inf, cur)
        keep = jnp.where(blk_row < i, sel, jnp.where(blk_row == i, 1.0, 0.0))
        sel_bias = jnp.where(keep > 0.0, 0.0, NEG)
        pieces = [qn * QK_SCALE, sel_bias]
        if nblk < HEAD_DIM:
            pieces.append(jnp.zeros((HEAD_DIM - nblk, tm), F32))
        qaug_ref[h] = jnp.concatenate(pieces + [const_rows], axis=0).astype(BF16)

        slope = 2.0 ** -(h + 1)
        k_lo = jnp.where(lane == HEAD_DIM + i, 1.0, kn)
        k_hi = jnp.where(lane == 0, slope * rowf, jnp.where(lane == 1, (slope * MOBA_BLOCK) * i_f, 0.0))
        kaug_ref[h] = jnp.concatenate([k_lo, k_hi], axis=1).astype(BF16)
        vt_ref[h] = ht[ATTN_W + HEAD_DIM * h:ATTN_W + HEAD_DIM * (h + 1), :].astype(BF16)

    cols = []
    for c in range(ATTN_W // LANES):
        sc = jnp.where(lane < HEAD_DIM, rs_heads[2 * c], rs_heads[2 * c + 1])
        cols.append(hn[:, _KN0 + LANES * c:_KN0 + LANES * (c + 1)] * sc)
    k_ref[...] = jnp.concatenate(cols, axis=1) * kgn_ref[...]
    v_ref[...] = hn[:, _V0:_V0 + ATTN_W]

    gu = _gelu(hn[:, _GU0:_GU0 + GMLP_W])
    gvn = _gmlp_norm(_gelu(hn[:, _GV0:_GV0 + GMLP_W]), bones_ref, gvg_ref)
    gvl_ref[...] = gvn[tm - GMLP_CHUNK:, :]
    group_of_lane = _iota((1, GMLP_W), 1) >> 6
    outs = []
    for c in range(tm // GMLP_CHUNK):
        gc = gvn[GMLP_CHUNK * c:GMLP_CHUNK * (c + 1), :]
        stacked = jnp.concatenate([jnp.where(group_of_lane == g, gc, 0.0) for g in range(N_GROUPS)], axis=0)
        mixed = _dot(wtril_s[...], stacked.astype(BF16)) + gbias_ref[...]
        outs.append(gu[GMLP_CHUNK * c:GMLP_CHUNK * (c + 1), :] * mixed)
    gm_ref[...] = (_rms(jnp.concatenate(outs, axis=0)) * gog_ref[...]).astype(BF16)


def _proj_prompt(x, w, tm=MOBA_BLOCK):
    t = x.shape[0]
    nblk = t // tm
    assert nblk <= HEAD_DIM
    row_tile = lambda width: pl.BlockSpec((tm, width), lambda i: (i, 0))
    in_specs = [row_tile(D_MODEL), _full((1, D_MODEL)), _full((D_MODEL, _NAT_COLS)), _full((D_MODEL, D_MODEL)),
                _full((HEAD_DIM, 1)), _full((1, LANES)), _full((1, ATTN_W)), _full((1, GMLP_W)),
                _full((GMLP_W, GMLP_W)), _full((GMLP_CHUNK, N_GROUPS * GMLP_CHUNK)), _full((GMLP_CHUNK, GMLP_W)),
                _full((1, GMLP_W))]
    out_shape = (jax.ShapeDtypeStruct((t, ATTN_W), F32), jax.ShapeDtypeStruct((t, ATTN_W), F32),
                 jax.ShapeDtypeStruct((N_HEADS, t, 2 * LANES), BF16),
                 jax.ShapeDtypeStruct((N_HEADS, 2 * LANES, t), BF16),
                 jax.ShapeDtypeStruct((N_HEADS, HEAD_DIM, t), BF16),
                 jax.ShapeDtypeStruct((t, GMLP_W), BF16), jax.ShapeDtypeStruct((GMLP_CHUNK, GMLP_W), F32))
    out_specs = (row_tile(ATTN_W), row_tile(ATTN_W),
                 pl.BlockSpec((N_HEADS, tm, 2 * LANES), lambda i: (0, i, 0)),
                 pl.BlockSpec((N_HEADS, 2 * LANES, tm), lambda i: (0, 0, i)),
                 pl.BlockSpec((N_HEADS, HEAD_DIM, tm), lambda i: (0, 0, i)),
                 row_tile(GMLP_W), _full((GMLP_CHUNK, GMLP_W)))
    return pl.pallas_call(
        _proj_prompt_kernel, grid=(nblk,), in_specs=in_specs, out_specs=out_specs, out_shape=out_shape,
        scratch_shapes=[pltpu.VMEM((N_HEADS, nblk, LANES), F32),
                        pltpu.VMEM((GMLP_CHUNK, N_GROUPS * GMLP_CHUNK), BF16)],
        compiler_params=_params("arbitrary"), name="proj_prompt",
    )(x, w["g1"], w["wnat"], w["wt"], w["qg_col"], w["kg_pad"], w["kg_nat"], w["gvg"], w["bones"],
      w["wcat"], w["gbias"], w["gog"])


_MOBA_HEADS_PER_STEP = 2


def _moba_kernel(kaug_ref, qaug_ref, vt_ref, out_ref, s_even, s_odd):
    j = pl.program_id(1)
    tq = qaug_ref.shape[2]
    heads = range(_MOBA_HEADS_PER_STEP)

    def block_start(n):
        return pl.multiple_of(n * MOBA_BLOCK, MOBA_BLOCK)

    def produce(buf, n):
        for hh in heads:
            buf[hh] = _dot(kaug_ref[hh, pl.ds(block_start(n), MOBA_BLOCK), :], qaug_ref[hh])

    def fold(state, s, hh, n):
        m, l, acc = state
        m_new = jnp.maximum(m, jnp.max(s, axis=0, keepdims=True))
        alpha = jnp.exp(m - m_new)
        p = jnp.exp(s - m_new)
        l = l * alpha + jnp.sum(p, axis=0, keepdims=True)
        acc = acc * alpha + _dot(vt_ref[hh, :, pl.ds(block_start(n), MOBA_BLOCK)], p.astype(BF16))
        return m_new, l, acc

    def body(i, states):
        produce(s_odd, 2 * i + 1)
        states = tuple(fold(states[hh], s_even[hh], hh, 2 * i) for hh in heads)
        produce(s_even, 2 * i + 2)
        return tuple(fold(states[hh], s_odd[hh], hh, 2 * i + 1) for hh in heads)

    init = tuple((jnp.full((1, tq), -jnp.inf, F32), jnp.zeros((1, tq), F32), jnp.zeros((HEAD_DIM, tq), F32))
                 for _ in heads)
    produce(s_even, 0)
    states = lax.fori_loop(0, j // 2, body, init)

    causal = _iota((MOBA_BLOCK, tq), 0) <= _iota((MOBA_BLOCK, tq), 1)
    j_odd = (j & 1) == 1
    produce(s_odd, j)
    for hh in heads:
        first = jnp.where(j_odd, s_even[hh], jnp.where(causal, s_even[hh], NEG))
        state = fold(states[hh], first, hh, 2 * (j // 2))
        second = jnp.where(j_odd, jnp.where(causal, s_odd[hh], NEG), NEG)
        _, l, acc = fold(state, second, hh, j)
        out_ref[HEAD_DIM * hh:HEAD_DIM * (hh + 1), :] = acc / l


def _moba_prompt(kaug, qaug, vt):
    t = kaug.shape[1]
    tq = MOBA_BLOCK
    hs = _MOBA_HEADS_PER_STEP
    once = pl.Buffered(1)
    return pl.pallas_call(
        _moba_kernel, grid=(N_HEADS // hs, t // tq),
        in_specs=[pl.BlockSpec((hs, t, 2 * LANES), lambda g, j: (g, 0, 0), pipeline_mode=once),
                  pl.BlockSpec((hs, 2 * LANES, tq), lambda g, j: (g, 0, j)),
                  pl.BlockSpec((hs, HEAD_DIM, t), lambda g, j: (g, 0, 0), pipeline_mode=once)],
        out_specs=pl.BlockSpec((hs * HEAD_DIM, tq), lambda g, j: (g, j)),
        out_shape=jax.ShapeDtypeStruct((ATTN_W, t), F32),
        scratch_shapes=[pltpu.VMEM((hs, MOBA_BLOCK, tq), F32), pltpu.VMEM((hs, MOBA_BLOCK, tq), F32)],
        compiler_params=_params("parallel", "arbitrary"), name="moba_prompt",
    )(kaug, qaug, vt)


def _cand_index_table():
    rows = []
    for a, nb in _CAND_ROWS:
        for c0 in range(0, max(nb, 8), 8):
            rows.append([a * PEER_TOPK + c0 + b for b in range(8)])
    rows.append([(8 + a) * PEER_TOPK for a in range(8)])
    flat = np.asarray(rows, np.float32).reshape(-1, 1)
    return jnp.asarray(np.broadcast_to(flat, (flat.shape[0], LANES)).copy())


_N_CAND = 8 * (sum(max(nb, 8) // 8 for _, nb in _CAND_ROWS) + 1)


def _extract_top(cur, idx, n, on_hit):
    big = float(1 << 20)
    for r in range(n):
        m = jnp.max(cur, axis=0, keepdims=True)
        first = jnp.min(jnp.where(cur == m, idx, big), axis=0, keepdims=True)
        hit = idx == first
        cur = jnp.where(hit, -jnp.inf, cur)
        on_hit(r, m, hit)
    return cur


def _route_kernel(x_ref, attn_ref, gm_ref, aog_ref, woa_ref, wog_ref, g2_ref, wqt_ref, keys_ref, cidx_ref,
                  h_ref, xn2_ref, m1_ref, c1_ref, r2_ref, e2_ref,
                  s_s, rank_s, sv_s, na_s, *, attn_transposed):
    tt = x_ref.shape[0]
    at = attn_ref[...]
    if attn_transposed:
        at = at.T
    an = (_rms(at) * aog_ref[...]).astype(BF16)
    h = x_ref[...] + _dot(an, woa_ref[...]) + _dot(gm_ref[...], wog_ref[...])
    h_ref[...] = h
    xn2 = (_rms(h) * g2_ref[...]).astype(BF16)
    xn2_ref[...] = xn2
    qt = _dot_nt(wqt_ref[...], xn2)
    for hp in range(2 * PEER_HEADS):
        s_s[hp] = _dot3(keys_ref[hp], qt[PEER_HALF * hp:PEER_HALF * (hp + 1), :])

    key_idx = _iota((PEER_KEYS, LANES), 0).astype(F32)
    row8 = _iota((8, LANES), 0)

    def rank_distinct(hp, lanes):
        cur = s_s[hp, :, lanes]
        rank = jnp.full((PEER_KEYS, LANES), float(PEER_TOPK), F32)
        for r in range(PEER_TOPK):
            m = jnp.max(cur, axis=0, keepdims=True)
            hit = cur == m
            cur = jnp.where(hit, -jnp.inf, cur)
            rank = jnp.where(hit, float(r), rank)
            sv_s[hp, r:r + 1, lanes] = m
        rank_s[hp, :, lanes] = rank
        return jnp.sum(jnp.where(rank < float(PEER_TOPK), 1.0, 0.0), axis=0, keepdims=True)

    def rank_exact(hp, lanes):
        rank = [jnp.full((PEER_KEYS, LANES), float(PEER_TOPK), F32)]

        def on_hit(r, m, hit):
            rank[0] = jnp.where(hit, float(r), rank[0])
            sv_s[hp, r:r + 1, lanes] = m

        _extract_top(s_s[hp, :, lanes], key_idx, PEER_TOPK, on_hit)
        rank_s[hp, :, lanes] = rank[0]

    def per_head(hd, carry):
        ranked = []
        for sl in range(tt // LANES):
            lanes = slice(LANES * sl, LANES * (sl + 1))
            for hp in (2 * hd, 2 * hd + 1):
                ranked.append((hp, lanes, rank_distinct(hp, lanes)))
        for hp, lanes, n_ranked in ranked:
            @pl.when(jnp.max(jnp.abs(n_ranked - float(PEER_TOPK))) > 0.0)
            def _redo(hp=hp, lanes=lanes):
                rank_exact(hp, lanes)
        for sl in range(tt // LANES):
            lanes = slice(LANES * sl, LANES * (sl + 1))
            sv1 = sv_s[2 * hd, :, lanes]
            sv2 = sv_s[2 * hd + 1, :, lanes]
            pieces = []
            for a, nb in _CAND_ROWS:
                for c0 in range(0, max(nb, 8), 8):
                    piece = sv1[a:a + 1, :] + sv2[c0:c0 + 8, :]
                    pieces.append(piece if nb >= 8 else jnp.where(row8 < nb, piece, -jnp.inf))
            pieces.append(sv1[8:16, :] + sv2[0:1, :])
            cand = jnp.concatenate(pieces, axis=0)
            chosen = [jnp.zeros_like(cand)]

            def on_hit(r, m, hit):
                chosen[0] = jnp.where(hit, 1.0, chosen[0])

            _extract_top(cand, cidx_ref[...], PEER_TOPK, on_hit)
            chosen = chosen[0]
            top = sv1[0:1, :] + sv2[0:1, :]
            z = jnp.sum(jnp.where(chosen > 0.0, jnp.exp(cand - top), 0.0), axis=0, keepdims=True)
            row = 0
            for a, nb in _CAND_ROWS:
                nrows = max(nb, 8)
                na_s[sl, a:a + 1, :] = jnp.sum(chosen[row:row + nrows, :], axis=0, keepdims=True)
                row += nrows
            na_s[sl, 8:16, :] = chosen[row:row + 8, :]
            na = na_s[sl]
            rank1 = rank_s[2 * hd, :, lanes]
            m1 = jnp.zeros((PEER_KEYS, LANES), F32)
            for a in range(PEER_TOPK):
                m1 = jnp.where(rank1 == float(a), na[a:a + 1, :], m1)
            m1_ref[hd, :, lanes] = m1
            c1_ref[hd, :, lanes] = jnp.exp(s_s[2 * hd, :, lanes] - sv1[0:1, :]) / z
            r2_ref[hd, :, lanes] = rank_s[2 * hd + 1, :, lanes].astype(BF16)
            e2_ref[hd, :, lanes] = jnp.exp(s_s[2 * hd + 1, :, lanes] - sv2[0:1, :]).astype(BF16)
        return carry

    lax.fori_loop(0, PEER_HEADS, per_head, 0)


def _route(x, attn, gm, w, *, tt, attn_transposed):
    t = x.shape[0]
    row_tile = lambda width: pl.BlockSpec((tt, width), lambda i: (i, 0))
    attn_spec = pl.BlockSpec((ATTN_W, tt), lambda i: (0, i)) if attn_transposed else row_tile(ATTN_W)
    head_tile = pl.BlockSpec((PEER_HEADS, PEER_KEYS, tt), lambda i: (0, 0, i))
    in_specs = [row_tile(D_MODEL), attn_spec, row_tile(GMLP_W), _full((1, ATTN_W)),
                _full((ATTN_W, D_MODEL)), _full((GMLP_W, D_MODEL)), _full((1, D_MODEL)),
                _full((2 * PEER_HEADS * PEER_HALF, D_MODEL)), _full((2 * PEER_HEADS, PEER_KEYS, PEER_HALF)),
                _full((_N_CAND, LANES))]
    stat = lambda dt: jax.ShapeDtypeStruct((PEER_HEADS, PEER_KEYS, t), dt)
    out_shape = (jax.ShapeDtypeStruct((t, D_MODEL), F32), jax.ShapeDtypeStruct((t, D_MODEL), BF16),
                 stat(F32), stat(F32), stat(BF16), stat(BF16))
    out_specs = (row_tile(D_MODEL), row_tile(D_MODEL), head_tile, head_tile, head_tile, head_tile)
    return pl.pallas_call(
        functools.partial(_route_kernel, attn_transposed=attn_transposed),
        grid=(t // tt,), in_specs=in_specs, out_specs=out_specs, out_shape=out_shape,
        scratch_shapes=[pltpu.VMEM((2 * PEER_HEADS, PEER_KEYS, tt), F32),
                        pltpu.VMEM((2 * PEER_HEADS, PEER_KEYS, tt), F32),
                        pltpu.VMEM((2 * PEER_HEADS, PEER_TOPK, tt), F32),
                        pltpu.VMEM((tt // LANES, PEER_TOPK, LANES), F32)],
        compiler_params=_params("parallel"), name="route",
    )(x, attn, gm, w["aog"], w["wo_attn"], w["wo_gmlp"], w["g2"], w["wqt"], w["keys"], w["cand_idx"])


def _peer_kernel(xn2_ref, u_ref, vt_ref, m1_ref, c1_ref, r2_ref, e2_ref, h_ref, y_ref, acc_s, act_even_s, act_odd_s,
                 *, keys_per_step):
    c = pl.program_id(1)
    n_blocks = pl.num_programs(1) - 1

    @pl.when(c == 0)
    def _zero():
        acc_s[...] = jnp.zeros(acc_s.shape, F32)
        act_odd_s[...] = jnp.zeros(act_odd_s.shape, BF16)

    def step(act_write, act_read):
        act_write[...] = _gelu(_dot_nt(u_ref[...], xn2_ref[...])).astype(BF16)
        first_key = jnp.maximum(c - 1, 0) * keys_per_step
        parts = []
        for ii in range(keys_per_step):
            i = first_key + ii
            g = None
            for hd in range(PEER_HEADS):
                partners = m1_ref[hd, pl.ds(i, 1), :].astype(BF16)
                weight = c1_ref[hd, pl.ds(i, 1), :].astype(BF16)
                term = jnp.where(r2_ref[hd] < partners, e2_ref[hd], jnp.zeros((), BF16)) * weight
                g = term if g is None else g + term
            parts.append(g * act_read[PEER_KEYS * ii:PEER_KEYS * (ii + 1), :])
        acc_s[...] += _dot(vt_ref[...], jnp.concatenate(parts, axis=0))

    @pl.when((c & 1) == 0)
    def _even():
        step(act_even_s, act_odd_s)

    @pl.when((c & 1) == 1)
    def _odd():
        step(act_odd_s, act_even_s)

    @pl.when(c == n_blocks)
    def _finish():
        y_ref[...] = h_ref[...] + acc_s[...].T


def _peer(xn2, h, m1, c1, r2, e2, w, *, tt, keys_per_step=2):
    t = xn2.shape[0]
    ne = keys_per_step * PEER_KEYS
    n_blocks = N_EXPERTS // ne
    head_tile = pl.BlockSpec((PEER_HEADS, PEER_KEYS, tt), lambda j, c: (0, 0, j))
    row_tile = pl.BlockSpec((tt, D_MODEL), lambda j, c: (j, 0))
    return pl.pallas_call(
        functools.partial(_peer_kernel, keys_per_step=keys_per_step),
        grid=(t // tt, n_blocks + 1),
        in_specs=[row_tile, pl.BlockSpec((ne, D_MODEL), lambda j, c: (jnp.minimum(c, n_blocks - 1), 0)),
                  pl.BlockSpec((D_MODEL, ne), lambda j, c: (0, jnp.maximum(c - 1, 0))),
                  head_tile, head_tile, head_tile, head_tile, row_tile],
        out_specs=row_tile, out_shape=jax.ShapeDtypeStruct((t, D_MODEL), F32),
        scratch_shapes=[pltpu.VMEM((D_MODEL, tt), F32), pltpu.VMEM((ne, tt), BF16), pltpu.VMEM((ne, tt), BF16)],
        compiler_params=_params("parallel", "arbitrary"), name="peer",
    )(xn2, w["u"], w["vt"], m1, c1, r2, e2, h)


def _proj_sample_kernel(x_ref, g1_ref, win_ref, bones_ref, qgn_ref, kgn_ref, gvg_ref, wsc_ref, bsc_ref, gog_ref,
                        k_ref, v_ref, gv_ref, gm_ref, q_ref, qt_ref):
    xn = (_rms(x_ref[...]) * g1_ref[...]).astype(BF16)
    hn = _dot(xn, win_ref[...])

    def head_norm(z, g_ref):
        hi, lo = _split(z * z)
        ss = _dot(hi, bones_ref[...]) + _dot(lo, bones_ref[...])
        return z * lax.rsqrt(ss * (1.0 / HEAD_DIM) + EPS) * g_ref[...]

    qn = head_norm(hn[:, 0:ATTN_W], qgn_ref)
    kn = head_norm(hn[:, ATTN_W:2 * ATTN_W], kgn_ref)
    k_ref[...] = kn
    v_ref[...] = hn[:, 2 * ATTN_W:3 * ATTN_W]
    gu = _gelu(hn[:, 3 * ATTN_W:3 * ATTN_W + GMLP_W])
    gvn = _gmlp_norm(_gelu(hn[:, 3 * ATTN_W + GMLP_W:]), bones_ref, gvg_ref)
    gv_ref[...] = gvn
    gm_ref[...] = (_rms(gu * (wsc_ref[...] * gvn + bsc_ref[...])) * gog_ref[...]).astype(BF16)
    qs = qn * QK_SCALE
    q_ref[...] = qs
    qt_ref[...] = qs.T


def _proj_sample(x, w):
    nb = x.shape[0]
    out_shape = (jax.ShapeDtypeStruct((nb, ATTN_W), F32), jax.ShapeDtypeStruct((nb, ATTN_W), F32),
                 jax.ShapeDtypeStruct((nb, GMLP_W), F32), jax.ShapeDtypeStruct((nb, GMLP_W), BF16),
                 jax.ShapeDtypeStruct((nb, ATTN_W), F32), jax.ShapeDtypeStruct((ATTN_W, nb), F32))
    return pl.pallas_call(
        _proj_sample_kernel, out_shape=out_shape, compiler_params=_params(), name="proj_sample",
    )(x, w["g1"], w["win"], w["bones"], w["qg_nat"], w["kg_nat"], w["gvg"], w["ws_one"], w["b_one"], w["gog"])


_PAGES_PER_STEP = 16
_PAGES_PER_BLOCK = MOBA_BLOCK // PAGE
_PAGE_BLOCK = (None, N_HEADS, HEAD_DIM, PAGE)


def _sample_scores_kernel(pt_ref, qt_ref, q_ref, knew_ref, *refs, n_steps, past_len):
    k_refs = refs[:_PAGES_PER_STEP]
    p_ref, pself_ref, sel_ref, qcol_s, sc_s, gate_s = refs[_PAGES_PER_STEP:]
    d = pl.program_id(0)
    c = pl.program_id(1)
    n_seq = qt_ref.shape[1]
    lane = _iota((N_HEADS, LANES), 1)
    blocks_per_step = _PAGES_PER_STEP // _PAGES_PER_BLOCK

    @pl.when(c == 0)
    def _init():
        pick = jnp.where(_iota((n_seq, LANES), 0) == d, 1.0, 0.0)
        qcol_s[...] = _dot3(qt_ref[...], pick)
        gate_s[...] = jnp.zeros(gate_s.shape, F32)

    gates = gate_s[...]
    for b in range(blocks_per_step):
        blk = None
        for g in range(_PAGES_PER_BLOCK):
            r = b * _PAGES_PER_BLOCK + g
            rows = [jnp.sum(k_refs[r][hd] * qcol_s[HEAD_DIM * hd:HEAD_DIM * (hd + 1), :], axis=0, keepdims=True)
                    for hd in range(N_HEADS)]
            s_page = jnp.concatenate(rows, axis=0)
            sc_s[:, pl.ds(pl.multiple_of((c * _PAGES_PER_STEP + r) * PAGE, PAGE), PAGE)] = s_page
            blk = s_page if blk is None else blk + s_page
        gate = jnp.sum(blk, axis=-1, keepdims=True) * (1.0 / MOBA_BLOCK)
        gates = jnp.where(lane == c * blocks_per_step + b, gate, gates)
    gate_s[...] = gates

    @pl.when(c == n_steps - 1)
    def _select():
        n_blocks = n_steps * blocks_per_step
        lane_f = lane.astype(F32)
        cur = jnp.where(lane < n_blocks, gates, -jnp.inf)
        sel = jnp.zeros_like(cur)
        for _ in range(MOBA_TOPK):
            m = jnp.max(cur, axis=-1, keepdims=True)
            first = jnp.min(jnp.where(cur == m, lane_f, float(LANES)), axis=-1, keepdims=True)
            hit = lane_f == first
            sel = jnp.where(hit, 1.0, sel)
            cur = jnp.where(hit, -jnp.inf, cur)
        sel_ref[...] = sel

        n_pos = n_blocks * MOBA_BLOCK
        chosen = jnp.concatenate([jnp.broadcast_to(sel[:, n:n + 1], (N_HEADS, MOBA_BLOCK)) for n in range(n_blocks)],
                                 axis=1)
        slopes = jnp.exp2(-(_iota((N_HEADS, 1), 0) + 1).astype(F32))
        distance = float(past_len) - _iota((1, n_pos), 1).astype(F32)
        logit = jnp.where(chosen > 0.0, sc_s[...] - slopes * distance, NEG)
        self_logit = jnp.sum(q_ref[...] * knew_ref[...], axis=-1, keepdims=True)
        top = jnp.maximum(jnp.max(logit, axis=-1, keepdims=True), self_logit)
        p = jnp.exp(logit - top)
        p_self = jnp.exp(self_logit - top)
        inv = 1.0 / (jnp.sum(p, axis=-1, keepdims=True) + p_self)
        p_ref[...] = p * inv
        pself_ref[...] = jnp.broadcast_to(p_self * inv, pself_ref.shape)


def _sample_scores(page_table, cache_kt, qt, q, k_new, past_len):
    nb, n_pages = page_table.shape
    n_steps = n_pages // _PAGES_PER_STEP
    n_blocks = n_pages // _PAGES_PER_BLOCK
    assert n_blocks <= LANES
    page_spec = lambda r: pl.BlockSpec(
        _PAGE_BLOCK, lambda d, c, pt, r=r: (pt[d, c * _PAGES_PER_STEP + r], 0, 0, 0))
    per_seq = lambda shape: pl.BlockSpec((None,) + shape, lambda d, c, pt: (d,) + (0,) * len(shape))
    grid_spec = pltpu.PrefetchScalarGridSpec(
        num_scalar_prefetch=1, grid=(nb, n_steps),
        in_specs=[pl.BlockSpec((ATTN_W, nb), lambda d, c, pt: (0, 0)), per_seq((N_HEADS, HEAD_DIM)),
                  per_seq((N_HEADS, HEAD_DIM))] + [page_spec(r) for r in range(_PAGES_PER_STEP)],
        out_specs=(per_seq((N_HEADS, n_pages * PAGE)), per_seq((N_HEADS, HEAD_DIM)), per_seq((N_HEADS, LANES))),
        scratch_shapes=[pltpu.VMEM((ATTN_W, LANES), F32), pltpu.VMEM((N_HEADS, n_pages * PAGE), F32),
                        pltpu.VMEM((N_HEADS, LANES), F32)])
    out_shape = (jax.ShapeDtypeStruct((nb, N_HEADS, n_pages * PAGE), F32),
                 jax.ShapeDtypeStruct((nb, N_HEADS, HEAD_DIM), F32), jax.ShapeDtypeStruct((nb, N_HEADS, LANES), F32))
    return pl.pallas_call(
        functools.partial(_sample_scores_kernel, n_steps=n_steps, past_len=past_len),
        grid_spec=grid_spec, out_shape=out_shape,
        compiler_params=_params("parallel", "arbitrary"), name="sample_scores",
    )(page_table, qt, q, k_new, *([cache_kt] * _PAGES_PER_STEP))


_SLABS_PER_HEAD = MOBA_TOPK * _PAGES_PER_BLOCK


def _sample_values_kernel(phys_ref, logical_ref, p_ref, pself_ref, vnew_ref, *refs):
    n_slabs = N_HEADS * _SLABS_PER_HEAD
    v_refs = refs[:n_slabs]
    out_ref = refs[n_slabs]
    d = pl.program_id(0)
    cols = []
    for hd in range(N_HEADS):
        acc = None
        for k in range(_SLABS_PER_HEAD):
            i = hd * _SLABS_PER_HEAD + k
            start = pl.multiple_of(logical_ref[d, i] * PAGE, PAGE)
            term = v_refs[i][...] * p_ref[hd:hd + 1, pl.ds(start, PAGE)]
            acc = term if acc is None else acc + term
        cols.append(jnp.sum(acc, axis=-1, keepdims=True))
    ctx_t = jnp.concatenate(cols + [jnp.zeros((HEAD_DIM, LANES - N_HEADS), F32)], axis=1)
    out_ref[...] = ctx_t.T[0:N_HEADS, :] + pself_ref[...] * vnew_ref[...]


def _sample_values(slab_phys, slab_logical, cache_vt, p, pself, v_new):
    nb, n_slabs = slab_phys.shape
    assert n_slabs == N_HEADS * _SLABS_PER_HEAD
    slab_spec = lambda i: pl.BlockSpec(
        (None, None, HEAD_DIM, PAGE), lambda d, ph, lg, i=i: (ph[d, i], i // _SLABS_PER_HEAD, 0, 0))
    per_seq = lambda shape: pl.BlockSpec((None,) + shape, lambda d, ph, lg: (d,) + (0,) * len(shape))
    head_rows = per_seq((N_HEADS, HEAD_DIM))
    grid_spec = pltpu.PrefetchScalarGridSpec(
        num_scalar_prefetch=2, grid=(nb,),
        in_specs=[per_seq((N_HEADS, p.shape[2])), head_rows, head_rows] + [slab_spec(i) for i in range(n_slabs)],
        out_specs=head_rows)
    return pl.pallas_call(
        _sample_values_kernel, grid_spec=grid_spec, out_shape=jax.ShapeDtypeStruct((nb, N_HEADS, HEAD_DIM), F32),
        compiler_params=_params("parallel"), name="sample_values",
    )(slab_phys, slab_logical, p, pself, v_new, *([cache_vt] * n_slabs))


def _prepare(norm1_g, w_in, q_norm_g, k_norm_g, gmlp_v_norm_g, gmlp_ws, gmlp_b, attn_out_norm_g, gmlp_out_norm_g,
             w_out, norm2_g, peer_wq, peer_keys, peer_u, peer_v):
    wq, wk, wv, wgu, wgv = jnp.split(w_in, [ATTN_W, 2 * ATTN_W, 3 * ATTN_W, 3 * ATTN_W + GMLP_W], axis=1)
    wk_pad = jnp.pad(wk.reshape(D_MODEL, N_HEADS, HEAD_DIM), ((0, 0), (0, 0), (0, LANES - HEAD_DIM)))
    feat = np.arange(ATTN_W)
    w = {
        "g1": norm1_g.reshape(1, D_MODEL),
        "win": w_in.astype(BF16),
        "wnat": jnp.concatenate([wk_pad.reshape(D_MODEL, N_HEADS * LANES), wk, wv, wgu, wgv], axis=1).astype(BF16),
        "wt": jnp.concatenate([wq, wv], axis=1).T.astype(BF16),
        "qg_col": q_norm_g.reshape(HEAD_DIM, 1),
        "kg_pad": jnp.pad(k_norm_g, (0, LANES - HEAD_DIM)).reshape(1, LANES),
        "qg_nat": jnp.tile(q_norm_g, N_HEADS).reshape(1, ATTN_W),
        "kg_nat": jnp.tile(k_norm_g, N_HEADS).reshape(1, ATTN_W),
        "gvg": gmlp_v_norm_g.reshape(1, GMLP_W),
        "bones": jnp.asarray((feat[:, None] // GROUP_DIM) == (feat[None, :] // GROUP_DIM), BF16),
        "wcat": jnp.transpose(gmlp_ws, (1, 0, 2)).reshape(GMLP_CHUNK, N_GROUPS * GMLP_CHUNK),
        "gbias": jnp.repeat(gmlp_b.T, GROUP_DIM, axis=1),
        "ws_one": jnp.repeat(gmlp_ws[:, 0, 0], GROUP_DIM).reshape(1, GMLP_W),
        "b_one": jnp.repeat(gmlp_b[:, 0], GROUP_DIM).reshape(1, GMLP_W),
        "gog": gmlp_out_norm_g.reshape(1, GMLP_W),
        "aog": attn_out_norm_g.reshape(1, ATTN_W),
        "wo_attn": w_out[:ATTN_W].astype(BF16),
        "wo_gmlp": w_out[ATTN_W:].astype(BF16),
        "g2": norm2_g.reshape(1, D_MODEL),
        "wqt": peer_wq.T.astype(BF16),
        "keys": peer_keys.reshape(2 * PEER_HEADS, PEER_KEYS, PEER_HALF),
        "u": peer_u.astype(BF16),
        "vt": peer_v.T.astype(BF16),
        "cand_idx": _cand_index_table(),
    }
    return w


def _sample_attention(page_table, cache_k, cache_v, q, qt, k_new, v_new, past_len):
    nb = q.shape[0]
    heads = lambda a: a.reshape(nb, N_HEADS, HEAD_DIM)
    as_stored = lambda cache: jnp.transpose(cache, (0, 2, 3, 1))
    p, pself, sel = _sample_scores(page_table, as_stored(cache_k), qt, heads(q), heads(k_new), past_len)
    n_blocks = past_len // MOBA_BLOCK
    _, blocks = lax.top_k(sel[:, :, :n_blocks], MOBA_TOPK)
    logical = (blocks[..., None] * _PAGES_PER_BLOCK + jnp.arange(_PAGES_PER_BLOCK, dtype=jnp.int32)).reshape(nb, -1)
    phys = jnp.take_along_axis(page_table, logical, axis=1)
    ctx = _sample_values(phys, logical.astype(jnp.int32), as_stored(cache_v), p, pself, heads(v_new))
    return ctx.reshape(nb, ATTN_W)


def _layer_tail(x, attn, gm, w, *, route_tile, peer_tile, attn_transposed):
    h, xn2, m1, c1, r2, e2 = _route(x, attn, gm, w, tt=route_tile, attn_transposed=attn_transposed)
    return _peer(xn2, h, m1, c1, r2, e2, w, tt=peer_tile)


def kernel(x_prompt, x_sample, cache_k, cache_v, page_table, norm1_g, w_in, q_norm_g, k_norm_g, gmlp_v_norm_g,
           gmlp_ws, gmlp_b, attn_out_norm_g, gmlp_out_norm_g, w_out, norm2_g, peer_wq, peer_keys, peer_u, peer_v):
    w = _prepare(norm1_g, w_in, q_norm_g, k_norm_g, gmlp_v_norm_g, gmlp_ws, gmlp_b, attn_out_norm_g,
                 gmlp_out_norm_g, w_out, norm2_g, peer_wq, peer_keys, peer_u, peer_v)
    b, t, _ = x_prompt.shape
    assert b == 1
    xp = x_prompt.reshape(t, D_MODEL)
    k_p, v_p, kaug, qaug, vt, gm_p, gv_last = _proj_prompt(xp, w)
    attn_t = _moba_prompt(kaug, qaug, vt)
    y_p = _layer_tail(xp, attn_t, gm_p, w, route_tile=256, peer_tile=512, attn_transposed=True)

    nb, ds, _ = x_sample.shape
    assert ds == 1
    xs = x_sample.reshape(nb, D_MODEL)
    past_len = page_table.shape[1] * PAGE
    assert past_len % MOBA_BLOCK == 0
    k_s, v_s, gv_s, gm_s, q_s, qt_s = _proj_sample(xs, w)
    attn_s = _sample_attention(page_table, cache_k, cache_v, q_s, qt_s, k_s, v_s, past_len)
    y_s = _layer_tail(xs, attn_s, gm_s, w, route_tile=nb, peer_tile=nb, attn_transposed=False)

    return (y_p.reshape(1, t, D_MODEL), y_s.reshape(nb, 1, D_MODEL),
            k_p.reshape(1, t, N_HEADS, HEAD_DIM), v_p.reshape(1, t, N_HEADS, HEAD_DIM),
            gv_last.reshape(1, GMLP_CHUNK, GMLP_W),
            k_s.reshape(nb, 1, N_HEADS, HEAD_DIM), v_s.reshape(nb, 1, N_HEADS, HEAD_DIM),
            gv_s.reshape(nb, 1, GMLP_W))
```

```python
import functools
import math

import jax
import jax.numpy as jnp
import numpy as np
from jax import lax
from jax.experimental import pallas as pl
from jax.experimental.pallas import tpu as pltpu

F32 = jnp.float32
BF16 = jnp.bfloat16

D_MODEL = 1024
N_HEADS = 8
HEAD_DIM = 64
ATTN_W = N_HEADS * HEAD_DIM
GMLP_W = 512
N_GROUPS = 8
GROUP_DIM = GMLP_W // N_GROUPS
GMLP_CHUNK = 128
MOBA_BLOCK = 256
MOBA_TOPK = 3
PAGE = 128
PEER_HEADS = 8
PEER_KEYS = 128
PEER_HALF = 128
PEER_TOPK = 16
N_EXPERTS = PEER_KEYS * PEER_KEYS
EPS = 1e-6
NEG = -1e30
QK_SCALE = HEAD_DIM ** -0.5
GELU_C = math.sqrt(2.0 / math.pi)


def _bf16_pieces(x, n):
    out, rest = [], np.float32(x)
    for _ in range(n):
        piece = np.float32(rest.astype(jnp.bfloat16))
        out.append(float(piece))
        rest = np.float32(rest - piece)
    return tuple(out)


LOG2E = float(np.float32(math.log2(math.e)))
_LOG2E_PIECES = _bf16_pieces(LOG2E, 3)

LANES = 128
VMEM_LIMIT = 56 * 1024 * 1024

_CAND_ROWS = tuple((a, PEER_TOPK // (a + 1)) for a in range(8))


def _dot(a, b):
    return jnp.dot(a, b, preferred_element_type=F32)


def _dot_nt(a, b):
    return lax.dot_general(a, b, (((1,), (1,)), ((), ())), preferred_element_type=F32)


def _split(a):
    hi = a.astype(BF16)
    lo = (a - hi.astype(F32)).astype(BF16)
    return hi, lo


def _dot3(a, b):
    ah, al = _split(a)
    bh, bl = _split(b)
    return _dot(ah, bh) + (_dot(ah, bl) + _dot(al, bh))


def _rms(x, axis=-1):
    return x * lax.rsqrt(jnp.mean(x * x, axis=axis, keepdims=True) + EPS)


def _gelu(x):
    return x * (0.5 * (1.0 + jnp.tanh(GELU_C * (x + 0.044715 * (x * x * x)))))


def _gelu_sigmoid(x):
    z2 = x * ((-2.0 * GELU_C * 0.044715) * (x * x) - 2.0 * GELU_C)
    return x / (1.0 + jnp.exp(z2))


def _iota(shape, dim, dtype=jnp.int32):
    return lax.broadcasted_iota(dtype, shape, dim)


def _params(*sem):
    return pltpu.CompilerParams(dimension_semantics=sem, vmem_limit_bytes=VMEM_LIMIT)


def _full(shape):
    nd = len(shape)
    return pl.BlockSpec(shape, lambda *_: (0,) * nd)


_KP0, _KN0, _V0, _GU0, _GV0, _NAT_COLS = 0, 1024, 1536, 2048, 2560, 3072


def _gmlp_norm(gvr, bones_ref, gvg_ref):
    hi, lo = _split(gvr * gvr)
    ss = _dot(hi, bones_ref[...]) + _dot(lo, bones_ref[...])
    return gvr * lax.rsqrt(ss * (1.0 / GROUP_DIM) + EPS) * gvg_ref[...]


def _proj_prompt_kernel(x_ref, g1_ref, wnat_ref, wt_ref, qg_ref, kgp_ref, kgn_ref, gvg_ref, bones_ref,
                        wcat_ref, gbias_ref, gog_ref,
                        k_ref, v_ref, kaug_ref, qaug_ref, vt_ref, gm_ref, gvl_ref,
                        kmean_s, wtril_s):
    i = pl.program_id(0)
    tm = x_ref.shape[0]
    nblk = kmean_s.shape[1]

    @pl.when(i == 0)
    def _init():
        row = _iota((GMLP_CHUNK, N_GROUPS * GMLP_CHUNK), 0)
        col = _iota((GMLP_CHUNK, N_GROUPS * GMLP_CHUNK), 1) & (GMLP_CHUNK - 1)
        wtril_s[...] = jnp.where(col <= row, wcat_ref[...], 0.0).astype(BF16)
        kmean_s[...] = jnp.zeros(kmean_s.shape, F32)

    xn = (_rms(x_ref[...]) * g1_ref[...]).astype(BF16)
    hn = _dot(xn, wnat_ref[...])
    ht = _dot_nt(wt_ref[...], xn)

    lane = _iota((tm, LANES), 1)
    rowf = _iota((tm, LANES), 0).astype(F32)
    blk_row = _iota((nblk, tm), 0)
    blk_rowf = blk_row.astype(F32)
    piece_row = _iota((LANES, tm), 0)
    i_f = i.astype(F32)
    k_hi = jnp.where(lane < 3, rowf, jnp.where(lane < 6, float(MOBA_BLOCK) * i_f, 0.0))

    rs_heads = []
    for h in range(N_HEADS):
        kp = hn[:, _KP0 + LANES * h:_KP0 + LANES * (h + 1)]
        rs = lax.rsqrt(jnp.sum(kp * kp, axis=-1, keepdims=True) * (1.0 / HEAD_DIM) + EPS)
        rs_heads.append(rs)
        kn = kp * rs * kgp_ref[...]
        kmean_s[h, pl.ds(i, 1), :] = jnp.mean(kn, axis=0, keepdims=True)

        qt = ht[HEAD_DIM * h:HEAD_DIM * (h + 1), :]
        qn = qt * lax.rsqrt(jnp.sum(qt * qt, axis=0, keepdims=True) * (1.0 / HEAD_DIM) + EPS) * qg_ref[...]
        gate = _dot3(kmean_s[h], jnp.concatenate([qn, jnp.zeros_like(qn)], axis=0))
        cur = jnp.where(blk_row < i, gate, NEG)
        sel = jnp.zeros_like(cur)
        for _ in range(MOBA_TOPK):
            m = jnp.max(cur, axis=0, keepdims=True)
            first = jnp.min(jnp.where(cur == m, blk_rowf, float(nblk)), axis=0, keepdims=True)
            hit = blk_rowf == first
            sel = jnp.where(hit, 1.0, sel)
            cur = jnp.where(hit, -jnp.inf, cur)
        keep = jnp.where(blk_row < i, sel, jnp.where(blk_row == i, 1.0, 0.0))
        sel_bias = jnp.where(keep > 0.0, 0.0, NEG)
        pieces = [qn * (QK_SCALE * LOG2E), sel_bias]
        if nblk < HEAD_DIM:
            pieces.append(jnp.zeros((HEAD_DIM - nblk, tm), F32))
        slope = 2.0 ** -(h + 1)
        slope_rows = jnp.zeros((LANES, tm), F32)
        for r, piece in enumerate(_LOG2E_PIECES * 2):
            slope_rows = jnp.where(piece_row == r, slope * piece, slope_rows)
        qaug_ref[h] = jnp.concatenate(pieces + [slope_rows], axis=0).astype(BF16)

        k_lo = jnp.where(lane == HEAD_DIM + i, 1.0, kn)
        kaug_ref[h] = jnp.concatenate([k_lo, k_hi], axis=1).astype(BF16)
        vt_ref[h] = ht[ATTN_W + HEAD_DIM * h:ATTN_W + HEAD_DIM * (h + 1), :].astype(BF16)

    cols = []
    for c in range(ATTN_W // LANES):
        sc = jnp.where(lane < HEAD_DIM, rs_heads[2 * c], rs_heads[2 * c + 1])
        cols.append(hn[:, _KN0 + LANES * c:_KN0 + LANES * (c + 1)] * sc)
    k_ref[...] = jnp.concatenate(cols, axis=1) * kgn_ref[...]
    v_ref[...] = hn[:, _V0:_V0 + ATTN_W]

    gu = _gelu(hn[:, _GU0:_GU0 + GMLP_W])
    gvn = _gmlp_norm(_gelu(hn[:, _GV0:_GV0 + GMLP_W]), bones_ref, gvg_ref)
    gvl_ref[...] = gvn[tm - GMLP_CHUNK:, :]
    group_of_lane = _iota((1, GMLP_W), 1) >> 6
    outs = []
    for c in range(tm // GMLP_CHUNK):
        gc = gvn[GMLP_CHUNK * c:GMLP_CHUNK * (c + 1), :]
        stacked = jnp.concatenate([jnp.where(group_of_lane == g, gc, 0.0) for g in range(N_GROUPS)], axis=0)
        mixed = _dot(wtril_s[...], stacked.astype(BF16)) + gbias_ref[...]
        outs.append(gu[GMLP_CHUNK * c:GMLP_CHUNK * (c + 1), :] * mixed)
    gm_ref[...] = (_rms(jnp.concatenate(outs, axis=0)) * gog_ref[...]).astype(BF16)


def _proj_prompt(x, w, tm=MOBA_BLOCK):
    t = x.shape[0]
    nblk = t // tm
    assert nblk <= HEAD_DIM
    row_tile = lambda width: pl.BlockSpec((tm, width), lambda i: (i, 0))
    in_specs = [row_tile(D_MODEL), _full((1, D_MODEL)), _full((D_MODEL, _NAT_COLS)), _full((D_MODEL, D_MODEL)),
                _full((HEAD_DIM, 1)), _full((1, LANES)), _full((1, ATTN_W)), _full((1, GMLP_W)),
                _full((GMLP_W, GMLP_W)), _full((GMLP_CHUNK, N_GROUPS * GMLP_CHUNK)), _full((GMLP_CHUNK, GMLP_W)),
                _full((1, GMLP_W))]
    out_shape = (jax.ShapeDtypeStruct((t, ATTN_W), F32), jax.ShapeDtypeStruct((t, ATTN_W), F32),
                 jax.ShapeDtypeStruct((N_HEADS, t, 2 * LANES), BF16),
                 jax.ShapeDtypeStruct((N_HEADS, 2 * LANES, t), BF16),
                 jax.ShapeDtypeStruct((N_HEADS, HEAD_DIM, t), BF16),
                 jax.ShapeDtypeStruct((t, GMLP_W), BF16), jax.ShapeDtypeStruct((GMLP_CHUNK, GMLP_W), F32))
    out_specs = (row_tile(ATTN_W), row_tile(ATTN_W),
                 pl.BlockSpec((N_HEADS, tm, 2 * LANES), lambda i: (0, i, 0)),
                 pl.BlockSpec((N_HEADS, 2 * LANES, tm), lambda i: (0, 0, i)),
                 pl.BlockSpec((N_HEADS, HEAD_DIM, tm), lambda i: (0, 0, i)),
                 row_tile(GMLP_W), _full((GMLP_CHUNK, GMLP_W)))
    return pl.pallas_call(
        _proj_prompt_kernel, grid=(nblk,), in_specs=in_specs, out_specs=out_specs, out_shape=out_shape,
        scratch_shapes=[pltpu.VMEM((N_HEADS, nblk, LANES), F32),
                        pltpu.VMEM((GMLP_CHUNK, N_GROUPS * GMLP_CHUNK), BF16)],
        compiler_params=_params("arbitrary"), name="proj_prompt",
    )(x, w["g1"], w["wnat"], w["wt"], w["qg_col"], w["kg_pad"], w["kg_nat"], w["gvg"], w["bones"],
      w["wcat"], w["gbias"], w["gog"])


_MOBA_HEADS_PER_STEP = 4


def _moba_kernel(kaug_ref, qaug_ref, vt_ref, out_ref, s_even, s_odd):
    j = pl.program_id(1)
    tq = qaug_ref.shape[2]
    heads = range(_MOBA_HEADS_PER_STEP)

    def block_start(n):
        return pl.multiple_of(n * MOBA_BLOCK, MOBA_BLOCK)

    def produce(buf, n):
        for hh in heads:
            buf[hh] = _dot(kaug_ref[hh, pl.ds(block_start(n), MOBA_BLOCK), :], qaug_ref[hh])

    def fold(state, s, hh, n):
        m, l, acc = state
        m_new = jnp.maximum(m, jnp.max(s, axis=0, keepdims=True))
        alpha = jnp.exp2(m - m_new)
        p = jnp.exp2(s - m_new)
        l = l * alpha + jnp.sum(p, axis=0, keepdims=True)
        acc = acc * alpha + _dot(vt_ref[hh, :, pl.ds(block_start(n), MOBA_BLOCK)], p.astype(BF16))
        return m_new, l, acc

    def body(i, states):
        produce(s_odd, 2 * i + 1)
        states = tuple(fold(states[hh], s_even[hh], hh, 2 * i) for hh in heads)
        produce(s_even, 2 * i + 2)
        return tuple(fold(states[hh], s_odd[hh], hh, 2 * i + 1) for hh in heads)

    init = tuple((jnp.full((1, tq), -jnp.inf, F32), jnp.zeros((1, tq), F32), jnp.zeros((HEAD_DIM, tq), F32))
                 for _ in heads)
    produce(s_even, 0)
    states = lax.fori_loop(0, j // 2, body, init)

    causal = _iota((MOBA_BLOCK, tq), 0) <= _iota((MOBA_BLOCK, tq), 1)
    j_odd = (j & 1) == 1
    produce(s_odd, j)
    for hh in heads:
        first = jnp.where(j_odd, s_even[hh], jnp.where(causal, s_even[hh], NEG))
        state = fold(states[hh], first, hh, 2 * (j // 2))
        second = jnp.where(j_odd, jnp.where(causal, s_odd[hh], NEG), NEG)
        _, l, acc = fold(state, second, hh, j)
        out_ref[HEAD_DIM * hh:HEAD_DIM * (hh + 1), :] = acc / l


def _moba_prompt(kaug, qaug, vt):
    t = kaug.shape[1]
    tq = MOBA_BLOCK
    hs = _MOBA_HEADS_PER_STEP
    once = pl.Buffered(1)
    return pl.pallas_call(
        _moba_kernel, grid=(N_HEADS // hs, t // tq),
        in_specs=[pl.BlockSpec((hs, t, 2 * LANES), lambda g, j: (g, 0, 0), pipeline_mode=once),
                  pl.BlockSpec((hs, 2 * LANES, tq), lambda g, j: (g, 0, j)),
                  pl.BlockSpec((hs, HEAD_DIM, t), lambda g, j: (g, 0, 0), pipeline_mode=once)],
        out_specs=pl.BlockSpec((hs * HEAD_DIM, tq), lambda g, j: (g, j)),
        out_shape=jax.ShapeDtypeStruct((ATTN_W, t), F32),
        scratch_shapes=[pltpu.VMEM((hs, MOBA_BLOCK, tq), F32), pltpu.VMEM((hs, MOBA_BLOCK, tq), F32)],
        compiler_params=_params("parallel", "arbitrary"), name="moba_prompt",
    )(kaug, qaug, vt)


def _cand_index_table():
    rows = []
    for a, nb in _CAND_ROWS:
        for c0 in range(0, max(nb, 8), 8):
            rows.append([a * PEER_TOPK + c0 + b for b in range(8)])
    rows.append([(8 + a) * PEER_TOPK for a in range(8)])
    flat = np.asarray(rows, np.float32).reshape(-1, 1)
    return jnp.asarray(np.broadcast_to(flat, (flat.shape[0], LANES)).copy())


_N_CAND = 8 * (sum(max(nb, 8) // 8 for _, nb in _CAND_ROWS) + 1)


def _extract_top(cur, idx, n, on_hit):
    big = float(1 << 20)
    for r in range(n):
        m = jnp.max(cur, axis=0, keepdims=True)
        first = jnp.min(jnp.where(cur == m, idx, big), axis=0, keepdims=True)
        hit = idx == first
        cur = jnp.where(hit, -jnp.inf, cur)
        on_hit(r, m, hit)
    return cur


def _route_kernel(x_ref, attn_ref, gm_ref, aog_ref, woa_ref, wog_ref, g2_ref, wqt_ref, keys_ref, cidx_ref,
                  h_ref, xn2_ref, m1_ref, c1_ref, r2_ref, e2_ref,
                  s_s, rank_s, sv_s, na_s, *, attn_transposed):
    tt = x_ref.shape[0]
    at = attn_ref[...]
    if attn_transposed:
        at = at.T
    an = (_rms(at) * aog_ref[...]).astype(BF16)
    h = x_ref[...] + _dot(an, woa_ref[...]) + _dot(gm_ref[...], wog_ref[...])
    h_ref[...] = h
    xn2 = (_rms(h) * g2_ref[...]).astype(BF16)
    xn2_ref[...] = xn2
    qt = _dot_nt(wqt_ref[...], xn2)
    for hp in range(2 * PEER_HEADS):
        s_s[hp] = _dot3(keys_ref[hp], qt[PEER_HALF * hp:PEER_HALF * (hp + 1), :])

    key_idx = _iota((PEER_KEYS, LANES), 0).astype(F32)
    row8 = _iota((8, LANES), 0)

    def rank_distinct(hp, lanes):
        cur = s_s[hp, :, lanes]
        rank = jnp.full((PEER_KEYS, LANES), float(PEER_TOPK), F32)
        for r in range(PEER_TOPK):
            m = jnp.max(cur, axis=0, keepdims=True)
            hit = cur == m
            cur = jnp.where(hit, -jnp.inf, cur)
            rank = jnp.where(hit, float(r), rank)
            sv_s[hp, r:r + 1, lanes] = m
        rank_s[hp, :, lanes] = rank
        return jnp.sum(jnp.where(rank < float(PEER_TOPK), 1.0, 0.0), axis=0, keepdims=True)

    def rank_exact(hp, lanes):
        rank = [jnp.full((PEER_KEYS, LANES), float(PEER_TOPK), F32)]

        def on_hit(r, m, hit):
            rank[0] = jnp.where(hit, float(r), rank[0])
            sv_s[hp, r:r + 1, lanes] = m

        _extract_top(s_s[hp, :, lanes], key_idx, PEER_TOPK, on_hit)
        rank_s[hp, :, lanes] = rank[0]

    def per_head(hd, carry):
        ranked = []
        for sl in range(tt // LANES):
            lanes = slice(LANES * sl, LANES * (sl + 1))
            for hp in (2 * hd, 2 * hd + 1):
                ranked.append((hp, lanes, rank_distinct(hp, lanes)))
        for hp, lanes, n_ranked in ranked:
            @pl.when(jnp.max(jnp.abs(n_ranked - float(PEER_TOPK))) > 0.0)
            def _redo(hp=hp, lanes=lanes):
                rank_exact(hp, lanes)
        for sl in range(tt // LANES):
            lanes = slice(LANES * sl, LANES * (sl + 1))
            sv1 = sv_s[2 * hd, :, lanes]
            sv2 = sv_s[2 * hd + 1, :, lanes]
            pieces = []
            for a, nb in _CAND_ROWS:
                for c0 in range(0, max(nb, 8), 8):
                    piece = sv1[a:a + 1, :] + sv2[c0:c0 + 8, :]
                    pieces.append(piece if nb >= 8 else jnp.where(row8 < nb, piece, -jnp.inf))
            pieces.append(sv1[8:16, :] + sv2[0:1, :])
            cand = jnp.concatenate(pieces, axis=0)
            chosen = [jnp.zeros_like(cand)]

            def on_hit(r, m, hit):
                chosen[0] = jnp.where(hit, 1.0, chosen[0])

            _extract_top(cand, cidx_ref[...], PEER_TOPK, on_hit)
            chosen = chosen[0]
            top = sv1[0:1, :] + sv2[0:1, :]
            z = jnp.sum(jnp.where(chosen > 0.0, jnp.exp(cand - top), 0.0), axis=0, keepdims=True)
            row = 0
            for a, nb in _CAND_ROWS:
                nrows = max(nb, 8)
                na_s[sl, a:a + 1, :] = jnp.sum(chosen[row:row + nrows, :], axis=0, keepdims=True)
                row += nrows
            na_s[sl, 8:16, :] = chosen[row:row + 8, :]
            na = na_s[sl]
            rank1 = rank_s[2 * hd, :, lanes]
            m1 = jnp.zeros((PEER_KEYS, LANES), F32)
            for a in range(PEER_TOPK):
                m1 = jnp.where(rank1 == float(a), na[a:a + 1, :], m1)
            m1_ref[hd, :, lanes] = m1
            c1_ref[hd, :, lanes] = jnp.exp(s_s[2 * hd, :, lanes] - sv1[0:1, :]) / z
            r2_ref[hd, :, lanes] = rank_s[2 * hd + 1, :, lanes].astype(BF16)
            e2_ref[hd, :, lanes] = jnp.exp(s_s[2 * hd + 1, :, lanes] - sv2[0:1, :]).astype(BF16)
        return carry

    lax.fori_loop(0, PEER_HEADS, per_head, 0)


def _route(x, attn, gm, w, *, tt, attn_transposed):
    t = x.shape[0]
    row_tile = lambda width: pl.BlockSpec((tt, width), lambda i: (i, 0))
    attn_spec = pl.BlockSpec((ATTN_W, tt), lambda i: (0, i)) if attn_transposed else row_tile(ATTN_W)
    head_tile = pl.BlockSpec((PEER_HEADS, PEER_KEYS, tt), lambda i: (0, 0, i))
    in_specs = [row_tile(D_MODEL), attn_spec, row_tile(GMLP_W), _full((1, ATTN_W)),
                _full((ATTN_W, D_MODEL)), _full((GMLP_W, D_MODEL)), _full((1, D_MODEL)),
                _full((2 * PEER_HEADS * PEER_HALF, D_MODEL)), _full((2 * PEER_HEADS, PEER_KEYS, PEER_HALF)),
                _full((_N_CAND, LANES))]
    stat = lambda dt: jax.ShapeDtypeStruct((PEER_HEADS, PEER_KEYS, t), dt)
    out_shape = (jax.ShapeDtypeStruct((t, D_MODEL), F32), jax.ShapeDtypeStruct((t, D_MODEL), BF16),
                 stat(F32), stat(F32), stat(BF16), stat(BF16))
    out_specs = (row_tile(D_MODEL), row_tile(D_MODEL), head_tile, head_tile, head_tile, head_tile)
    return pl.pallas_call(
        functools.partial(_route_kernel, attn_transposed=attn_transposed),
        grid=(t // tt,), in_specs=in_specs, out_specs=out_specs, out_shape=out_shape,
        scratch_shapes=[pltpu.VMEM((2 * PEER_HEADS, PEER_KEYS, tt), F32),
                        pltpu.VMEM((2 * PEER_HEADS, PEER_KEYS, tt), F32),
                        pltpu.VMEM((2 * PEER_HEADS, PEER_TOPK, tt), F32),
                        pltpu.VMEM((tt // LANES, PEER_TOPK, LANES), F32)],
        compiler_params=_params("parallel"), name="route",
    )(x, attn, gm, w["aog"], w["wo_attn"], w["wo_gmlp"], w["g2"], w["wqt"], w["keys"], w["cand_idx"])


def _peer_kernel(xn2_ref, u_ref, vt_ref, m1_ref, c1_ref, r2_ref, e2_ref, h_ref, y_ref, acc_s, act_even_s, act_odd_s,
                 gated_s, *, keys_per_step):
    c = pl.program_id(1)
    n_blocks = pl.num_programs(1) - 1

    @pl.when(c == 0)
    def _zero():
        acc_s[...] = jnp.zeros(acc_s.shape, F32)
        act_odd_s[...] = jnp.zeros(act_odd_s.shape, BF16)

    def step(act_write, act_read):
        act_write[...] = _gelu_sigmoid(_dot_nt(u_ref[...], xn2_ref[...]).astype(BF16))
        first_key = jnp.maximum(c - 1, 0) * keys_per_step
        for ii in range(keys_per_step):
            i = first_key + ii
            rows = slice(PEER_KEYS * ii, PEER_KEYS * (ii + 1))
            g = None
            for hd in range(PEER_HEADS):
                partners = m1_ref[hd, pl.ds(i, 1), :].astype(BF16)
                weight = c1_ref[hd, pl.ds(i, 1), :].astype(BF16)
                term = jnp.where(r2_ref[hd] < partners, e2_ref[hd], jnp.zeros((), BF16)) * weight
                g = term if g is None else g + term
            gated_s[rows, :] = g * act_read[rows, :]
        acc_s[...] += _dot(vt_ref[...], gated_s[...])

    @pl.when((c & 1) == 0)
    def _even():
        step(act_even_s, act_odd_s)

    @pl.when((c & 1) == 1)
    def _odd():
        step(act_odd_s, act_even_s)

    @pl.when(c == n_blocks)
    def _finish():
        y_ref[...] = h_ref[...] + acc_s[...].T


def _peer(xn2, h, m1, c1, r2, e2, w, *, tt, keys_per_step=4):
    t = xn2.shape[0]
    ne = keys_per_step * PEER_KEYS
    n_blocks = N_EXPERTS // ne
    head_tile = pl.BlockSpec((PEER_HEADS, PEER_KEYS, tt), lambda j, c: (0, 0, j))
    row_tile = pl.BlockSpec((tt, D_MODEL), lambda j, c: (j, 0))
    return pl.pallas_call(
        functools.partial(_peer_kernel, keys_per_step=keys_per_step),
        grid=(t // tt, n_blocks + 1),
        in_specs=[row_tile, pl.BlockSpec((ne, D_MODEL), lambda j, c: (jnp.minimum(c, n_blocks - 1), 0)),
                  pl.BlockSpec((D_MODEL, ne), lambda j, c: (0, jnp.maximum(c - 1, 0))),
                  head_tile, head_tile, head_tile, head_tile, row_tile],
        out_specs=row_tile, out_shape=jax.ShapeDtypeStruct((t, D_MODEL), F32),
        scratch_shapes=[pltpu.VMEM((D_MODEL, tt), F32)] + [pltpu.VMEM((ne, tt), BF16)] * 3,
        compiler_params=_params("parallel", "arbitrary"), name="peer",
    )(xn2, w["u"], w["vt"], m1, c1, r2, e2, h)


def _proj_sample_kernel(x_ref, g1_ref, win_ref, bones_ref, qgn_ref, kgn_ref, gvg_ref, wsc_ref, bsc_ref, gog_ref,
                        k_ref, v_ref, gv_ref, gm_ref, q_ref, qt_ref):
    xn = (_rms(x_ref[...]) * g1_ref[...]).astype(BF16)
    hn = _dot(xn, win_ref[...])

    def head_norm(z, g_ref):
        hi, lo = _split(z * z)
        ss = _dot(hi, bones_ref[...]) + _dot(lo, bones_ref[...])
        return z * lax.rsqrt(ss * (1.0 / HEAD_DIM) + EPS) * g_ref[...]

    qn = head_norm(hn[:, 0:ATTN_W], qgn_ref)
    kn = head_norm(hn[:, ATTN_W:2 * ATTN_W], kgn_ref)
    k_ref[...] = kn
    v_ref[...] = hn[:, 2 * ATTN_W:3 * ATTN_W]
    gu = _gelu(hn[:, 3 * ATTN_W:3 * ATTN_W + GMLP_W])
    gvn = _gmlp_norm(_gelu(hn[:, 3 * ATTN_W + GMLP_W:]), bones_ref, gvg_ref)
    gv_ref[...] = gvn
    gm_ref[...] = (_rms(gu * (wsc_ref[...] * gvn + bsc_ref[...])) * gog_ref[...]).astype(BF16)
    qs = qn * QK_SCALE
    q_ref[...] = qs
    qt_ref[...] = qs.T


def _proj_sample(x, w):
    nb = x.shape[0]
    out_shape = (jax.ShapeDtypeStruct((nb, ATTN_W), F32), jax.ShapeDtypeStruct((nb, ATTN_W), F32),
                 jax.ShapeDtypeStruct((nb, GMLP_W), F32), jax.ShapeDtypeStruct((nb, GMLP_W), BF16),
                 jax.ShapeDtypeStruct((nb, ATTN_W), F32), jax.ShapeDtypeStruct((ATTN_W, nb), F32))
    return pl.pallas_call(
        _proj_sample_kernel, out_shape=out_shape, compiler_params=_params(), name="proj_sample",
    )(x, w["g1"], w["win"], w["bones"], w["qg_nat"], w["kg_nat"], w["gvg"], w["ws_one"], w["b_one"], w["gog"])


_PAGES_PER_STEP = 16
_PAGES_PER_BLOCK = MOBA_BLOCK // PAGE
_PAGE_BLOCK = (None, N_HEADS, HEAD_DIM, PAGE)


def _sample_scores_kernel(pt_ref, qt_ref, q_ref, knew_ref, *refs, n_steps, past_len):
    k_refs = refs[:_PAGES_PER_STEP]
    p_ref, pself_ref, sel_ref, qcol_s, sc_s, gate_s = refs[_PAGES_PER_STEP:]
    d = pl.program_id(0)
    c = pl.program_id(1)
    n_seq = qt_ref.shape[1]
    lane = _iota((N_HEADS, LANES), 1)
    blocks_per_step = _PAGES_PER_STEP // _PAGES_PER_BLOCK

    @pl.when(c == 0)
    def _init():
        pick = jnp.where(_iota((n_seq, LANES), 0) == d, 1.0, 0.0)
        qcol_s[...] = _dot3(qt_ref[...], pick)
        gate_s[...] = jnp.zeros(gate_s.shape, F32)

    gates = gate_s[...]
    for b in range(blocks_per_step):
        blk = None
        for g in range(_PAGES_PER_BLOCK):
            r = b * _PAGES_PER_BLOCK + g
            rows = [jnp.sum(k_refs[r][hd] * qcol_s[HEAD_DIM * hd:HEAD_DIM * (hd + 1), :], axis=0, keepdims=True)
                    for hd in range(N_HEADS)]
            s_page = jnp.concatenate(rows, axis=0)
            sc_s[:, pl.ds(pl.multiple_of((c * _PAGES_PER_STEP + r) * PAGE, PAGE), PAGE)] = s_page
            blk = s_page if blk is None else blk + s_page
        gate = jnp.sum(blk, axis=-1, keepdims=True) * (1.0 / MOBA_BLOCK)
        gates = jnp.where(lane == c * blocks_per_step + b, gate, gates)
    gate_s[...] = gates

    @pl.when(c == n_steps - 1)
    def _select():
        n_blocks = n_steps * blocks_per_step
        lane_f = lane.astype(F32)
        cur = jnp.where(lane < n_blocks, gates, -jnp.inf)
        sel = jnp.zeros_like(cur)
        picked = jnp.zeros_like(cur)
        for r in range(MOBA_TOPK):
            m = jnp.max(cur, axis=-1, keepdims=True)
            first = jnp.min(jnp.where(cur == m, lane_f, float(LANES)), axis=-1, keepdims=True)
            hit = lane_f == first
            sel = jnp.where(hit, 1.0, sel)
            picked = jnp.where(lane == r, first, picked)
            cur = jnp.where(hit, -jnp.inf, cur)
        sel_ref[...] = picked

        n_pos = n_blocks * MOBA_BLOCK
        chosen = jnp.concatenate([jnp.broadcast_to(sel[:, n:n + 1], (N_HEADS, MOBA_BLOCK)) for n in range(n_blocks)],
                                 axis=1)
        slopes = jnp.exp2(-(_iota((N_HEADS, 1), 0) + 1).astype(F32))
        distance = float(past_len) - _iota((1, n_pos), 1).astype(F32)
        logit = jnp.where(chosen > 0.0, sc_s[...] - slopes * distance, NEG)
        self_logit = jnp.sum(q_ref[...] * knew_ref[...], axis=-1, keepdims=True)
        top = jnp.maximum(jnp.max(logit, axis=-1, keepdims=True), self_logit)
        p = jnp.exp(logit - top)
        p_self = jnp.exp(self_logit - top)
        inv = 1.0 / (jnp.sum(p, axis=-1, keepdims=True) + p_self)
        p_ref[...] = p * inv
        pself_ref[...] = jnp.broadcast_to(p_self * inv, pself_ref.shape)


def _sample_scores(page_table, cache_kt, qt, q, k_new, past_len):
    nb, n_pages = page_table.shape
    n_steps = n_pages // _PAGES_PER_STEP
    n_blocks = n_pages // _PAGES_PER_BLOCK
    assert n_blocks <= LANES
    page_spec = lambda r: pl.BlockSpec(
        _PAGE_BLOCK, lambda d, c, pt, r=r: (pt[d, c * _PAGES_PER_STEP + r], 0, 0, 0))
    per_seq = lambda shape: pl.BlockSpec((None,) + shape, lambda d, c, pt: (d,) + (0,) * len(shape))
    grid_spec = pltpu.PrefetchScalarGridSpec(
        num_scalar_prefetch=1, grid=(nb, n_steps),
        in_specs=[pl.BlockSpec((ATTN_W, nb), lambda d, c, pt: (0, 0)), per_seq((N_HEADS, HEAD_DIM)),
                  per_seq((N_HEADS, HEAD_DIM))] + [page_spec(r) for r in range(_PAGES_PER_STEP)],
        out_specs=(per_seq((N_HEADS, n_pages * PAGE)), per_seq((N_HEADS, HEAD_DIM)), per_seq((N_HEADS, LANES))),
        scratch_shapes=[pltpu.VMEM((ATTN_W, LANES), F32), pltpu.VMEM((N_HEADS, n_pages * PAGE), F32),
                        pltpu.VMEM((N_HEADS, LANES), F32)])
    out_shape = (jax.ShapeDtypeStruct((nb, N_HEADS, n_pages * PAGE), F32),
                 jax.ShapeDtypeStruct((nb, N_HEADS, HEAD_DIM), F32), jax.ShapeDtypeStruct((nb, N_HEADS, LANES), F32))
    return pl.pallas_call(
        functools.partial(_sample_scores_kernel, n_steps=n_steps, past_len=past_len),
        grid_spec=grid_spec, out_shape=out_shape,
        compiler_params=_params("parallel", "arbitrary"), name="sample_scores",
    )(page_table, qt, q, k_new, *([cache_kt] * _PAGES_PER_STEP))


_SLABS_PER_HEAD = MOBA_TOPK * _PAGES_PER_BLOCK


def _sample_values_kernel(phys_ref, logical_ref, p_ref, pself_ref, vnew_ref, *refs):
    n_slabs = N_HEADS * _SLABS_PER_HEAD
    v_refs = refs[:n_slabs]
    out_ref = refs[n_slabs]
    d = pl.program_id(0)
    cols = []
    for hd in range(N_HEADS):
        acc = None
        for k in range(_SLABS_PER_HEAD):
            i = hd * _SLABS_PER_HEAD + k
            start = pl.multiple_of(logical_ref[d, i] * PAGE, PAGE)
            term = v_refs[i][...] * p_ref[hd:hd + 1, pl.ds(start, PAGE)]
            acc = term if acc is None else acc + term
        cols.append(jnp.sum(acc, axis=-1, keepdims=True))
    ctx_t = jnp.concatenate(cols + [jnp.zeros((HEAD_DIM, LANES - N_HEADS), F32)], axis=1)
    out_ref[...] = ctx_t.T[0:N_HEADS, :] + pself_ref[...] * vnew_ref[...]


def _sample_values(slab_phys, slab_logical, cache_vt, p, pself, v_new):
    nb, n_slabs = slab_phys.shape
    assert n_slabs == N_HEADS * _SLABS_PER_HEAD
    slab_spec = lambda i: pl.BlockSpec(
        (None, None, HEAD_DIM, PAGE), lambda d, ph, lg, i=i: (ph[d, i], i // _SLABS_PER_HEAD, 0, 0))
    per_seq = lambda shape: pl.BlockSpec((None,) + shape, lambda d, ph, lg: (d,) + (0,) * len(shape))
    head_rows = per_seq((N_HEADS, HEAD_DIM))
    grid_spec = pltpu.PrefetchScalarGridSpec(
        num_scalar_prefetch=2, grid=(nb,),
        in_specs=[per_seq((N_HEADS, p.shape[2])), head_rows, head_rows] + [slab_spec(i) for i in range(n_slabs)],
        out_specs=head_rows)
    return pl.pallas_call(
        _sample_values_kernel, grid_spec=grid_spec, out_shape=jax.ShapeDtypeStruct((nb, N_HEADS, HEAD_DIM), F32),
        compiler_params=_params("parallel"), name="sample_values",
    )(slab_phys, slab_logical, p, pself, v_new, *([cache_vt] * n_slabs))


def _prepare(norm1_g, w_in, q_norm_g, k_norm_g, gmlp_v_norm_g, gmlp_ws, gmlp_b, attn_out_norm_g, gmlp_out_norm_g,
             w_out, norm2_g, peer_wq, peer_keys, peer_u, peer_v):
    wq, wk, wv, wgu, wgv = jnp.split(w_in, [ATTN_W, 2 * ATTN_W, 3 * ATTN_W, 3 * ATTN_W + GMLP_W], axis=1)
    wk_pad = jnp.pad(wk.reshape(D_MODEL, N_HEADS, HEAD_DIM), ((0, 0), (0, 0), (0, LANES - HEAD_DIM)))
    feat = np.arange(ATTN_W)
    w = {
        "g1": norm1_g.reshape(1, D_MODEL),
        "win": w_in.astype(BF16),
        "wnat": jnp.concatenate([wk_pad.reshape(D_MODEL, N_HEADS * LANES), wk, wv, wgu, wgv], axis=1).astype(BF16),
        "wt": jnp.concatenate([wq, wv], axis=1).T.astype(BF16),
        "qg_col": q_norm_g.reshape(HEAD_DIM, 1),
        "kg_pad": jnp.pad(k_norm_g, (0, LANES - HEAD_DIM)).reshape(1, LANES),
        "qg_nat": jnp.tile(q_norm_g, N_HEADS).reshape(1, ATTN_W),
        "kg_nat": jnp.tile(k_norm_g, N_HEADS).reshape(1, ATTN_W),
        "gvg": gmlp_v_norm_g.reshape(1, GMLP_W),
        "bones": jnp.asarray((feat[:, None] // GROUP_DIM) == (feat[None, :] // GROUP_DIM), BF16),
        "wcat": jnp.transpose(gmlp_ws, (1, 0, 2)).reshape(GMLP_CHUNK, N_GROUPS * GMLP_CHUNK),
        "gbias": jnp.repeat(gmlp_b.T, GROUP_DIM, axis=1),
        "ws_one": jnp.repeat(gmlp_ws[:, 0, 0], GROUP_DIM).reshape(1, GMLP_W),
        "b_one": jnp.repeat(gmlp_b[:, 0], GROUP_DIM).reshape(1, GMLP_W),
        "gog": gmlp_out_norm_g.reshape(1, GMLP_W),
        "aog": attn_out_norm_g.reshape(1, ATTN_W),
        "wo_attn": w_out[:ATTN_W].astype(BF16),
        "wo_gmlp": w_out[ATTN_W:].astype(BF16),
        "g2": norm2_g.reshape(1, D_MODEL),
        "wqt": peer_wq.T.astype(BF16),
        "keys": peer_keys.reshape(2 * PEER_HEADS, PEER_KEYS, PEER_HALF),
        "u": peer_u.astype(BF16),
        "vt": peer_v.T.astype(BF16),
        "cand_idx": _cand_index_table(),
    }
    return w


def _sample_attention(page_table, cache_k, cache_v, q, qt, k_new, v_new, past_len):
    nb = q.shape[0]
    heads = lambda a: a.reshape(nb, N_HEADS, HEAD_DIM)
    as_stored = lambda cache: jnp.transpose(cache, (0, 2, 3, 1))
    p, pself, picked = _sample_scores(page_table, as_stored(cache_k), qt, heads(q), heads(k_new), past_len)
    blocks = picked[:, :, :MOBA_TOPK].astype(jnp.int32)
    logical = (blocks[..., None] * _PAGES_PER_BLOCK + jnp.arange(_PAGES_PER_BLOCK, dtype=jnp.int32)).reshape(nb, -1)
    phys = jnp.take_along_axis(page_table, logical, axis=1)
    ctx = _sample_values(phys, logical.astype(jnp.int32), as_stored(cache_v), p, pself, heads(v_new))
    return ctx.reshape(nb, ATTN_W)


def _layer_tail(x, attn, gm, w, *, route_tile, peer_tile, attn_transposed):
    h, xn2, m1, c1, r2, e2 = _route(x, attn, gm, w, tt=route_tile, attn_transposed=attn_transposed)
    return _peer(xn2, h, m1, c1, r2, e2, w, tt=peer_tile)


def kernel(x_prompt, x_sample, cache_k, cache_v, page_table, norm1_g, w_in, q_norm_g, k_norm_g, gmlp_v_norm_g,
           gmlp_ws, gmlp_b, attn_out_norm_g, gmlp_out_norm_g, w_out, norm2_g, peer_wq, peer_keys, peer_u, peer_v):
    w = _prepare(norm1_g, w_in, q_norm_g, k_norm_g, gmlp_v_norm_g, gmlp_ws, gmlp_b, attn_out_norm_g,
                 gmlp_out_norm_g, w_out, norm2_g, peer_wq, peer_keys, peer_u, peer_v)
    b, t, _ = x_prompt.shape
    assert b == 1
    xp = x_prompt.reshape(t, D_MODEL)
    k_p, v_p, kaug, qaug, vt, gm_p, gv_last = _proj_prompt(xp, w)
    attn_t = _moba_prompt(kaug, qaug, vt)
    y_p = _layer_tail(xp, attn_t, gm_p, w, route_tile=256, peer_tile=512, attn_transposed=True)

    nb, ds, _ = x_sample.shape
    assert ds == 1
    xs = x_sample.reshape(nb, D_MODEL)
    past_len = page_table.shape[1] * PAGE
    assert past_len % MOBA_BLOCK == 0
    k_s, v_s, gv_s, gm_s, q_s, qt_s = _proj_sample(xs, w)
    attn_s = _sample_attention(page_table, cache_k, cache_v, q_s, qt_s, k_s, v_s, past_len)
    y_s = _layer_tail(xs, attn_s, gm_s, w, route_tile=nb, peer_tile=nb, attn_transposed=False)

    return (y_p.reshape(1, t, D_MODEL), y_s.reshape(nb, 1, D_MODEL),
            k_p.reshape(1, t, N_HEADS, HEAD_DIM), v_p.reshape(1, t, N_HEADS, HEAD_DIM),
            gv_last.reshape(1, GMLP_CHUNK, GMLP_W),
            k_s.reshape(nb, 1, N_HEADS, HEAD_DIM), v_s.reshape(nb, 1, N_HEADS, HEAD_DIM),
            gv_s.reshape(nb, 1, GMLP_W))
```

```python
import functools
import math

import jax
import jax.numpy as jnp
import numpy as np
from jax import lax
from jax.experimental import pallas as pl
from jax.experimental.pallas import tpu as pltpu

F32 = jnp.float32
BF16 = jnp.bfloat16

D_MODEL = 1024
N_HEADS = 8
HEAD_DIM = 64
ATTN_W = N_HEADS * HEAD_DIM
GMLP_W = 512
N_GROUPS = 8
GROUP_DIM = GMLP_W // N_GROUPS
GMLP_CHUNK = 128
MOBA_BLOCK = 256
MOBA_TOPK = 3
PAGE = 128
PEER_HEADS = 8
PEER_KEYS = 128
PEER_HALF = 128
PEER_TOPK = 16
N_EXPERTS = PEER_KEYS * PEER_KEYS
EPS = 1e-6
NEG = -1e30
QK_SCALE = HEAD_DIM ** -0.5
GELU_C = math.sqrt(2.0 / math.pi)


def _bf16_pieces(x, n):
    out, rest = [], np.float32(x)
    for _ in range(n):
        piece = np.float32(rest.astype(jnp.bfloat16))
        out.append(float(piece))
        rest = np.float32(rest - piece)
    return tuple(out)


LOG2E = float(np.float32(math.log2(math.e)))
_LOG2E_PIECES = _bf16_pieces(LOG2E, 3)

LANES = 128
VMEM_LIMIT = 56 * 1024 * 1024

_CAND_ROWS = tuple((a, PEER_TOPK // (a + 1)) for a in range(8))


def _dot(a, b):
    return jnp.dot(a, b, preferred_element_type=F32)


def _dot_nt(a, b):
    return lax.dot_general(a, b, (((1,), (1,)), ((), ())), preferred_element_type=F32)


def _split(a):
    hi = a.astype(BF16)
    lo = (a - hi.astype(F32)).astype(BF16)
    return hi, lo


def _dot3(a, b):
    ah, al = _split(a)
    bh, bl = _split(b)
    return _dot(ah, bh) + (_dot(ah, bl) + _dot(al, bh))


def _rms(x, axis=-1):
    return x * lax.rsqrt(jnp.mean(x * x, axis=axis, keepdims=True) + EPS)


def _gelu(x):
    return x * (0.5 * (1.0 + jnp.tanh(GELU_C * (x + 0.044715 * (x * x * x)))))


def _gelu_sigmoid(x):
    z2 = x * ((-2.0 * GELU_C * 0.044715) * (x * x) - 2.0 * GELU_C)
    return x / (1.0 + jnp.exp(z2))


def _iota(shape, dim, dtype=jnp.int32):
    return lax.broadcasted_iota(dtype, shape, dim)


def _params(*sem):
    return pltpu.CompilerParams(dimension_semantics=sem, vmem_limit_bytes=VMEM_LIMIT)


def _full(shape):
    nd = len(shape)
    return pl.BlockSpec(shape, lambda *_: (0,) * nd)


_KP0, _KN0, _V0, _GU0, _GV0, _NAT_COLS = 0, 1024, 1536, 2048, 2560, 3072


def _gmlp_norm(gvr, bones_ref, gvg_ref):
    hi, lo = _split(gvr * gvr)
    ss = _dot(hi, bones_ref[...]) + _dot(lo, bones_ref[...])
    return gvr * lax.rsqrt(ss * (1.0 / GROUP_DIM) + EPS) * gvg_ref[...]


def _proj_prompt_kernel(x_ref, g1_ref, wnat_ref, wt_ref, qg_ref, kgp_ref, kgn_ref, gvg_ref, bones_ref,
                        wcat_ref, gbias_ref, gog_ref,
                        k_ref, v_ref, kaug_ref, qaug_ref, vt_ref, gm_ref, gvl_ref,
                        kmean_s, wtril_s):
    i = pl.program_id(0)
    tm = x_ref.shape[0]
    nblk = kmean_s.shape[1]

    @pl.when(i == 0)
    def _init():
        row = _iota((GMLP_CHUNK, N_GROUPS * GMLP_CHUNK), 0)
        col = _iota((GMLP_CHUNK, N_GROUPS * GMLP_CHUNK), 1) & (GMLP_CHUNK - 1)
        wtril_s[...] = jnp.where(col <= row, wcat_ref[...], 0.0).astype(BF16)
        kmean_s[...] = jnp.zeros(kmean_s.shape, F32)

    xn = (_rms(x_ref[...]) * g1_ref[...]).astype(BF16)
    hn = _dot(xn, wnat_ref[...])
    ht = _dot_nt(wt_ref[...], xn)

    lane = _iota((tm, LANES), 1)
    rowf = _iota((tm, LANES), 0).astype(F32)
    blk_row = _iota((nblk, tm), 0)
    blk_rowf = blk_row.astype(F32)
    piece_row = _iota((LANES, tm), 0)
    i_f = i.astype(F32)
    k_hi = jnp.where(lane < 3, rowf, jnp.where(lane < 6, float(MOBA_BLOCK) * i_f, 0.0))

    rs_heads = []
    for h in range(N_HEADS):
        kp = hn[:, _KP0 + LANES * h:_KP0 + LANES * (h + 1)]
        rs = lax.rsqrt(jnp.sum(kp * kp, axis=-1, keepdims=True) * (1.0 / HEAD_DIM) + EPS)
        rs_heads.append(rs)
        kn = kp * rs * kgp_ref[...]
        kmean_s[h, pl.ds(i, 1), :] = jnp.mean(kn, axis=0, keepdims=True)

        qt = ht[HEAD_DIM * h:HEAD_DIM * (h + 1), :]
        qn = qt * lax.rsqrt(jnp.sum(qt * qt, axis=0, keepdims=True) * (1.0 / HEAD_DIM) + EPS) * qg_ref[...]
        gate = _dot3(kmean_s[h], jnp.concatenate([qn, jnp.zeros_like(qn)], axis=0))
        cur = jnp.where(blk_row < i, gate, NEG)
        sel = jnp.zeros_like(cur)
        for _ in range(MOBA_TOPK):
            m = jnp.max(cur, axis=0, keepdims=True)
            first = jnp.min(jnp.where(cur == m, blk_rowf, float(nblk)), axis=0, keepdims=True)
            hit = blk_rowf == first
            sel = jnp.where(hit, 1.0, sel)
            cur = jnp.where(hit, -jnp.inf, cur)
        keep = jnp.where(blk_row < i, sel, jnp.where(blk_row == i, 1.0, 0.0))
        sel_bias = jnp.where(keep > 0.0, 0.0, NEG)
        pieces = [qn * (QK_SCALE * LOG2E), sel_bias]
        if nblk < HEAD_DIM:
            pieces.append(jnp.zeros((HEAD_DIM - nblk, tm), F32))
        slope = 2.0 ** -(h + 1)
        slope_rows = jnp.zeros((LANES, tm), F32)
        for r, piece in enumerate(_LOG2E_PIECES * 2):
            slope_rows = jnp.where(piece_row == r, slope * piece, slope_rows)
        qaug_ref[h] = jnp.concatenate(pieces + [slope_rows], axis=0).astype(BF16)

        k_lo = jnp.where(lane == HEAD_DIM + i, 1.0, kn)
        kaug_ref[h] = jnp.concatenate([k_lo, k_hi], axis=1).astype(BF16)
        vt_ref[h] = ht[ATTN_W + HEAD_DIM * h:ATTN_W + HEAD_DIM * (h + 1), :].astype(BF16)

    cols = []
    for c in range(ATTN_W // LANES):
        sc = jnp.where(lane < HEAD_DIM, rs_heads[2 * c], rs_heads[2 * c + 1])
        cols.append(hn[:, _KN0 + LANES * c:_KN0 + LANES * (c + 1)] * sc)
    k_ref[...] = jnp.concatenate(cols, axis=1) * kgn_ref[...]
    v_ref[...] = hn[:, _V0:_V0 + ATTN_W]

    gu = _gelu(hn[:, _GU0:_GU0 + GMLP_W])
    gvn = _gmlp_norm(_gelu(hn[:, _GV0:_GV0 + GMLP_W]), bones_ref, gvg_ref)
    gvl_ref[...] = gvn[tm - GMLP_CHUNK:, :]
    group_of_lane = _iota((1, GMLP_W), 1) >> 6
    outs = []
    for c in range(tm // GMLP_CHUNK):
        gc = gvn[GMLP_CHUNK * c:GMLP_CHUNK * (c + 1), :]
        stacked = jnp.concatenate([jnp.where(group_of_lane == g, gc, 0.0) for g in range(N_GROUPS)], axis=0)
        mixed = _dot(wtril_s[...], stacked.astype(BF16)) + gbias_ref[...]
        outs.append(gu[GMLP_CHUNK * c:GMLP_CHUNK * (c + 1), :] * mixed)
    gm_ref[...] = (_rms(jnp.concatenate(outs, axis=0)) * gog_ref[...]).astype(BF16)


def _proj_prompt(x, w, tm=MOBA_BLOCK):
    t = x.shape[0]
    nblk = t // tm
    assert nblk <= HEAD_DIM
    row_tile = lambda width: pl.BlockSpec((tm, width), lambda i: (i, 0))
    in_specs = [row_tile(D_MODEL), _full((1, D_MODEL)), _full((D_MODEL, _NAT_COLS)), _full((D_MODEL, D_MODEL)),
                _full((HEAD_DIM, 1)), _full((1, LANES)), _full((1, ATTN_W)), _full((1, GMLP_W)),
                _full((GMLP_W, GMLP_W)), _full((GMLP_CHUNK, N_GROUPS * GMLP_CHUNK)), _full((GMLP_CHUNK, GMLP_W)),
                _full((1, GMLP_W))]
    out_shape = (jax.ShapeDtypeStruct((t, ATTN_W), F32), jax.ShapeDtypeStruct((t, ATTN_W), F32),
                 jax.ShapeDtypeStruct((N_HEADS, t, 2 * LANES), BF16),
                 jax.ShapeDtypeStruct((N_HEADS, 2 * LANES, t), BF16),
                 jax.ShapeDtypeStruct((N_HEADS, HEAD_DIM, t), BF16),
                 jax.ShapeDtypeStruct((t, GMLP_W), BF16), jax.ShapeDtypeStruct((GMLP_CHUNK, GMLP_W), F32))
    out_specs = (row_tile(ATTN_W), row_tile(ATTN_W),
                 pl.BlockSpec((N_HEADS, tm, 2 * LANES), lambda i: (0, i, 0)),
                 pl.BlockSpec((N_HEADS, 2 * LANES, tm), lambda i: (0, 0, i)),
                 pl.BlockSpec((N_HEADS, HEAD_DIM, tm), lambda i: (0, 0, i)),
                 row_tile(GMLP_W), _full((GMLP_CHUNK, GMLP_W)))
    return pl.pallas_call(
        _proj_prompt_kernel, grid=(nblk,), in_specs=in_specs, out_specs=out_specs, out_shape=out_shape,
        scratch_shapes=[pltpu.VMEM((N_HEADS, nblk, LANES), F32),
                        pltpu.VMEM((GMLP_CHUNK, N_GROUPS * GMLP_CHUNK), BF16)],
        compiler_params=_params("arbitrary"), name="proj_prompt",
    )(x, w["g1"], w["wnat"], w["wt"], w["qg_col"], w["kg_pad"], w["kg_nat"], w["gvg"], w["bones"],
      w["wcat"], w["gbias"], w["gog"])


_MOBA_HEADS_PER_STEP = 4


def _moba_kernel(kaug_ref, qaug_ref, vt_ref, out_ref, s_even, s_odd):
    j = pl.program_id(1)
    tq = qaug_ref.shape[2]
    heads = range(_MOBA_HEADS_PER_STEP)

    def block_start(n):
        return pl.multiple_of(n * MOBA_BLOCK, MOBA_BLOCK)

    def produce(buf, n):
        for hh in heads:
            buf[hh] = _dot(kaug_ref[hh, pl.ds(block_start(n), MOBA_BLOCK), :], qaug_ref[hh])

    def fold(state, s, hh, n):
        m, l, acc = state
        m_new = jnp.maximum(m, jnp.max(s, axis=0, keepdims=True))
        alpha = jnp.exp2(m - m_new)
        p = jnp.exp2(s - m_new)
        l = l * alpha + jnp.sum(p, axis=0, keepdims=True)
        acc = acc * alpha + _dot(vt_ref[hh, :, pl.ds(block_start(n), MOBA_BLOCK)], p.astype(BF16))
        return m_new, l, acc

    def body(i, states):
        produce(s_odd, 2 * i + 1)
        states = tuple(fold(states[hh], s_even[hh], hh, 2 * i) for hh in heads)
        produce(s_even, 2 * i + 2)
        return tuple(fold(states[hh], s_odd[hh], hh, 2 * i + 1) for hh in heads)

    init = tuple((jnp.full((1, tq), -jnp.inf, F32), jnp.zeros((1, tq), F32), jnp.zeros((HEAD_DIM, tq), F32))
                 for _ in heads)
    produce(s_even, 0)
    states = lax.fori_loop(0, j // 2, body, init)

    causal = _iota((MOBA_BLOCK, tq), 0) <= _iota((MOBA_BLOCK, tq), 1)
    j_odd = (j & 1) == 1
    produce(s_odd, j)
    for hh in heads:
        first = jnp.where(j_odd, s_even[hh], jnp.where(causal, s_even[hh], NEG))
        state = fold(states[hh], first, hh, 2 * (j // 2))
        second = jnp.where(j_odd, jnp.where(causal, s_odd[hh], NEG), NEG)
        _, l, acc = fold(state, second, hh, j)
        out_ref[HEAD_DIM * hh:HEAD_DIM * (hh + 1), :] = acc / l


def _moba_prompt(kaug, qaug, vt):
    t = kaug.shape[1]
    tq = MOBA_BLOCK
    hs = _MOBA_HEADS_PER_STEP
    once = pl.Buffered(1)
    return pl.pallas_call(
        _moba_kernel, grid=(N_HEADS // hs, t // tq),
        in_specs=[pl.BlockSpec((hs, t, 2 * LANES), lambda g, j: (g, 0, 0), pipeline_mode=once),
                  pl.BlockSpec((hs, 2 * LANES, tq), lambda g, j: (g, 0, j)),
                  pl.BlockSpec((hs, HEAD_DIM, t), lambda g, j: (g, 0, 0), pipeline_mode=once)],
        out_specs=pl.BlockSpec((hs * HEAD_DIM, tq), lambda g, j: (g, j)),
        out_shape=jax.ShapeDtypeStruct((ATTN_W, t), F32),
        scratch_shapes=[pltpu.VMEM((hs, MOBA_BLOCK, tq), F32), pltpu.VMEM((hs, MOBA_BLOCK, tq), F32)],
        compiler_params=_params("parallel", "arbitrary"), name="moba_prompt",
    )(kaug, qaug, vt)


def _cand_index_table():
    rows = []
    for a, nb in _CAND_ROWS:
        for c0 in range(0, max(nb, 8), 8):
            rows.append([a * PEER_TOPK + c0 + b for b in range(8)])
    rows.append([(8 + a) * PEER_TOPK for a in range(8)])
    flat = np.asarray(rows, np.float32).reshape(-1, 1)
    return jnp.asarray(np.broadcast_to(flat, (flat.shape[0], LANES)).copy())


_N_CAND = 8 * (sum(max(nb, 8) // 8 for _, nb in _CAND_ROWS) + 1)


def _extract_top(cur, idx, n, on_hit):
    big = float(1 << 20)
    for r in range(n):
        m = jnp.max(cur, axis=0, keepdims=True)
        first = jnp.min(jnp.where(cur == m, idx, big), axis=0, keepdims=True)
        hit = idx == first
        cur = jnp.where(hit, -jnp.inf, cur)
        on_hit(r, m, hit)
    return cur


def _route_kernel(x_ref, attn_ref, gm_ref, aog_ref, woa_ref, wog_ref, g2_ref, wqt_ref, keys_ref, cidx_ref,
                  h_ref, xn2_ref, m1_ref, c1_ref, r2_ref, e2_ref,
                  s_s, rank_s, sv_s, na_s, *, attn_transposed):
    tt = x_ref.shape[0]
    at = attn_ref[...]
    if attn_transposed:
        at = at.T
    an = (_rms(at) * aog_ref[...]).astype(BF16)
    h = x_ref[...] + _dot(an, woa_ref[...]) + _dot(gm_ref[...], wog_ref[...])
    h_ref[...] = h
    xn2 = (_rms(h) * g2_ref[...]).astype(BF16)
    xn2_ref[...] = xn2
    qt = _dot_nt(wqt_ref[...], xn2)
    for hp in range(2 * PEER_HEADS):
        s_s[hp] = _dot3(keys_ref[hp], qt[PEER_HALF * hp:PEER_HALF * (hp + 1), :])

    key_idx = _iota((PEER_KEYS, LANES), 0).astype(F32)
    row8 = _iota((8, LANES), 0)

    def rank_distinct(hp, lanes):
        cur = s_s[hp, :, lanes]
        rank = jnp.full((PEER_KEYS, LANES), float(PEER_TOPK), F32)
        for r in range(PEER_TOPK):
            m = jnp.max(cur, axis=0, keepdims=True)
            hit = cur == m
            cur = jnp.where(hit, -jnp.inf, cur)
            rank = jnp.where(hit, float(r), rank)
            sv_s[hp, r:r + 1, lanes] = m
        rank_s[hp, :, lanes] = rank
        return jnp.sum(jnp.where(rank < float(PEER_TOPK), 1.0, 0.0), axis=0, keepdims=True)

    def rank_exact(hp, lanes):
        rank = [jnp.full((PEER_KEYS, LANES), float(PEER_TOPK), F32)]

        def on_hit(r, m, hit):
            rank[0] = jnp.where(hit, float(r), rank[0])
            sv_s[hp, r:r + 1, lanes] = m

        _extract_top(s_s[hp, :, lanes], key_idx, PEER_TOPK, on_hit)
        rank_s[hp, :, lanes] = rank[0]

    def per_head(hd, carry):
        ranked = []
        for sl in range(tt // LANES):
            lanes = slice(LANES * sl, LANES * (sl + 1))
            for hp in (2 * hd, 2 * hd + 1):
                ranked.append((hp, lanes, rank_distinct(hp, lanes)))
        for hp, lanes, n_ranked in ranked:
            @pl.when(jnp.max(jnp.abs(n_ranked - float(PEER_TOPK))) > 0.0)
            def _redo(hp=hp, lanes=lanes):
                rank_exact(hp, lanes)
        for sl in range(tt // LANES):
            lanes = slice(LANES * sl, LANES * (sl + 1))
            sv1 = sv_s[2 * hd, :, lanes]
            sv2 = sv_s[2 * hd + 1, :, lanes]
            pieces = []
            for a, nb in _CAND_ROWS:
                for c0 in range(0, max(nb, 8), 8):
                    piece = sv1[a:a + 1, :] + sv2[c0:c0 + 8, :]
                    pieces.append(piece if nb >= 8 else jnp.where(row8 < nb, piece, -jnp.inf))
            pieces.append(sv1[8:16, :] + sv2[0:1, :])
            cand = jnp.concatenate(pieces, axis=0)
            chosen = [jnp.zeros_like(cand)]

            def on_hit(r, m, hit):
                chosen[0] = jnp.where(hit, 1.0, chosen[0])

            _extract_top(cand, cidx_ref[...], PEER_TOPK, on_hit)
            chosen = chosen[0]
            top = sv1[0:1, :] + sv2[0:1, :]
            z = jnp.sum(jnp.where(chosen > 0.0, jnp.exp(cand - top), 0.0), axis=0, keepdims=True)
            row = 0
            for a, nb in _CAND_ROWS:
                nrows = max(nb, 8)
                na_s[sl, a:a + 1, :] = jnp.sum(chosen[row:row + nrows, :], axis=0, keepdims=True)
                row += nrows
            na_s[sl, 8:16, :] = chosen[row:row + 8, :]
            na = na_s[sl]
            rank1 = rank_s[2 * hd, :, lanes]
            m1 = jnp.zeros((PEER_KEYS, LANES), F32)
            for a in range(PEER_TOPK):
                m1 = jnp.where(rank1 == float(a), na[a:a + 1, :], m1)
            m1_ref[hd, :, lanes] = m1
            c1_ref[hd, :, lanes] = jnp.exp(s_s[2 * hd, :, lanes] - sv1[0:1, :]) / z
            r2_ref[hd, :, lanes] = rank_s[2 * hd + 1, :, lanes].astype(BF16)
            e2_ref[hd, :, lanes] = jnp.exp(s_s[2 * hd + 1, :, lanes] - sv2[0:1, :]).astype(BF16)
        return carry

    lax.fori_loop(0, PEER_HEADS, per_head, 0)


def _route(x, attn, gm, w, *, tt, attn_transposed):
    t = x.shape[0]
    row_tile = lambda width: pl.BlockSpec((tt, width), lambda i: (i, 0))
    attn_spec = pl.BlockSpec((ATTN_W, tt), lambda i: (0, i)) if attn_transposed else row_tile(ATTN_W)
    head_tile = pl.BlockSpec((PEER_HEADS, PEER_KEYS, tt), lambda i: (0, 0, i))
    in_specs = [row_tile(D_MODEL), attn_spec, row_tile(GMLP_W), _full((1, ATTN_W)),
                _full((ATTN_W, D_MODEL)), _full((GMLP_W, D_MODEL)), _full((1, D_MODEL)),
                _full((2 * PEER_HEADS * PEER_HALF, D_MODEL)), _full((2 * PEER_HEADS, PEER_KEYS, PEER_HALF)),
                _full((_N_CAND, LANES))]
    stat = lambda dt: jax.ShapeDtypeStruct((PEER_HEADS, PEER_KEYS, t), dt)
    out_shape = (jax.ShapeDtypeStruct((t, D_MODEL), F32), jax.ShapeDtypeStruct((t, D_MODEL), BF16),
                 stat(F32), stat(F32), stat(BF16), stat(BF16))
    out_specs = (row_tile(D_MODEL), row_tile(D_MODEL), head_tile, head_tile, head_tile, head_tile)
    return pl.pallas_call(
        functools.partial(_route_kernel, attn_transposed=attn_transposed),
        grid=(t // tt,), in_specs=in_specs, out_specs=out_specs, out_shape=out_shape,
        scratch_shapes=[pltpu.VMEM((2 * PEER_HEADS, PEER_KEYS, tt), F32),
                        pltpu.VMEM((2 * PEER_HEADS, PEER_KEYS, tt), F32),
                        pltpu.VMEM((2 * PEER_HEADS, PEER_TOPK, tt), F32),
                        pltpu.VMEM((tt // LANES, PEER_TOPK, LANES), F32)],
        compiler_params=_params("parallel"), name="route",
    )(x, attn, gm, w["aog"], w["wo_attn"], w["wo_gmlp"], w["g2"], w["wqt"], w["keys"], w["cand_idx"])


def _peer_kernel(xn2_ref, u_ref, vt_ref, m1_ref, c1_ref, r2_ref, e2_ref, h_ref, y_ref, acc_s, act_even_s, act_odd_s,
                 gated_s, *, keys_per_step):
    c = pl.program_id(1)
    n_blocks = pl.num_programs(1) - 1

    @pl.when(c == 0)
    def _zero():
        acc_s[...] = jnp.zeros(acc_s.shape, F32)
        act_odd_s[...] = jnp.zeros(act_odd_s.shape, BF16)

    def step(act_write, act_read):
        act_write[...] = _gelu_sigmoid(_dot_nt(u_ref[...], xn2_ref[...])).astype(BF16)
        first_key = jnp.maximum(c - 1, 0) * keys_per_step
        for ii in range(keys_per_step):
            i = first_key + ii
            rows = slice(PEER_KEYS * ii, PEER_KEYS * (ii + 1))
            g = None
            for hd in range(PEER_HEADS):
                partners = m1_ref[hd, pl.ds(i, 1), :].astype(BF16)
                weight = c1_ref[hd, pl.ds(i, 1), :].astype(BF16)
                term = jnp.where(r2_ref[hd] < partners, e2_ref[hd], jnp.zeros((), BF16)) * weight
                g = term if g is None else g + term
            gated_s[rows, :] = g * act_read[rows, :]
        acc_s[...] += _dot(vt_ref[...], gated_s[...])

    @pl.when((c & 1) == 0)
    def _even():
        step(act_even_s, act_odd_s)

    @pl.when((c & 1) == 1)
    def _odd():
        step(act_odd_s, act_even_s)

    @pl.when(c == n_blocks)
    def _finish():
        y_ref[...] = h_ref[...] + acc_s[...].T


def _peer(xn2, h, m1, c1, r2, e2, w, *, tt, keys_per_step=8):
    t = xn2.shape[0]
    ne = keys_per_step * PEER_KEYS
    n_blocks = N_EXPERTS // ne
    head_tile = pl.BlockSpec((PEER_HEADS, PEER_KEYS, tt), lambda j, c: (0, 0, j))
    row_tile = pl.BlockSpec((tt, D_MODEL), lambda j, c: (j, 0))
    return pl.pallas_call(
        functools.partial(_peer_kernel, keys_per_step=keys_per_step),
        grid=(t // tt, n_blocks + 1),
        in_specs=[row_tile, pl.BlockSpec((ne, D_MODEL), lambda j, c: (jnp.minimum(c, n_blocks - 1), 0)),
                  pl.BlockSpec((D_MODEL, ne), lambda j, c: (0, jnp.maximum(c - 1, 0))),
                  head_tile, head_tile, head_tile, head_tile, row_tile],
        out_specs=row_tile, out_shape=jax.ShapeDtypeStruct((t, D_MODEL), F32),
        scratch_shapes=[pltpu.VMEM((D_MODEL, tt), F32)] + [pltpu.VMEM((ne, tt), BF16)] * 3,
        compiler_params=_params("parallel", "arbitrary"), name="peer",
    )(xn2, w["u"], w["vt"], m1, c1, r2, e2, h)


def _proj_sample_kernel(x_ref, g1_ref, win_ref, bones_ref, qgn_ref, kgn_ref, gvg_ref, wsc_ref, bsc_ref, gog_ref,
                        k_ref, v_ref, gv_ref, gm_ref, q_ref, qt_ref):
    xn = (_rms(x_ref[...]) * g1_ref[...]).astype(BF16)
    hn = _dot(xn, win_ref[...])

    def head_norm(z, g_ref):
        hi, lo = _split(z * z)
        ss = _dot(hi, bones_ref[...]) + _dot(lo, bones_ref[...])
        return z * lax.rsqrt(ss * (1.0 / HEAD_DIM) + EPS) * g_ref[...]

    qn = head_norm(hn[:, 0:ATTN_W], qgn_ref)
    kn = head_norm(hn[:, ATTN_W:2 * ATTN_W], kgn_ref)
    k_ref[...] = kn
    v_ref[...] = hn[:, 2 * ATTN_W:3 * ATTN_W]
    gu = _gelu(hn[:, 3 * ATTN_W:3 * ATTN_W + GMLP_W])
    gvn = _gmlp_norm(_gelu(hn[:, 3 * ATTN_W + GMLP_W:]), bones_ref, gvg_ref)
    gv_ref[...] = gvn
    gm_ref[...] = (_rms(gu * (wsc_ref[...] * gvn + bsc_ref[...])) * gog_ref[...]).astype(BF16)
    qs = qn * QK_SCALE
    q_ref[...] = qs
    qt_ref[...] = qs.T


def _proj_sample(x, w):
    nb = x.shape[0]
    out_shape = (jax.ShapeDtypeStruct((nb, ATTN_W), F32), jax.ShapeDtypeStruct((nb, ATTN_W), F32),
                 jax.ShapeDtypeStruct((nb, GMLP_W), F32), jax.ShapeDtypeStruct((nb, GMLP_W), BF16),
                 jax.ShapeDtypeStruct((nb, ATTN_W), F32), jax.ShapeDtypeStruct((ATTN_W, nb), F32))
    return pl.pallas_call(
        _proj_sample_kernel, out_shape=out_shape, compiler_params=_params(), name="proj_sample",
    )(x, w["g1"], w["win"], w["bones"], w["qg_nat"], w["kg_nat"], w["gvg"], w["ws_one"], w["b_one"], w["gog"])


_PAGES_PER_STEP = 16
_PAGES_PER_BLOCK = MOBA_BLOCK // PAGE
_PAGE_BLOCK = (None, N_HEADS, HEAD_DIM, PAGE)


def _sample_scores_kernel(pt_ref, qt_ref, q_ref, knew_ref, *refs, n_steps, past_len):
    k_refs = refs[:_PAGES_PER_STEP]
    p_ref, pself_ref, sel_ref, qcol_s, sc_s, gate_s = refs[_PAGES_PER_STEP:]
    d = pl.program_id(0)
    c = pl.program_id(1)
    n_seq = qt_ref.shape[1]
    lane = _iota((N_HEADS, LANES), 1)
    blocks_per_step = _PAGES_PER_STEP // _PAGES_PER_BLOCK

    @pl.when(c == 0)
    def _init():
        pick = jnp.where(_iota((n_seq, LANES), 0) == d, 1.0, 0.0)
        qcol_s[...] = _dot3(qt_ref[...], pick)
        gate_s[...] = jnp.zeros(gate_s.shape, F32)

    gates = gate_s[...]
    for b in range(blocks_per_step):
        blk = None
        for g in range(_PAGES_PER_BLOCK):
            r = b * _PAGES_PER_BLOCK + g
            rows = [jnp.sum(k_refs[r][hd] * qcol_s[HEAD_DIM * hd:HEAD_DIM * (hd + 1), :], axis=0, keepdims=True)
                    for hd in range(N_HEADS)]
            s_page = jnp.concatenate(rows, axis=0)
            sc_s[:, pl.ds(pl.multiple_of((c * _PAGES_PER_STEP + r) * PAGE, PAGE), PAGE)] = s_page
            blk = s_page if blk is None else blk + s_page
        gate = jnp.sum(blk, axis=-1, keepdims=True) * (1.0 / MOBA_BLOCK)
        gates = jnp.where(lane == c * blocks_per_step + b, gate, gates)
    gate_s[...] = gates

    @pl.when(c == n_steps - 1)
    def _select():
        n_blocks = n_steps * blocks_per_step
        lane_f = lane.astype(F32)
        cur = jnp.where(lane < n_blocks, gates, -jnp.inf)
        sel = jnp.zeros_like(cur)
        picked = jnp.zeros_like(cur)
        for r in range(MOBA_TOPK):
            m = jnp.max(cur, axis=-1, keepdims=True)
            first = jnp.min(jnp.where(cur == m, lane_f, float(LANES)), axis=-1, keepdims=True)
            hit = lane_f == first
            sel = jnp.where(hit, 1.0, sel)
            picked = jnp.where(lane == r, first, picked)
            cur = jnp.where(hit, -jnp.inf, cur)
        sel_ref[...] = picked

        n_pos = n_blocks * MOBA_BLOCK
        chosen = jnp.concatenate([jnp.broadcast_to(sel[:, n:n + 1], (N_HEADS, MOBA_BLOCK)) for n in range(n_blocks)],
                                 axis=1)
        slopes = jnp.exp2(-(_iota((N_HEADS, 1), 0) + 1).astype(F32))
        distance = float(past_len) - _iota((1, n_pos), 1).astype(F32)
        logit = jnp.where(chosen > 0.0, sc_s[...] - slopes * distance, NEG)
        self_logit = jnp.sum(q_ref[...] * knew_ref[...], axis=-1, keepdims=True)
        top = jnp.maximum(jnp.max(logit, axis=-1, keepdims=True), self_logit)
        p = jnp.exp(logit - top)
        p_self = jnp.exp(self_logit - top)
        inv = 1.0 / (jnp.sum(p, axis=-1, keepdims=True) + p_self)
        p_ref[...] = p * inv
        pself_ref[...] = jnp.broadcast_to(p_self * inv, pself_ref.shape)


def _sample_scores(page_table, cache_kt, qt, q, k_new, past_len):
    nb, n_pages = page_table.shape
    n_steps = n_pages // _PAGES_PER_STEP
    n_blocks = n_pages // _PAGES_PER_BLOCK
    assert n_blocks <= LANES
    page_spec = lambda r: pl.BlockSpec(
        _PAGE_BLOCK, lambda d, c, pt, r=r: (pt[d, c * _PAGES_PER_STEP + r], 0, 0, 0))
    per_seq = lambda shape: pl.BlockSpec((None,) + shape, lambda d, c, pt: (d,) + (0,) * len(shape))
    grid_spec = pltpu.PrefetchScalarGridSpec(
        num_scalar_prefetch=1, grid=(nb, n_steps),
        in_specs=[pl.BlockSpec((ATTN_W, nb), lambda d, c, pt: (0, 0)), per_seq((N_HEADS, HEAD_DIM)),
                  per_seq((N_HEADS, HEAD_DIM))] + [page_spec(r) for r in range(_PAGES_PER_STEP)],
        out_specs=(per_seq((N_HEADS, n_pages * PAGE)), per_seq((N_HEADS, HEAD_DIM)), per_seq((N_HEADS, LANES))),
        scratch_shapes=[pltpu.VMEM((ATTN_W, LANES), F32), pltpu.VMEM((N_HEADS, n_pages * PAGE), F32),
                        pltpu.VMEM((N_HEADS, LANES), F32)])
    out_shape = (jax.ShapeDtypeStruct((nb, N_HEADS, n_pages * PAGE), F32),
                 jax.ShapeDtypeStruct((nb, N_HEADS, HEAD_DIM), F32), jax.ShapeDtypeStruct((nb, N_HEADS, LANES), F32))
    return pl.pallas_call(
        functools.partial(_sample_scores_kernel, n_steps=n_steps, past_len=past_len),
        grid_spec=grid_spec, out_shape=out_shape,
        compiler_params=_params("parallel", "arbitrary"), name="sample_scores",
    )(page_table, qt, q, k_new, *([cache_kt] * _PAGES_PER_STEP))


_SLABS_PER_HEAD = MOBA_TOPK * _PAGES_PER_BLOCK


def _sample_values_kernel(phys_ref, logical_ref, p_ref, pself_ref, vnew_ref, *refs):
    n_slabs = N_HEADS * _SLABS_PER_HEAD
    v_refs = refs[:n_slabs]
    out_ref = refs[n_slabs]
    d = pl.program_id(0)
    cols = []
    for hd in range(N_HEADS):
        acc = None
        for k in range(_SLABS_PER_HEAD):
            i = hd * _SLABS_PER_HEAD + k
            start = pl.multiple_of(logical_ref[d, i] * PAGE, PAGE)
            term = v_refs[i][...] * p_ref[hd:hd + 1, pl.ds(start, PAGE)]
            acc = term if acc is None else acc + term
        cols.append(jnp.sum(acc, axis=-1, keepdims=True))
    ctx_t = jnp.concatenate(cols + [jnp.zeros((HEAD_DIM, LANES - N_HEADS), F32)], axis=1)
    out_ref[...] = ctx_t.T[0:N_HEADS, :] + pself_ref[...] * vnew_ref[...]


def _sample_values(slab_phys, slab_logical, cache_vt, p, pself, v_new):
    nb, n_slabs = slab_phys.shape
    assert n_slabs == N_HEADS * _SLABS_PER_HEAD
    slab_spec = lambda i: pl.BlockSpec(
        (None, None, HEAD_DIM, PAGE), lambda d, ph, lg, i=i: (ph[d, i], i // _SLABS_PER_HEAD, 0, 0))
    per_seq = lambda shape: pl.BlockSpec((None,) + shape, lambda d, ph, lg: (d,) + (0,) * len(shape))
    head_rows = per_seq((N_HEADS, HEAD_DIM))
    grid_spec = pltpu.PrefetchScalarGridSpec(
        num_scalar_prefetch=2, grid=(nb,),
        in_specs=[per_seq((N_HEADS, p.shape[2])), head_rows, head_rows] + [slab_spec(i) for i in range(n_slabs)],
        out_specs=head_rows)
    return pl.pallas_call(
        _sample_values_kernel, grid_spec=grid_spec, out_shape=jax.ShapeDtypeStruct((nb, N_HEADS, HEAD_DIM), F32),
        compiler_params=_params("parallel"), name="sample_values",
    )(slab_phys, slab_logical, p, pself, v_new, *([cache_vt] * n_slabs))


def _prepare(norm1_g, w_in, q_norm_g, k_norm_g, gmlp_v_norm_g, gmlp_ws, gmlp_b, attn_out_norm_g, gmlp_out_norm_g,
             w_out, norm2_g, peer_wq, peer_keys, peer_u, peer_v):
    wq, wk, wv, wgu, wgv = jnp.split(w_in, [ATTN_W, 2 * ATTN_W, 3 * ATTN_W, 3 * ATTN_W + GMLP_W], axis=1)
    wk_pad = jnp.pad(wk.reshape(D_MODEL, N_HEADS, HEAD_DIM), ((0, 0), (0, 0), (0, LANES - HEAD_DIM)))
    feat = np.arange(ATTN_W)
    w = {
        "g1": norm1_g.reshape(1, D_MODEL),
        "win": w_in.astype(BF16),
        "wnat": jnp.concatenate([wk_pad.reshape(D_MODEL, N_HEADS * LANES), wk, wv, wgu, wgv], axis=1).astype(BF16),
        "wt": jnp.concatenate([wq, wv], axis=1).T.astype(BF16),
        "qg_col": q_norm_g.reshape(HEAD_DIM, 1),
        "kg_pad": jnp.pad(k_norm_g, (0, LANES - HEAD_DIM)).reshape(1, LANES),
        "qg_nat": jnp.tile(q_norm_g, N_HEADS).reshape(1, ATTN_W),
        "kg_nat": jnp.tile(k_norm_g, N_HEADS).reshape(1, ATTN_W),
        "gvg": gmlp_v_norm_g.reshape(1, GMLP_W),
        "bones": jnp.asarray((feat[:, None] // GROUP_DIM) == (feat[None, :] // GROUP_DIM), BF16),
        "wcat": jnp.transpose(gmlp_ws, (1, 0, 2)).reshape(GMLP_CHUNK, N_GROUPS * GMLP_CHUNK),
        "gbias": jnp.repeat(gmlp_b.T, GROUP_DIM, axis=1),
        "ws_one": jnp.repeat(gmlp_ws[:, 0, 0], GROUP_DIM).reshape(1, GMLP_W),
        "b_one": jnp.repeat(gmlp_b[:, 0], GROUP_DIM).reshape(1, GMLP_W),
        "gog": gmlp_out_norm_g.reshape(1, GMLP_W),
        "aog": attn_out_norm_g.reshape(1, ATTN_W),
        "wo_attn": w_out[:ATTN_W].astype(BF16),
        "wo_gmlp": w_out[ATTN_W:].astype(BF16),
        "g2": norm2_g.reshape(1, D_MODEL),
        "wqt": peer_wq.T.astype(BF16),
        "keys": peer_keys.reshape(2 * PEER_HEADS, PEER_KEYS, PEER_HALF),
        "u": peer_u.astype(BF16),
        "vt": peer_v.T.astype(BF16),
        "cand_idx": _cand_index_table(),
    }
    return w


def _sample_attention(page_table, cache_k, cache_v, q, qt, k_new, v_new, past_len):
    nb = q.shape[0]
    heads = lambda a: a.reshape(nb, N_HEADS, HEAD_DIM)
    as_stored = lambda cache: jnp.transpose(cache, (0, 2, 3, 1))
    p, pself, picked = _sample_scores(page_table, as_stored(cache_k), qt, heads(q), heads(k_new), past_len)
    blocks = picked[:, :, :MOBA_TOPK].astype(jnp.int32)
    logical = (blocks[..., None] * _PAGES_PER_BLOCK + jnp.arange(_PAGES_PER_BLOCK, dtype=jnp.int32)).reshape(nb, -1)
    phys = jnp.take_along_axis(page_table, logical, axis=1)
    ctx = _sample_values(phys, logical.astype(jnp.int32), as_stored(cache_v), p, pself, heads(v_new))
    return ctx.reshape(nb, ATTN_W)


def _layer_tail(x, attn, gm, w, *, route_tile, peer_tile, attn_transposed):
    h, xn2, m1, c1, r2, e2 = _route(x, attn, gm, w, tt=route_tile, attn_transposed=attn_transposed)
    return _peer(xn2, h, m1, c1, r2, e2, w, tt=peer_tile)


def kernel(x_prompt, x_sample, cache_k, cache_v, page_table, norm1_g, w_in, q_norm_g, k_norm_g, gmlp_v_norm_g,
           gmlp_ws, gmlp_b, attn_out_norm_g, gmlp_out_norm_g, w_out, norm2_g, peer_wq, peer_keys, peer_u, peer_v):
    w = _prepare(norm1_g, w_in, q_norm_g, k_norm_g, gmlp_v_norm_g, gmlp_ws, gmlp_b, attn_out_norm_g,
                 gmlp_out_norm_g, w_out, norm2_g, peer_wq, peer_keys, peer_u, peer_v)
    b, t, _ = x_prompt.shape
    assert b == 1
    xp = x_prompt.reshape(t, D_MODEL)
    k_p, v_p, kaug, qaug, vt, gm_p, gv_last = _proj_prompt(xp, w)
    attn_t = _moba_prompt(kaug, qaug, vt)
    y_p = _layer_tail(xp, attn_t, gm_p, w, route_tile=512, peer_tile=512, attn_transposed=True)

    nb, ds, _ = x_sample.shape
    assert ds == 1
    xs = x_sample.reshape(nb, D_MODEL)
    past_len = page_table.shape[1] * PAGE
    assert past_len % MOBA_BLOCK == 0
    k_s, v_s, gv_s, gm_s, q_s, qt_s = _proj_sample(xs, w)
    attn_s = _sample_attention(page_table, cache_k, cache_v, q_s, qt_s, k_s, v_s, past_len)
    y_s = _layer_tail(xs, attn_s, gm_s, w, route_tile=nb, peer_tile=nb, attn_transposed=False)

    return (y_p.reshape(1, t, D_MODEL), y_s.reshape(nb, 1, D_MODEL),
            k_p.reshape(1, t, N_HEADS, HEAD_DIM), v_p.reshape(1, t, N_HEADS, HEAD_DIM),
            gv_last.reshape(1, GMLP_CHUNK, GMLP_W),
            k_s.reshape(nb, 1, N_HEADS, HEAD_DIM), v_s.reshape(nb, 1, N_HEADS, HEAD_DIM),
            gv_s.reshape(nb, 1, GMLP_W))
```

```python
import functools
import math

import jax
import jax.numpy as jnp
import numpy as np
from jax import lax
from jax.experimental import pallas as pl
from jax.experimental.pallas import tpu as pltpu

F32 = jnp.float32
BF16 = jnp.bfloat16

D_MODEL = 1024
N_HEADS = 8
HEAD_DIM = 64
ATTN_W = N_HEADS * HEAD_DIM
GMLP_W = 512
N_GROUPS = 8
GROUP_DIM = GMLP_W // N_GROUPS
GMLP_CHUNK = 128
MOBA_BLOCK = 256
MOBA_TOPK = 3
PAGE = 128
PEER_HEADS = 8
PEER_KEYS = 128
PEER_HALF = 128
PEER_TOPK = 16
N_EXPERTS = PEER_KEYS * PEER_KEYS
EPS = 1e-6
NEG = -1e30
QK_SCALE = HEAD_DIM ** -0.5
GELU_C = math.sqrt(2.0 / math.pi)


def _bf16_pieces(x, n):
    out, rest = [], np.float32(x)
    for _ in range(n):
        piece = np.float32(rest.astype(jnp.bfloat16))
        out.append(float(piece))
        rest = np.float32(rest - piece)
    return tuple(out)


LOG2E = float(np.float32(math.log2(math.e)))
_LOG2E_PIECES = _bf16_pieces(LOG2E, 3)

LANES = 128
VMEM_LIMIT = 56 * 1024 * 1024

_CAND_ROWS = tuple((a, PEER_TOPK // (a + 1)) for a in range(8))


def _dot(a, b):
    return jnp.dot(a, b, preferred_element_type=F32)


def _dot_nt(a, b):
    return lax.dot_general(a, b, (((1,), (1,)), ((), ())), preferred_element_type=F32)


def _split(a):
    hi = a.astype(BF16)
    lo = (a - hi.astype(F32)).astype(BF16)
    return hi, lo


def _dot3(a, b):
    ah, al = _split(a)
    bh, bl = _split(b)
    return _dot(ah, bh) + (_dot(ah, bl) + _dot(al, bh))


def _rms(x, axis=-1):
    return x * lax.rsqrt(jnp.mean(x * x, axis=axis, keepdims=True) + EPS)


def _gelu(x):
    return x * (0.5 * (1.0 + jnp.tanh(GELU_C * (x + 0.044715 * (x * x * x)))))


def _gelu_sigmoid(x):
    z2 = x * ((-2.0 * GELU_C * 0.044715) * (x * x) - 2.0 * GELU_C)
    return x / (1.0 + jnp.exp(z2))


def _iota(shape, dim, dtype=jnp.int32):
    return lax.broadcasted_iota(dtype, shape, dim)


def _params(*sem):
    return pltpu.CompilerParams(dimension_semantics=sem, vmem_limit_bytes=VMEM_LIMIT)


def _full(shape):
    nd = len(shape)
    return pl.BlockSpec(shape, lambda *_: (0,) * nd)


_KP0, _KN0, _V0, _GU0, _GV0, _NAT_COLS = 0, 1024, 1536, 2048, 2560, 3072
_VT_ROWS = HEAD_DIM + 16


def _gmlp_norm(gvr, bones_ref, gvg_ref):
    hi, lo = _split(gvr * gvr)
    ss = _dot(hi, bones_ref[...]) + _dot(lo, bones_ref[...])
    return gvr * lax.rsqrt(ss * (1.0 / GROUP_DIM) + EPS) * gvg_ref[...]


def _proj_prompt_kernel(x_ref, g1_ref, wnat_ref, wt_ref, qg_ref, kgp_ref, kgn_ref, gvg_ref, bones_ref,
                        wcat_ref, gbias_ref, gog_ref,
                        k_ref, v_ref, kaug_ref, qaug_ref, vt_ref, gm_ref, gvl_ref,
                        kmean_s, wtril_s):
    i = pl.program_id(0)
    tm = x_ref.shape[0]
    nblk = kmean_s.shape[1]

    @pl.when(i == 0)
    def _init():
        row = _iota((GMLP_CHUNK, N_GROUPS * GMLP_CHUNK), 0)
        col = _iota((GMLP_CHUNK, N_GROUPS * GMLP_CHUNK), 1) & (GMLP_CHUNK - 1)
        wtril_s[...] = jnp.where(col <= row, wcat_ref[...], 0.0).astype(BF16)
        kmean_s[...] = jnp.zeros(kmean_s.shape, F32)

    xn = (_rms(x_ref[...]) * g1_ref[...]).astype(BF16)
    hn = _dot(xn, wnat_ref[...])
    ht = _dot_nt(wt_ref[...], xn)

    lane = _iota((tm, LANES), 1)
    rowf = _iota((tm, LANES), 0).astype(F32)
    blk_row = _iota((nblk, tm), 0)
    blk_rowf = blk_row.astype(F32)
    piece_row = _iota((LANES, tm), 0)
    i_f = i.astype(F32)
    k_hi = jnp.where(lane < 3, rowf, jnp.where(lane < 6, float(MOBA_BLOCK) * i_f, 0.0))
    ones_rows = jnp.where(_iota((_VT_ROWS - HEAD_DIM, tm), 0) == 0, 1.0, 0.0)

    rs_heads = []
    for h in range(N_HEADS):
        kp = hn[:, _KP0 + LANES * h:_KP0 + LANES * (h + 1)]
        rs = lax.rsqrt(jnp.sum(kp * kp, axis=-1, keepdims=True) * (1.0 / HEAD_DIM) + EPS)
        rs_heads.append(rs)
        kn = kp * rs * kgp_ref[...]
        kmean_s[h, pl.ds(i, 1), :] = jnp.mean(kn, axis=0, keepdims=True)

        qt = ht[HEAD_DIM * h:HEAD_DIM * (h + 1), :]
        qn = qt * lax.rsqrt(jnp.sum(qt * qt, axis=0, keepdims=True) * (1.0 / HEAD_DIM) + EPS) * qg_ref[...]
        gate = _dot3(kmean_s[h], jnp.concatenate([qn, jnp.zeros_like(qn)], axis=0))
        cur = jnp.where(blk_row < i, gate, NEG)
        sel = jnp.zeros_like(cur)
        for _ in range(MOBA_TOPK):
            m = jnp.max(cur, axis=0, keepdims=True)
            first = jnp.min(jnp.where(cur == m, blk_rowf, float(nblk)), axis=0, keepdims=True)
            hit = blk_rowf == first
            sel = jnp.where(hit, 1.0, sel)
            cur = jnp.where(hit, -jnp.inf, cur)
        keep = jnp.where(blk_row < i, sel, jnp.where(blk_row == i, 1.0, 0.0))
        sel_bias = jnp.where(keep > 0.0, 0.0, NEG)
        pieces = [qn * (QK_SCALE * LOG2E), sel_bias]
        if nblk < HEAD_DIM:
            pieces.append(jnp.zeros((HEAD_DIM - nblk, tm), F32))
        slope = 2.0 ** -(h + 1)
        slope_rows = jnp.zeros((LANES, tm), F32)
        for r, piece in enumerate(_LOG2E_PIECES * 2):
            slope_rows = jnp.where(piece_row == r, slope * piece, slope_rows)
        qaug_ref[h] = jnp.concatenate(pieces + [slope_rows], axis=0).astype(BF16)

        k_lo = jnp.where(lane == HEAD_DIM + i, 1.0, kn)
        kaug_ref[h] = jnp.concatenate([k_lo, k_hi], axis=1).astype(BF16)
        vt_ref[h] = jnp.concatenate([ht[ATTN_W + HEAD_DIM * h:ATTN_W + HEAD_DIM * (h + 1), :], ones_rows],
                                    axis=0).astype(BF16)

    cols = []
    for c in range(ATTN_W // LANES):
        sc = jnp.where(lane < HEAD_DIM, rs_heads[2 * c], rs_heads[2 * c + 1])
        cols.append(hn[:, _KN0 + LANES * c:_KN0 + LANES * (c + 1)] * sc)
    k_ref[...] = jnp.concatenate(cols, axis=1) * kgn_ref[...]
    v_ref[...] = hn[:, _V0:_V0 + ATTN_W]

    gu = _gelu(hn[:, _GU0:_GU0 + GMLP_W])
    gvn = _gmlp_norm(_gelu(hn[:, _GV0:_GV0 + GMLP_W]), bones_ref, gvg_ref)
    gvl_ref[...] = gvn[tm - GMLP_CHUNK:, :]
    group_of_lane = _iota((1, GMLP_W), 1) >> 6
    outs = []
    for c in range(tm // GMLP_CHUNK):
        gc = gvn[GMLP_CHUNK * c:GMLP_CHUNK * (c + 1), :]
        stacked = jnp.concatenate([jnp.where(group_of_lane == g, gc, 0.0) for g in range(N_GROUPS)], axis=0)
        mixed = _dot(wtril_s[...], stacked.astype(BF16)) + gbias_ref[...]
        outs.append(gu[GMLP_CHUNK * c:GMLP_CHUNK * (c + 1), :] * mixed)
    gm_ref[...] = (_rms(jnp.concatenate(outs, axis=0)) * gog_ref[...]).astype(BF16)


def _proj_prompt(x, w, tm=MOBA_BLOCK):
    t = x.shape[0]
    nblk = t // tm
    assert nblk <= HEAD_DIM
    row_tile = lambda width: pl.BlockSpec((tm, width), lambda i: (i, 0))
    in_specs = [row_tile(D_MODEL), _full((1, D_MODEL)), _full((D_MODEL, _NAT_COLS)), _full((D_MODEL, D_MODEL)),
                _full((HEAD_DIM, 1)), _full((1, LANES)), _full((1, ATTN_W)), _full((1, GMLP_W)),
                _full((GMLP_W, GMLP_W)), _full((GMLP_CHUNK, N_GROUPS * GMLP_CHUNK)), _full((GMLP_CHUNK, GMLP_W)),
                _full((1, GMLP_W))]
    out_shape = (jax.ShapeDtypeStruct((t, ATTN_W), F32), jax.ShapeDtypeStruct((t, ATTN_W), F32),
                 jax.ShapeDtypeStruct((N_HEADS, t, 2 * LANES), BF16),
                 jax.ShapeDtypeStruct((N_HEADS, 2 * LANES, t), BF16),
                 jax.ShapeDtypeStruct((N_HEADS, _VT_ROWS, t), BF16),
                 jax.ShapeDtypeStruct((t, GMLP_W), BF16), jax.ShapeDtypeStruct((GMLP_CHUNK, GMLP_W), F32))
    out_specs = (row_tile(ATTN_W), row_tile(ATTN_W),
                 pl.BlockSpec((N_HEADS, tm, 2 * LANES), lambda i: (0, i, 0)),
                 pl.BlockSpec((N_HEADS, 2 * LANES, tm), lambda i: (0, 0, i)),
                 pl.BlockSpec((N_HEADS, _VT_ROWS, tm), lambda i: (0, 0, i)),
                 row_tile(GMLP_W), _full((GMLP_CHUNK, GMLP_W)))
    return pl.pallas_call(
        _proj_prompt_kernel, grid=(nblk,), in_specs=in_specs, out_specs=out_specs, out_shape=out_shape,
        scratch_shapes=[pltpu.VMEM((N_HEADS, nblk, LANES), F32),
                        pltpu.VMEM((GMLP_CHUNK, N_GROUPS * GMLP_CHUNK), BF16)],
        compiler_params=_params("arbitrary"), name="proj_prompt",
    )(x, w["g1"], w["wnat"], w["wt"], w["qg_col"], w["kg_pad"], w["kg_nat"], w["gvg"], w["bones"],
      w["wcat"], w["gbias"], w["gog"])


_MOBA_HEADS_PER_STEP = 4


def _moba_kernel(kaug_ref, qaug_ref, vt_ref, out_ref, s_even, s_odd):
    j = pl.program_id(1)
    tq = qaug_ref.shape[2]
    heads = range(_MOBA_HEADS_PER_STEP)

    def block_start(n):
        return pl.multiple_of(n * MOBA_BLOCK, MOBA_BLOCK)

    def produce(buf, n):
        for hh in heads:
            buf[hh] = _dot(kaug_ref[hh, pl.ds(block_start(n), MOBA_BLOCK), :], qaug_ref[hh])

    def fold(state, s, hh, n):
        m, acc = state
        m_new = jnp.maximum(m, jnp.max(s, axis=0, keepdims=True))
        alpha = jnp.exp2(m - m_new)
        p = jnp.exp2(s - m_new).astype(BF16)
        acc = acc * alpha + _dot(vt_ref[hh, :, pl.ds(block_start(n), MOBA_BLOCK)], p)
        return m_new, acc

    def body(i, states):
        produce(s_odd, 2 * i + 1)
        states = tuple(fold(states[hh], s_even[hh], hh, 2 * i) for hh in heads)
        produce(s_even, 2 * i + 2)
        return tuple(fold(states[hh], s_odd[hh], hh, 2 * i + 1) for hh in heads)

    init = tuple((jnp.full((1, tq), -jnp.inf, F32), jnp.zeros((_VT_ROWS, tq), F32)) for _ in heads)
    produce(s_even, 0)
    states = lax.fori_loop(0, j // 2, body, init)

    causal = _iota((MOBA_BLOCK, tq), 0) <= _iota((MOBA_BLOCK, tq), 1)
    j_odd = (j & 1) == 1
    produce(s_odd, j)
    for hh in heads:
        first = jnp.where(j_odd, s_even[hh], jnp.where(causal, s_even[hh], NEG))
        state = fold(states[hh], first, hh, 2 * (j // 2))
        second = jnp.where(j_odd, jnp.where(causal, s_odd[hh], NEG), NEG)
        _, acc = fold(state, second, hh, j)
        out_ref[HEAD_DIM * hh:HEAD_DIM * (hh + 1), :] = acc[:HEAD_DIM, :] / acc[HEAD_DIM:HEAD_DIM + 1, :]


def _moba_prompt(kaug, qaug, vt):
    t = kaug.shape[1]
    tq = MOBA_BLOCK
    hs = _MOBA_HEADS_PER_STEP
    once = pl.Buffered(1)
    return pl.pallas_call(
        _moba_kernel, grid=(N_HEADS // hs, t // tq),
        in_specs=[pl.BlockSpec((hs, t, 2 * LANES), lambda g, j: (g, 0, 0), pipeline_mode=once),
                  pl.BlockSpec((hs, 2 * LANES, tq), lambda g, j: (g, 0, j)),
                  pl.BlockSpec((hs, _VT_ROWS, t), lambda g, j: (g, 0, 0), pipeline_mode=once)],
        out_specs=pl.BlockSpec((hs * HEAD_DIM, tq), lambda g, j: (g, j)),
        out_shape=jax.ShapeDtypeStruct((ATTN_W, t), F32),
        scratch_shapes=[pltpu.VMEM((hs, MOBA_BLOCK, tq), F32), pltpu.VMEM((hs, MOBA_BLOCK, tq), F32)],
        compiler_params=_params("parallel", "arbitrary"), name="moba_prompt",
    )(kaug, qaug, vt)


def _cand_index_table():
    rows = []
    for a, nb in _CAND_ROWS:
        for c0 in range(0, max(nb, 8), 8):
            rows.append([a * PEER_TOPK + c0 + b for b in range(8)])
    rows.append([(8 + a) * PEER_TOPK for a in range(8)])
    flat = np.asarray(rows, np.float32).reshape(-1, 1)
    return jnp.asarray(np.broadcast_to(flat, (flat.shape[0], LANES)).copy())


_N_CAND = 8 * (sum(max(nb, 8) // 8 for _, nb in _CAND_ROWS) + 1)


def _extract_top(cur, idx, n, on_hit):
    big = float(1 << 20)
    for r in range(n):
        m = jnp.max(cur, axis=0, keepdims=True)
        first = jnp.min(jnp.where(cur == m, idx, big), axis=0, keepdims=True)
        hit = idx == first
        cur = jnp.where(hit, -jnp.inf, cur)
        on_hit(r, m, hit)
    return cur


def _route_kernel(x_ref, attn_ref, gm_ref, aog_ref, woa_ref, wog_ref, g2_ref, wqt_ref, keys_ref, cidx_ref,
                  h_ref, xn2_ref, m1_ref, c1_ref, r2_ref, e2_ref,
                  s_s, rank_s, sv_s, na_s, chosen_s, *, attn_transposed):
    tt = x_ref.shape[0]
    at = attn_ref[...]
    if attn_transposed:
        at = at.T
    an = (_rms(at) * aog_ref[...]).astype(BF16)
    h = x_ref[...] + _dot(an, woa_ref[...]) + _dot(gm_ref[...], wog_ref[...])
    h_ref[...] = h
    xn2 = (_rms(h) * g2_ref[...]).astype(BF16)
    xn2_ref[...] = xn2
    qt = _dot_nt(wqt_ref[...], xn2)
    for hp in range(2 * PEER_HEADS):
        s_s[hp] = _dot3(keys_ref[hp], qt[PEER_HALF * hp:PEER_HALF * (hp + 1), :])

    key_idx = _iota((PEER_KEYS, LANES), 0).astype(F32)
    row8 = _iota((8, LANES), 0)

    def rank_distinct(hp, lanes):
        cur = s_s[hp, :, lanes]
        rank = jnp.full((PEER_KEYS, LANES), float(PEER_TOPK), F32)
        for r in range(PEER_TOPK):
            m = jnp.max(cur, axis=0, keepdims=True)
            hit = cur == m
            cur = jnp.where(hit, -jnp.inf, cur)
            rank = jnp.where(hit, float(r), rank)
            sv_s[hp, r:r + 1, lanes] = m
        rank_s[hp, :, lanes] = rank
        return jnp.sum(jnp.where(rank < float(PEER_TOPK), 1.0, 0.0), axis=0, keepdims=True)

    def rank_exact(hp, lanes):
        rank = [jnp.full((PEER_KEYS, LANES), float(PEER_TOPK), F32)]

        def on_hit(r, m, hit):
            rank[0] = jnp.where(hit, float(r), rank[0])
            sv_s[hp, r:r + 1, lanes] = m

        _extract_top(s_s[hp, :, lanes], key_idx, PEER_TOPK, on_hit)
        rank_s[hp, :, lanes] = rank[0]

    def per_head(hd, carry):
        ranked = []
        for sl in range(tt // LANES):
            lanes = slice(LANES * sl, LANES * (sl + 1))
            for hp in (2 * hd, 2 * hd + 1):
                ranked.append((hp, lanes, rank_distinct(hp, lanes)))
        off_count = functools.reduce(jnp.maximum, [jnp.abs(n - float(PEER_TOPK)) for _, _, n in ranked])

        @pl.when(jnp.max(off_count) > 0.0)
        def _redo_ranks():
            for hp, lanes, _ in ranked:
                rank_exact(hp, lanes)
        def candidates(lanes):
            sv1 = sv_s[2 * hd, :, lanes]
            sv2 = sv_s[2 * hd + 1, :, lanes]
            pieces = []
            for a, nb in _CAND_ROWS:
                for c0 in range(0, max(nb, 8), 8):
                    piece = sv1[a:a + 1, :] + sv2[c0:c0 + 8, :]
                    pieces.append(piece if nb >= 8 else jnp.where(row8 < nb, piece, -jnp.inf))
            pieces.append(sv1[8:16, :] + sv2[0:1, :])
            return sv1, sv2, jnp.concatenate(pieces, axis=0)

        n_chosen = []
        for sl in range(tt // LANES):
            _, _, cur = candidates(slice(LANES * sl, LANES * (sl + 1)))
            chosen = jnp.zeros_like(cur)
            for _ in range(PEER_TOPK):
                hit = cur == jnp.max(cur, axis=0, keepdims=True)
                cur = jnp.where(hit, -jnp.inf, cur)
                chosen = jnp.where(hit, 1.0, chosen)
            chosen_s[sl] = chosen
            n_chosen.append(jnp.sum(chosen, axis=0, keepdims=True))
        off_count = functools.reduce(jnp.maximum, [jnp.abs(n - float(PEER_TOPK)) for n in n_chosen])

        @pl.when(jnp.max(off_count) > 0.0)
        def _redo_sums():
            for sl in range(tt // LANES):
                _, _, cand = candidates(slice(LANES * sl, LANES * (sl + 1)))
                chosen = [jnp.zeros_like(cand)]

                def on_hit(r, m, hit):
                    chosen[0] = jnp.where(hit, 1.0, chosen[0])

                _extract_top(cand, cidx_ref[...], PEER_TOPK, on_hit)
                chosen_s[sl] = chosen[0]

        for sl in range(tt // LANES):
            lanes = slice(LANES * sl, LANES * (sl + 1))
            sv1, sv2, cand = candidates(lanes)
            chosen = chosen_s[sl]
            top = sv1[0:1, :] + sv2[0:1, :]
            z = jnp.sum(jnp.where(chosen > 0.0, jnp.exp(cand - top), 0.0), axis=0, keepdims=True)
            row = 0
            for a, nb in _CAND_ROWS:
                nrows = max(nb, 8)
                na_s[sl, a:a + 1, :] = jnp.sum(chosen[row:row + nrows, :], axis=0, keepdims=True)
                row += nrows
            na_s[sl, 8:16, :] = chosen[row:row + 8, :]
            na = na_s[sl]
            rank1 = rank_s[2 * hd, :, lanes]
            m1 = jnp.zeros((PEER_KEYS, LANES), F32)
            for a in range(PEER_TOPK):
                m1 = jnp.where(rank1 == float(a), na[a:a + 1, :], m1)
            m1_ref[hd, :, lanes] = m1
            c1_ref[hd, :, lanes] = jnp.exp(s_s[2 * hd, :, lanes] - sv1[0:1, :]) / z
            r2_ref[hd, :, lanes] = rank_s[2 * hd + 1, :, lanes].astype(BF16)
            e2_ref[hd, :, lanes] = jnp.exp(s_s[2 * hd + 1, :, lanes] - sv2[0:1, :]).astype(BF16)
        return carry

    lax.fori_loop(0, PEER_HEADS, per_head, 0)


def _route(x, attn, gm, w, *, tt, attn_transposed):
    t = x.shape[0]
    row_tile = lambda width: pl.BlockSpec((tt, width), lambda i: (i, 0))
    attn_spec = pl.BlockSpec((ATTN_W, tt), lambda i: (0, i)) if attn_transposed else row_tile(ATTN_W)
    head_tile = pl.BlockSpec((PEER_HEADS, PEER_KEYS, tt), lambda i: (0, 0, i))
    in_specs = [row_tile(D_MODEL), attn_spec, row_tile(GMLP_W), _full((1, ATTN_W)),
                _full((ATTN_W, D_MODEL)), _full((GMLP_W, D_MODEL)), _full((1, D_MODEL)),
                _full((2 * PEER_HEADS * PEER_HALF, D_MODEL)), _full((2 * PEER_HEADS, PEER_KEYS, PEER_HALF)),
                _full((_N_CAND, LANES))]
    stat = lambda dt: jax.ShapeDtypeStruct((PEER_HEADS, PEER_KEYS, t), dt)
    out_shape = (jax.ShapeDtypeStruct((t, D_MODEL), F32), jax.ShapeDtypeStruct((t, D_MODEL), BF16),
                 stat(F32), stat(F32), stat(BF16), stat(BF16))
    out_specs = (row_tile(D_MODEL), row_tile(D_MODEL), head_tile, head_tile, head_tile, head_tile)
    return pl.pallas_call(
        functools.partial(_route_kernel, attn_transposed=attn_transposed),
        grid=(t // tt,), in_specs=in_specs, out_specs=out_specs, out_shape=out_shape,
        scratch_shapes=[pltpu.VMEM((2 * PEER_HEADS, PEER_KEYS, tt), F32),
                        pltpu.VMEM((2 * PEER_HEADS, PEER_KEYS, tt), F32),
                        pltpu.VMEM((2 * PEER_HEADS, PEER_TOPK, tt), F32),
                        pltpu.VMEM((tt // LANES, PEER_TOPK, LANES), F32),
                        pltpu.VMEM((tt // LANES, _N_CAND, LANES), F32)],
        compiler_params=_params("parallel"), name="route",
    )(x, attn, gm, w["aog"], w["wo_attn"], w["wo_gmlp"], w["g2"], w["wqt"], w["keys"], w["cand_idx"])


def _peer_kernel(xn2_ref, u_ref, vt_ref, m1_ref, c1_ref, r2_ref, e2_ref, h_ref, y_ref, acc_s, act_even_s, act_odd_s,
                 gated_s, *, keys_per_step):
    c = pl.program_id(1)
    n_blocks = pl.num_programs(1) - 1

    @pl.when(c == 0)
    def _zero():
        acc_s[...] = jnp.zeros(acc_s.shape, F32)
        act_odd_s[...] = jnp.zeros(act_odd_s.shape, BF16)

    def step(act_write, act_read):
        act_write[...] = _gelu_sigmoid(_dot_nt(u_ref[...], xn2_ref[...])).astype(BF16)
        first_key = jnp.maximum(c - 1, 0) * keys_per_step
        for ii in range(keys_per_step):
            i = first_key + ii
            rows = slice(PEER_KEYS * ii, PEER_KEYS * (ii + 1))
            g = None
            for hd in range(PEER_HEADS):
                partners = m1_ref[hd, pl.ds(i, 1), :].astype(BF16)
                weight = c1_ref[hd, pl.ds(i, 1), :].astype(BF16)
                term = jnp.where(r2_ref[hd] < partners, e2_ref[hd], jnp.zeros((), BF16)) * weight
                g = term if g is None else g + term
            gated_s[rows, :] = g * act_read[rows, :]
        acc_s[...] += _dot(vt_ref[...], gated_s[...])

    @pl.when((c & 1) == 0)
    def _even():
        step(act_even_s, act_odd_s)

    @pl.when((c & 1) == 1)
    def _odd():
        step(act_odd_s, act_even_s)

    @pl.when(c == n_blocks)
    def _finish():
        y_ref[...] = h_ref[...] + acc_s[...].T


def _peer(xn2, h, m1, c1, r2, e2, w, *, tt, keys_per_step=8):
    t = xn2.shape[0]
    ne = keys_per_step * PEER_KEYS
    n_blocks = N_EXPERTS // ne
    head_tile = pl.BlockSpec((PEER_HEADS, PEER_KEYS, tt), lambda j, c: (0, 0, j))
    row_tile = pl.BlockSpec((tt, D_MODEL), lambda j, c: (j, 0))
    return pl.pallas_call(
        functools.partial(_peer_kernel, keys_per_step=keys_per_step),
        grid=(t // tt, n_blocks + 1),
        in_specs=[row_tile, pl.BlockSpec((ne, D_MODEL), lambda j, c: (jnp.minimum(c, n_blocks - 1), 0)),
                  pl.BlockSpec((D_MODEL, ne), lambda j, c: (0, jnp.maximum(c - 1, 0))),
                  head_tile, head_tile, head_tile, head_tile, row_tile],
        out_specs=row_tile, out_shape=jax.ShapeDtypeStruct((t, D_MODEL), F32),
        scratch_shapes=[pltpu.VMEM((D_MODEL, tt), F32)] + [pltpu.VMEM((ne, tt), BF16)] * 3,
        compiler_params=_params("parallel", "arbitrary"), name="peer",
    )(xn2, w["u"], w["vt"], m1, c1, r2, e2, h)


def _proj_sample_kernel(x_ref, g1_ref, win_ref, bones_ref, qgn_ref, kgn_ref, gvg_ref, wsc_ref, bsc_ref, gog_ref,
                        k_ref, v_ref, gv_ref, gm_ref, q_ref, qt_ref):
    xn = (_rms(x_ref[...]) * g1_ref[...]).astype(BF16)
    hn = _dot(xn, win_ref[...])

    def head_norm(z, g_ref):
        hi, lo = _split(z * z)
        ss = _dot(hi, bones_ref[...]) + _dot(lo, bones_ref[...])
        return z * lax.rsqrt(ss * (1.0 / HEAD_DIM) + EPS) * g_ref[...]

    qn = head_norm(hn[:, 0:ATTN_W], qgn_ref)
    kn = head_norm(hn[:, ATTN_W:2 * ATTN_W], kgn_ref)
    k_ref[...] = kn
    v_ref[...] = hn[:, 2 * ATTN_W:3 * ATTN_W]
    gu = _gelu(hn[:, 3 * ATTN_W:3 * ATTN_W + GMLP_W])
    gvn = _gmlp_norm(_gelu(hn[:, 3 * ATTN_W + GMLP_W:]), bones_ref, gvg_ref)
    gv_ref[...] = gvn
    gm_ref[...] = (_rms(gu * (wsc_ref[...] * gvn + bsc_ref[...])) * gog_ref[...]).astype(BF16)
    qs = qn * QK_SCALE
    q_ref[...] = qs
    qt_ref[...] = qs.T


def _proj_sample(x, w):
    nb = x.shape[0]
    out_shape = (jax.ShapeDtypeStruct((nb, ATTN_W), F32), jax.ShapeDtypeStruct((nb, ATTN_W), F32),
                 jax.ShapeDtypeStruct((nb, GMLP_W), F32), jax.ShapeDtypeStruct((nb, GMLP_W), BF16),
                 jax.ShapeDtypeStruct((nb, ATTN_W), F32), jax.ShapeDtypeStruct((ATTN_W, nb), F32))
    return pl.pallas_call(
        _proj_sample_kernel, out_shape=out_shape, compiler_params=_params(), name="proj_sample",
    )(x, w["g1"], w["win"], w["bones"], w["qg_nat"], w["kg_nat"], w["gvg"], w["ws_one"], w["b_one"], w["gog"])


_PAGES_PER_STEP = 32
_PAGES_PER_BLOCK = MOBA_BLOCK // PAGE
_PAGE_BLOCK = (None, N_HEADS, HEAD_DIM, PAGE)


def _sample_scores_kernel(pt_ref, qt_ref, q_ref, knew_ref, *refs, n_steps, past_len):
    k_refs = refs[:_PAGES_PER_STEP]
    p_ref, pself_ref, sel_ref, qcol_s, sc_s, gate_s = refs[_PAGES_PER_STEP:]
    d = pl.program_id(0)
    c = pl.program_id(1)
    n_seq = qt_ref.shape[1]
    lane = _iota((N_HEADS, LANES), 1)
    blocks_per_step = _PAGES_PER_STEP // _PAGES_PER_BLOCK

    @pl.when(c == 0)
    def _init():
        pick = jnp.where(_iota((n_seq, LANES), 0) == d, 1.0, 0.0)
        qcol_s[...] = _dot3(qt_ref[...], pick)
        gate_s[...] = jnp.zeros(gate_s.shape, F32)

    gates = gate_s[...]
    for b in range(blocks_per_step):
        blk = None
        for g in range(_PAGES_PER_BLOCK):
            r = b * _PAGES_PER_BLOCK + g
            rows = [jnp.sum(k_refs[r][hd] * qcol_s[HEAD_DIM * hd:HEAD_DIM * (hd + 1), :], axis=0, keepdims=True)
                    for hd in range(N_HEADS)]
            s_page = jnp.concatenate(rows, axis=0)
            sc_s[:, pl.ds(pl.multiple_of((c * _PAGES_PER_STEP + r) * PAGE, PAGE), PAGE)] = s_page
            blk = s_page if blk is None else blk + s_page
        gate = jnp.sum(blk, axis=-1, keepdims=True) * (1.0 / MOBA_BLOCK)
        gates = jnp.where(lane == c * blocks_per_step + b, gate, gates)
    gate_s[...] = gates

    @pl.when(c == n_steps - 1)
    def _select():
        n_blocks = n_steps * blocks_per_step
        lane_f = lane.astype(F32)
        cur = jnp.where(lane < n_blocks, gates, -jnp.inf)
        sel = jnp.zeros_like(cur)
        picked = jnp.zeros_like(cur)
        for r in range(MOBA_TOPK):
            m = jnp.max(cur, axis=-1, keepdims=True)
            first = jnp.min(jnp.where(cur == m, lane_f, float(LANES)), axis=-1, keepdims=True)
            hit = lane_f == first
            sel = jnp.where(hit, 1.0, sel)
            picked = jnp.where(lane == r, first, picked)
            cur = jnp.where(hit, -jnp.inf, cur)
        sel_ref[...] = picked

        n_pos = n_blocks * MOBA_BLOCK
        chosen = jnp.concatenate([jnp.broadcast_to(sel[:, n:n + 1], (N_HEADS, MOBA_BLOCK)) for n in range(n_blocks)],
                                 axis=1)
        slopes = jnp.exp2(-(_iota((N_HEADS, 1), 0) + 1).astype(F32))
        distance = float(past_len) - _iota((1, n_pos), 1).astype(F32)
        logit = jnp.where(chosen > 0.0, sc_s[...] - slopes * distance, NEG)
        self_logit = jnp.sum(q_ref[...] * knew_ref[...], axis=-1, keepdims=True)
        top = jnp.maximum(jnp.max(logit, axis=-1, keepdims=True), self_logit)
        p = jnp.exp(logit - top)
        p_self = jnp.exp(self_logit - top)
        inv = 1.0 / (jnp.sum(p, axis=-1, keepdims=True) + p_self)
        p_ref[...] = p * inv
        pself_ref[...] = jnp.broadcast_to(p_self * inv, pself_ref.shape)


def _sample_scores(page_table, cache_kt, qt, q, k_new, past_len):
    nb, n_pages = page_table.shape
    n_steps = n_pages // _PAGES_PER_STEP
    n_blocks = n_pages // _PAGES_PER_BLOCK
    assert n_blocks <= LANES
    page_spec = lambda r: pl.BlockSpec(
        _PAGE_BLOCK, lambda d, c, pt, r=r: (pt[d, c * _PAGES_PER_STEP + r], 0, 0, 0))
    per_seq = lambda shape: pl.BlockSpec((None,) + shape, lambda d, c, pt: (d,) + (0,) * len(shape))
    grid_spec = pltpu.PrefetchScalarGridSpec(
        num_scalar_prefetch=1, grid=(nb, n_steps),
        in_specs=[pl.BlockSpec((ATTN_W, nb), lambda d, c, pt: (0, 0)), per_seq((N_HEADS, HEAD_DIM)),
                  per_seq((N_HEADS, HEAD_DIM))] + [page_spec(r) for r in range(_PAGES_PER_STEP)],
        out_specs=(per_seq((N_HEADS, n_pages * PAGE)), per_seq((N_HEADS, HEAD_DIM)), per_seq((N_HEADS, LANES))),
        scratch_shapes=[pltpu.VMEM((ATTN_W, LANES), F32), pltpu.VMEM((N_HEADS, n_pages * PAGE), F32),
                        pltpu.VMEM((N_HEADS, LANES), F32)])
    out_shape = (jax.ShapeDtypeStruct((nb, N_HEADS, n_pages * PAGE), F32),
                 jax.ShapeDtypeStruct((nb, N_HEADS, HEAD_DIM), F32), jax.ShapeDtypeStruct((nb, N_HEADS, LANES), F32))
    return pl.pallas_call(
        functools.partial(_sample_scores_kernel, n_steps=n_steps, past_len=past_len),
        grid_spec=grid_spec, out_shape=out_shape,
        compiler_params=_params("parallel", "arbitrary"), name="sample_scores",
    )(page_table, qt, q, k_new, *([cache_kt] * _PAGES_PER_STEP))


_SLABS_PER_HEAD = MOBA_TOPK * _PAGES_PER_BLOCK


def _sample_values_kernel(phys_ref, logical_ref, p_ref, pself_ref, vnew_ref, *refs):
    n_slabs = N_HEADS * _SLABS_PER_HEAD
    v_refs = refs[:n_slabs]
    out_ref = refs[n_slabs]
    d = pl.program_id(0)
    cols = []
    for hd in range(N_HEADS):
        acc = None
        for k in range(_SLABS_PER_HEAD):
            i = hd * _SLABS_PER_HEAD + k
            start = pl.multiple_of(logical_ref[d, i] * PAGE, PAGE)
            term = v_refs[i][...] * p_ref[hd:hd + 1, pl.ds(start, PAGE)]
            acc = term if acc is None else acc + term
        cols.append(jnp.sum(acc, axis=-1, keepdims=True))
    ctx_t = jnp.concatenate(cols + [jnp.zeros((HEAD_DIM, LANES - N_HEADS), F32)], axis=1)
    out_ref[...] = ctx_t.T[0:N_HEADS, :] + pself_ref[...] * vnew_ref[...]


def _sample_values(slab_phys, slab_logical, cache_vt, p, pself, v_new):
    nb, n_slabs = slab_phys.shape
    assert n_slabs == N_HEADS * _SLABS_PER_HEAD
    slab_spec = lambda i: pl.BlockSpec(
        (None, None, HEAD_DIM, PAGE), lambda d, ph, lg, i=i: (ph[d, i], i // _SLABS_PER_HEAD, 0, 0))
    per_seq = lambda shape: pl.BlockSpec((None,) + shape, lambda d, ph, lg: (d,) + (0,) * len(shape))
    head_rows = per_seq((N_HEADS, HEAD_DIM))
    grid_spec = pltpu.PrefetchScalarGridSpec(
        num_scalar_prefetch=2, grid=(nb,),
        in_specs=[per_seq((N_HEADS, p.shape[2])), head_rows, head_rows] + [slab_spec(i) for i in range(n_slabs)],
        out_specs=head_rows)
    return pl.pallas_call(
        _sample_values_kernel, grid_spec=grid_spec, out_shape=jax.ShapeDtypeStruct((nb, N_HEADS, HEAD_DIM), F32),
        compiler_params=_params("parallel"), name="sample_values",
    )(slab_phys, slab_logical, p, pself, v_new, *([cache_vt] * n_slabs))


def _prepare(norm1_g, w_in, q_norm_g, k_norm_g, gmlp_v_norm_g, gmlp_ws, gmlp_b, attn_out_norm_g, gmlp_out_norm_g,
             w_out, norm2_g, peer_wq, peer_keys, peer_u, peer_v):
    wq, wk, wv, wgu, wgv = jnp.split(w_in, [ATTN_W, 2 * ATTN_W, 3 * ATTN_W, 3 * ATTN_W + GMLP_W], axis=1)
    wk_pad = jnp.pad(wk.reshape(D_MODEL, N_HEADS, HEAD_DIM), ((0, 0), (0, 0), (0, LANES - HEAD_DIM)))
    feat = np.arange(ATTN_W)
    w = {
        "g1": norm1_g.reshape(1, D_MODEL),
        "win": w_in.astype(BF16),
        "wnat": jnp.concatenate([wk_pad.reshape(D_MODEL, N_HEADS * LANES), wk, wv, wgu, wgv], axis=1).astype(BF16),
        "wt": jnp.concatenate([wq, wv], axis=1).T.astype(BF16),
        "qg_col": q_norm_g.reshape(HEAD_DIM, 1),
        "kg_pad": jnp.pad(k_norm_g, (0, LANES - HEAD_DIM)).reshape(1, LANES),
        "qg_nat": jnp.tile(q_norm_g, N_HEADS).reshape(1, ATTN_W),
        "kg_nat": jnp.tile(k_norm_g, N_HEADS).reshape(1, ATTN_W),
        "gvg": gmlp_v_norm_g.reshape(1, GMLP_W),
        "bones": jnp.asarray((feat[:, None] // GROUP_DIM) == (feat[None, :] // GROUP_DIM), BF16),
        "wcat": jnp.transpose(gmlp_ws, (1, 0, 2)).reshape(GMLP_CHUNK, N_GROUPS * GMLP_CHUNK),
        "gbias": jnp.repeat(gmlp_b.T, GROUP_DIM, axis=1),
        "ws_one": jnp.repeat(gmlp_ws[:, 0, 0], GROUP_DIM).reshape(1, GMLP_W),
        "b_one": jnp.repeat(gmlp_b[:, 0], GROUP_DIM).reshape(1, GMLP_W),
        "gog": gmlp_out_norm_g.reshape(1, GMLP_W),
        "aog": attn_out_norm_g.reshape(1, ATTN_W),
        "wo_attn": w_out[:ATTN_W].astype(BF16),
        "wo_gmlp": w_out[ATTN_W:].astype(BF16),
        "g2": norm2_g.reshape(1, D_MODEL),
        "wqt": peer_wq.T.astype(BF16),
        "keys": peer_keys.reshape(2 * PEER_HEADS, PEER_KEYS, PEER_HALF),
        "u": peer_u.astype(BF16),
        "vt": peer_v.T.astype(BF16),
        "cand_idx": _cand_index_table(),
    }
    return w


def _sample_attention(page_table, cache_k, cache_v, q, qt, k_new, v_new, past_len):
    nb = q.shape[0]
    heads = lambda a: a.reshape(nb, N_HEADS, HEAD_DIM)
    as_stored = lambda cache: jnp.transpose(cache, (0, 2, 3, 1))
    p, pself, picked = _sample_scores(page_table, as_stored(cache_k), qt, heads(q), heads(k_new), past_len)
    blocks = picked[:, :, :MOBA_TOPK].astype(jnp.int32)
    logical = (blocks[..., None] * _PAGES_PER_BLOCK + jnp.arange(_PAGES_PER_BLOCK, dtype=jnp.int32)).reshape(nb, -1)
    phys = jnp.take_along_axis(page_table, logical, axis=1)
    ctx = _sample_values(phys, logical.astype(jnp.int32), as_stored(cache_v), p, pself, heads(v_new))
    return ctx.reshape(nb, ATTN_W)


def _layer_tail(x, attn, gm, w, *, route_tile, peer_tile, attn_transposed):
    h, xn2, m1, c1, r2, e2 = _route(x, attn, gm, w, tt=route_tile, attn_transposed=attn_transposed)
    return _peer(xn2, h, m1, c1, r2, e2, w, tt=peer_tile)


def kernel(x_prompt, x_sample, cache_k, cache_v, page_table, norm1_g, w_in, q_norm_g, k_norm_g, gmlp_v_norm_g,
           gmlp_ws, gmlp_b, attn_out_norm_g, gmlp_out_norm_g, w_out, norm2_g, peer_wq, peer_keys, peer_u, peer_v):
    w = _prepare(norm1_g, w_in, q_norm_g, k_norm_g, gmlp_v_norm_g, gmlp_ws, gmlp_b, attn_out_norm_g,
                 gmlp_out_norm_g, w_out, norm2_g, peer_wq, peer_keys, peer_u, peer_v)
    b, t, _ = x_prompt.shape
    assert b == 1
    xp = x_prompt.reshape(t, D_MODEL)
    k_p, v_p, kaug, qaug, vt, gm_p, gv_last = _proj_prompt(xp, w)
    attn_t = _moba_prompt(kaug, qaug, vt)
    y_p = _layer_tail(xp, attn_t, gm_p, w, route_tile=512, peer_tile=512, attn_transposed=True)

    nb, ds, _ = x_sample.shape
    assert ds == 1
    xs = x_sample.reshape(nb, D_MODEL)
    past_len = page_table.shape[1] * PAGE
    assert past_len % MOBA_BLOCK == 0
    k_s, v_s, gv_s, gm_s, q_s, qt_s = _proj_sample(xs, w)
    attn_s = _sample_attention(page_table, cache_k, cache_v, q_s, qt_s, k_s, v_s, past_len)
    y_s = _layer_tail(xs, attn_s, gm_s, w, route_tile=nb, peer_tile=nb, attn_transposed=False)

    return (y_p.reshape(1, t, D_MODEL), y_s.reshape(nb, 1, D_MODEL),
            k_p.reshape(1, t, N_HEADS, HEAD_DIM), v_p.reshape(1, t, N_HEADS, HEAD_DIM),
            gv_last.reshape(1, GMLP_CHUNK, GMLP_W),
            k_s.reshape(nb, 1, N_HEADS, HEAD_DIM), v_s.reshape(nb, 1, N_HEADS, HEAD_DIM),
            gv_s.reshape(nb, 1, GMLP_W))
```

```python
import functools
import math

import jax
import jax.numpy as jnp
import numpy as np
from jax import lax
from jax.experimental import pallas as pl
from jax.experimental.pallas import tpu as pltpu

F32 = jnp.float32
BF16 = jnp.bfloat16

D_MODEL = 1024
N_HEADS = 8
HEAD_DIM = 64
ATTN_W = N_HEADS * HEAD_DIM
GMLP_W = 512
N_GROUPS = 8
GROUP_DIM = GMLP_W // N_GROUPS
GMLP_CHUNK = 128
MOBA_BLOCK = 256
MOBA_TOPK = 3
PAGE = 128
PEER_HEADS = 8
PEER_KEYS = 128
PEER_HALF = 128
PEER_TOPK = 16
N_EXPERTS = PEER_KEYS * PEER_KEYS
EPS = 1e-6
NEG = -1e30
QK_SCALE = HEAD_DIM ** -0.5
GELU_C = math.sqrt(2.0 / math.pi)


def _bf16_pieces(x, n):
    out, rest = [], np.float32(x)
    for _ in range(n):
        piece = np.float32(rest.astype(jnp.bfloat16))
        out.append(float(piece))
        rest = np.float32(rest - piece)
    return tuple(out)


LOG2E = float(np.float32(math.log2(math.e)))
_LOG2E_PIECES = _bf16_pieces(LOG2E, 3)
_GELU_A = _bf16_pieces(-2.0 * GELU_C * 0.044715, 2)
_GELU_B = _bf16_pieces(-2.0 * GELU_C, 2)

LANES = 128
VMEM_LIMIT = 56 * 1024 * 1024

_CAND_ROWS = tuple((a, PEER_TOPK // (a + 1)) for a in range(8))


def _dot(a, b):
    return jnp.dot(a, b, preferred_element_type=F32)


def _dot_nt(a, b):
    return lax.dot_general(a, b, (((1,), (1,)), ((), ())), preferred_element_type=F32)


def _split(a):
    hi = a.astype(BF16)
    lo = (a - hi.astype(F32)).astype(BF16)
    return hi, lo


def _dot3(a, b):
    ah, al = _split(a)
    bh, bl = _split(b)
    return _dot(ah, bh) + (_dot(ah, bl) + _dot(al, bh))


def _rms(x, axis=-1):
    return x * lax.rsqrt(jnp.mean(x * x, axis=axis, keepdims=True) + EPS)


def _gelu(x):
    return x * (0.5 * (1.0 + jnp.tanh(GELU_C * (x + 0.044715 * (x * x * x)))))


def _gelu_sigmoid(x):
    t = x * x
    if x.dtype == BF16:
        poly = (_GELU_A[0] * t + _GELU_B[0]) + (_GELU_A[1] * t + _GELU_B[1])
    else:
        poly = (-2.0 * GELU_C * 0.044715) * t - 2.0 * GELU_C
    return x / (1.0 + jnp.exp(x * poly))


def _iota(shape, dim, dtype=jnp.int32):
    return lax.broadcasted_iota(dtype, shape, dim)


def _params(*sem):
    return pltpu.CompilerParams(dimension_semantics=sem, vmem_limit_bytes=VMEM_LIMIT)


def _full(shape):
    nd = len(shape)
    return pl.BlockSpec(shape, lambda *_: (0,) * nd)


_KP0, _KN0, _V0, _GU0, _GV0, _NAT_COLS = 0, 1024, 1536, 2048, 2560, 3072
_VT_ROWS = HEAD_DIM + 16


def _gmlp_norm(gvr, bones_ref, gvg_ref):
    hi, lo = _split(gvr * gvr)
    ss = _dot(hi, bones_ref[...]) + _dot(lo, bones_ref[...])
    return gvr * lax.rsqrt(ss * (1.0 / GROUP_DIM) + EPS) * gvg_ref[...]


def _proj_prompt_kernel(x_ref, g1_ref, wnat_ref, wt_ref, qg_ref, kgp_ref, kgn_ref, gvg_ref, bones_ref,
                        wcat_ref, gbias_ref, gog_ref,
                        k_ref, v_ref, kaug_ref, qaug_ref, vt_ref, gm_ref, gvl_ref,
                        kmean_s, wtril_s):
    i = pl.program_id(0)
    tm = x_ref.shape[0]
    nblk = kmean_s.shape[1]

    @pl.when(i == 0)
    def _init():
        row = _iota((GMLP_CHUNK, N_GROUPS * GMLP_CHUNK), 0)
        col = _iota((GMLP_CHUNK, N_GROUPS * GMLP_CHUNK), 1) & (GMLP_CHUNK - 1)
        wtril_s[...] = jnp.where(col <= row, wcat_ref[...], 0.0).astype(BF16)
        kmean_s[...] = jnp.zeros(kmean_s.shape, F32)

    xn = (_rms(x_ref[...]) * g1_ref[...]).astype(BF16)
    hn = _dot(xn, wnat_ref[...])
    ht = _dot_nt(wt_ref[...], xn)

    lane = _iota((tm, LANES), 1)
    rowf = _iota((tm, LANES), 0).astype(F32)
    blk_row = _iota((nblk, tm), 0)
    blk_rowf = blk_row.astype(F32)
    piece_row = _iota((LANES, tm), 0)
    i_f = i.astype(F32)
    k_hi = jnp.where(lane < 3, rowf, jnp.where(lane < 6, float(MOBA_BLOCK) * i_f, 0.0))
    ones_rows = jnp.where(_iota((_VT_ROWS - HEAD_DIM, tm), 0) == 0, 1.0, 0.0)

    rs_heads = []
    for h in range(N_HEADS):
        kp = hn[:, _KP0 + LANES * h:_KP0 + LANES * (h + 1)]
        rs = lax.rsqrt(jnp.sum(kp * kp, axis=-1, keepdims=True) * (1.0 / HEAD_DIM) + EPS)
        rs_heads.append(rs)
        kn = kp * rs * kgp_ref[...]
        kmean_s[h, pl.ds(i, 1), :] = jnp.mean(kn, axis=0, keepdims=True)

        qt = ht[HEAD_DIM * h:HEAD_DIM * (h + 1), :]
        qn = qt * lax.rsqrt(jnp.sum(qt * qt, axis=0, keepdims=True) * (1.0 / HEAD_DIM) + EPS) * qg_ref[...]
        gate = _dot3(kmean_s[h], jnp.concatenate([qn, jnp.zeros_like(qn)], axis=0))
        cur = jnp.where(blk_row < i, gate, NEG)
        sel = jnp.zeros_like(cur)
        for _ in range(MOBA_TOPK):
            m = jnp.max(cur, axis=0, keepdims=True)
            first = jnp.min(jnp.where(cur == m, blk_rowf, float(nblk)), axis=0, keepdims=True)
            hit = blk_rowf == first
            sel = jnp.where(hit, 1.0, sel)
            cur = jnp.where(hit, -jnp.inf, cur)
        keep = jnp.where(blk_row < i, sel, jnp.where(blk_row == i, 1.0, 0.0))
        sel_bias = jnp.where(keep > 0.0, 0.0, NEG)
        pieces = [qn * (QK_SCALE * LOG2E), sel_bias]
        if nblk < HEAD_DIM:
            pieces.append(jnp.zeros((HEAD_DIM - nblk, tm), F32))
        slope = 2.0 ** -(h + 1)
        slope_rows = jnp.zeros((LANES, tm), F32)
        for r, piece in enumerate(_LOG2E_PIECES * 2):
            slope_rows = jnp.where(piece_row == r, slope * piece, slope_rows)
        qaug_ref[h] = jnp.concatenate(pieces + [slope_rows], axis=0).astype(BF16)

        k_lo = jnp.where(lane == HEAD_DIM + i, 1.0, kn)
        kaug_ref[h] = jnp.concatenate([k_lo, k_hi], axis=1).astype(BF16)
        vt_ref[h] = jnp.concatenate([ht[ATTN_W + HEAD_DIM * h:ATTN_W + HEAD_DIM * (h + 1), :], ones_rows],
                                    axis=0).astype(BF16)

    cols = []
    for c in range(ATTN_W // LANES):
        sc = jnp.where(lane < HEAD_DIM, rs_heads[2 * c], rs_heads[2 * c + 1])
        cols.append(hn[:, _KN0 + LANES * c:_KN0 + LANES * (c + 1)] * sc)
    k_ref[...] = jnp.concatenate(cols, axis=1) * kgn_ref[...]
    v_ref[...] = hn[:, _V0:_V0 + ATTN_W]

    gu = _gelu(hn[:, _GU0:_GU0 + GMLP_W])
    gvn = _gmlp_norm(_gelu(hn[:, _GV0:_GV0 + GMLP_W]), bones_ref, gvg_ref)
    gvl_ref[...] = gvn[tm - GMLP_CHUNK:, :]
    group_of_lane = _iota((1, GMLP_W), 1) >> 6
    outs = []
    for c in range(tm // GMLP_CHUNK):
        gc = gvn[GMLP_CHUNK * c:GMLP_CHUNK * (c + 1), :]
        stacked = jnp.concatenate([jnp.where(group_of_lane == g, gc, 0.0) for g in range(N_GROUPS)], axis=0)
        mixed = _dot(wtril_s[...], stacked.astype(BF16)) + gbias_ref[...]
        outs.append(gu[GMLP_CHUNK * c:GMLP_CHUNK * (c + 1), :] * mixed)
    gm_ref[...] = (_rms(jnp.concatenate(outs, axis=0)) * gog_ref[...]).astype(BF16)


def _proj_prompt(x, w, tm=MOBA_BLOCK):
    t = x.shape[0]
    nblk = t // tm
    assert nblk <= HEAD_DIM
    row_tile = lambda width: pl.BlockSpec((tm, width), lambda i: (i, 0))
    in_specs = [row_tile(D_MODEL), _full((1, D_MODEL)), _full((D_MODEL, _NAT_COLS)), _full((D_MODEL, D_MODEL)),
                _full((HEAD_DIM, 1)), _full((1, LANES)), _full((1, ATTN_W)), _full((1, GMLP_W)),
                _full((GMLP_W, GMLP_W)), _full((GMLP_CHUNK, N_GROUPS * GMLP_CHUNK)), _full((GMLP_CHUNK, GMLP_W)),
                _full((1, GMLP_W))]
    out_shape = (jax.ShapeDtypeStruct((t, ATTN_W), F32), jax.ShapeDtypeStruct((t, ATTN_W), F32),
                 jax.ShapeDtypeStruct((N_HEADS, t, 2 * LANES), BF16),
                 jax.ShapeDtypeStruct((N_HEADS, 2 * LANES, t), BF16),
                 jax.ShapeDtypeStruct((N_HEADS, _VT_ROWS, t), BF16),
                 jax.ShapeDtypeStruct((t, GMLP_W), BF16), jax.ShapeDtypeStruct((GMLP_CHUNK, GMLP_W), F32))
    out_specs = (row_tile(ATTN_W), row_tile(ATTN_W),
                 pl.BlockSpec((N_HEADS, tm, 2 * LANES), lambda i: (0, i, 0)),
                 pl.BlockSpec((N_HEADS, 2 * LANES, tm), lambda i: (0, 0, i)),
                 pl.BlockSpec((N_HEADS, _VT_ROWS, tm), lambda i: (0, 0, i)),
                 row_tile(GMLP_W), _full((GMLP_CHUNK, GMLP_W)))
    return pl.pallas_call(
        _proj_prompt_kernel, grid=(nblk,), in_specs=in_specs, out_specs=out_specs, out_shape=out_shape,
        scratch_shapes=[pltpu.VMEM((N_HEADS, nblk, LANES), F32),
                        pltpu.VMEM((GMLP_CHUNK, N_GROUPS * GMLP_CHUNK), BF16)],
        compiler_params=_params("arbitrary"), name="proj_prompt",
    )(x, w["g1"], w["wnat"], w["wt"], w["qg_col"], w["kg_pad"], w["kg_nat"], w["gvg"], w["bones"],
      w["wcat"], w["gbias"], w["gog"])


_MOBA_HEADS_PER_STEP = 4


def _moba_kernel(kaug_ref, qaug_ref, vt_ref, out_ref, s_even, s_odd):
    j = pl.program_id(1)
    tq = qaug_ref.shape[2]
    heads = range(_MOBA_HEADS_PER_STEP)

    def block_start(n):
        return pl.multiple_of(n * MOBA_BLOCK, MOBA_BLOCK)

    def produce(buf, n):
        for hh in heads:
            buf[hh] = _dot(kaug_ref[hh, pl.ds(block_start(n), MOBA_BLOCK), :], qaug_ref[hh])

    def fold(state, s, hh, n):
        m, acc = state
        m_new = jnp.maximum(m, jnp.max(s, axis=0, keepdims=True))
        alpha = jnp.exp2(m - m_new)
        p = jnp.exp2(s - m_new).astype(BF16)
        acc = acc * alpha + _dot(vt_ref[hh, :, pl.ds(block_start(n), MOBA_BLOCK)], p)
        return m_new, acc

    def body(i, states):
        produce(s_odd, 2 * i + 1)
        states = tuple(fold(states[hh], s_even[hh], hh, 2 * i) for hh in heads)
        produce(s_even, 2 * i + 2)
        return tuple(fold(states[hh], s_odd[hh], hh, 2 * i + 1) for hh in heads)

    init = tuple((jnp.full((1, tq), -jnp.inf, F32), jnp.zeros((_VT_ROWS, tq), F32)) for _ in heads)
    produce(s_even, 0)
    states = lax.fori_loop(0, j // 2, body, init)

    causal = _iota((MOBA_BLOCK, tq), 0) <= _iota((MOBA_BLOCK, tq), 1)
    j_odd = (j & 1) == 1
    produce(s_odd, j)
    for hh in heads:
        first = jnp.where(j_odd, s_even[hh], jnp.where(causal, s_even[hh], NEG))
        state = fold(states[hh], first, hh, 2 * (j // 2))
        second = jnp.where(j_odd, jnp.where(causal, s_odd[hh], NEG), NEG)
        _, acc = fold(state, second, hh, j)
        out_ref[HEAD_DIM * hh:HEAD_DIM * (hh + 1), :] = acc[:HEAD_DIM, :] / acc[HEAD_DIM:HEAD_DIM + 1, :]


def _moba_prompt(kaug, qaug, vt):
    t = kaug.shape[1]
    tq = MOBA_BLOCK
    hs = _MOBA_HEADS_PER_STEP
    once = pl.Buffered(1)
    return pl.pallas_call(
        _moba_kernel, grid=(N_HEADS // hs, t // tq),
        in_specs=[pl.BlockSpec((hs, t, 2 * LANES), lambda g, j: (g, 0, 0), pipeline_mode=once),
                  pl.BlockSpec((hs, 2 * LANES, tq), lambda g, j: (g, 0, j)),
                  pl.BlockSpec((hs, _VT_ROWS, t), lambda g, j: (g, 0, 0), pipeline_mode=once)],
        out_specs=pl.BlockSpec((hs * HEAD_DIM, tq), lambda g, j: (g, j)),
        out_shape=jax.ShapeDtypeStruct((ATTN_W, t), F32),
        scratch_shapes=[pltpu.VMEM((hs, MOBA_BLOCK, tq), F32), pltpu.VMEM((hs, MOBA_BLOCK, tq), F32)],
        compiler_params=_params("parallel", "arbitrary"), name="moba_prompt",
    )(kaug, qaug, vt)


def _cand_index_table():
    rows = []
    for a, nb in _CAND_ROWS:
        for c0 in range(0, max(nb, 8), 8):
            rows.append([a * PEER_TOPK + c0 + b for b in range(8)])
    rows.append([(8 + a) * PEER_TOPK for a in range(8)])
    flat = np.asarray(rows, np.float32).reshape(-1, 1)
    return jnp.asarray(np.broadcast_to(flat, (flat.shape[0], LANES)).copy())


_N_CAND = 8 * (sum(max(nb, 8) // 8 for _, nb in _CAND_ROWS) + 1)


def _extract_top(cur, idx, n, on_hit):
    big = float(1 << 20)
    for r in range(n):
        m = jnp.max(cur, axis=0, keepdims=True)
        first = jnp.min(jnp.where(cur == m, idx, big), axis=0, keepdims=True)
        hit = idx == first
        cur = jnp.where(hit, -jnp.inf, cur)
        on_hit(r, m, hit)
    return cur


def _route_kernel(x_ref, attn_ref, gm_ref, aog_ref, woa_ref, wog_ref, g2_ref, wqt_ref, keys_ref, cidx_ref,
                  h_ref, xn2_ref, m1_ref, c1_ref, r2_ref, e2_ref,
                  s_s, rank_s, sv_s, na_s, chosen_s, *, attn_transposed):
    tt = x_ref.shape[0]
    at = attn_ref[...]
    if attn_transposed:
        at = at.T
    an = (_rms(at) * aog_ref[...]).astype(BF16)
    h = x_ref[...] + _dot(an, woa_ref[...]) + _dot(gm_ref[...], wog_ref[...])
    h_ref[...] = h
    xn2 = (_rms(h) * g2_ref[...]).astype(BF16)
    xn2_ref[...] = xn2
    qt = _dot_nt(wqt_ref[...], xn2)
    for hp in range(2 * PEER_HEADS):
        s_s[hp] = _dot3(keys_ref[hp], qt[PEER_HALF * hp:PEER_HALF * (hp + 1), :])

    key_idx = _iota((PEER_KEYS, LANES), 0).astype(F32)
    row8 = _iota((8, LANES), 0)

    def rank_distinct(hp, lanes):
        cur = s_s[hp, :, lanes]
        rank = jnp.full((PEER_KEYS, LANES), float(PEER_TOPK), F32)
        for r in range(PEER_TOPK):
            m = jnp.max(cur, axis=0, keepdims=True)
            hit = cur == m
            cur = jnp.where(hit, -jnp.inf, cur)
            rank = jnp.where(hit, float(r), rank)
            sv_s[hp, r:r + 1, lanes] = m
        rank_s[hp, :, lanes] = rank
        return jnp.sum(jnp.where(rank < float(PEER_TOPK), 1.0, 0.0), axis=0, keepdims=True)

    def rank_exact(hp, lanes):
        rank = [jnp.full((PEER_KEYS, LANES), float(PEER_TOPK), F32)]

        def on_hit(r, m, hit):
            rank[0] = jnp.where(hit, float(r), rank[0])
            sv_s[hp, r:r + 1, lanes] = m

        _extract_top(s_s[hp, :, lanes], key_idx, PEER_TOPK, on_hit)
        rank_s[hp, :, lanes] = rank[0]

    def per_head(hd, carry):
        ranked = []
        for sl in range(tt // LANES):
            lanes = slice(LANES * sl, LANES * (sl + 1))
            for hp in (2 * hd, 2 * hd + 1):
                ranked.append((hp, lanes, rank_distinct(hp, lanes)))
        off_count = functools.reduce(jnp.maximum, [jnp.abs(n - float(PEER_TOPK)) for _, _, n in ranked])

        @pl.when(jnp.max(off_count) > 0.0)
        def _redo_ranks():
            for hp, lanes, _ in ranked:
                rank_exact(hp, lanes)
        def candidates(lanes):
            sv1 = sv_s[2 * hd, :, lanes]
            sv2 = sv_s[2 * hd + 1, :, lanes]
            pieces = []
            for a, nb in _CAND_ROWS:
                for c0 in range(0, max(nb, 8), 8):
                    piece = sv1[a:a + 1, :] + sv2[c0:c0 + 8, :]
                    pieces.append(piece if nb >= 8 else jnp.where(row8 < nb, piece, -jnp.inf))
            pieces.append(sv1[8:16, :] + sv2[0:1, :])
            return sv1, sv2, jnp.concatenate(pieces, axis=0)

        n_chosen = []
        for sl in range(tt // LANES):
            _, _, cur = candidates(slice(LANES * sl, LANES * (sl + 1)))
            chosen = jnp.zeros_like(cur)
            for _ in range(PEER_TOPK):
                hit = cur == jnp.max(cur, axis=0, keepdims=True)
                cur = jnp.where(hit, -jnp.inf, cur)
                chosen = jnp.where(hit, 1.0, chosen)
            chosen_s[sl] = chosen
            n_chosen.append(jnp.sum(chosen, axis=0, keepdims=True))
        off_count = functools.reduce(jnp.maximum, [jnp.abs(n - float(PEER_TOPK)) for n in n_chosen])

        @pl.when(jnp.max(off_count) > 0.0)
        def _redo_sums():
            for sl in range(tt // LANES):
                _, _, cand = candidates(slice(LANES * sl, LANES * (sl + 1)))
                chosen = [jnp.zeros_like(cand)]

                def on_hit(r, m, hit):
                    chosen[0] = jnp.where(hit, 1.0, chosen[0])

                _extract_top(cand, cidx_ref[...], PEER_TOPK, on_hit)
                chosen_s[sl] = chosen[0]

        for sl in range(tt // LANES):
            lanes = slice(LANES * sl, LANES * (sl + 1))
            sv1, sv2, cand = candidates(lanes)
            chosen = chosen_s[sl]
            top = sv1[0:1, :] + sv2[0:1, :]
            z = jnp.sum(jnp.where(chosen > 0.0, jnp.exp(cand - top), 0.0), axis=0, keepdims=True)
            row = 0
            for a, nb in _CAND_ROWS:
                nrows = max(nb, 8)
                na_s[sl, a:a + 1, :] = jnp.sum(chosen[row:row + nrows, :], axis=0, keepdims=True)
                row += nrows
            na_s[sl, 8:16, :] = chosen[row:row + 8, :]
            na = na_s[sl]
            rank1 = rank_s[2 * hd, :, lanes]
            m1 = jnp.zeros((PEER_KEYS, LANES), F32)
            for a in range(PEER_TOPK):
                m1 = jnp.where(rank1 == float(a), na[a:a + 1, :], m1)
            m1_ref[hd, :, lanes] = m1
            c1_ref[hd, :, lanes] = jnp.exp(s_s[2 * hd, :, lanes] - sv1[0:1, :]) / z
            r2_ref[hd, :, lanes] = rank_s[2 * hd + 1, :, lanes].astype(BF16)
            e2_ref[hd, :, lanes] = jnp.exp(s_s[2 * hd + 1, :, lanes] - sv2[0:1, :]).astype(BF16)
        return carry

    lax.fori_loop(0, PEER_HEADS, per_head, 0)


def _route(x, attn, gm, w, *, tt, attn_transposed):
    t = x.shape[0]
    row_tile = lambda width: pl.BlockSpec((tt, width), lambda i: (i, 0))
    attn_spec = pl.BlockSpec((ATTN_W, tt), lambda i: (0, i)) if attn_transposed else row_tile(ATTN_W)
    head_tile = pl.BlockSpec((PEER_HEADS, PEER_KEYS, tt), lambda i: (0, 0, i))
    in_specs = [row_tile(D_MODEL), attn_spec, row_tile(GMLP_W), _full((1, ATTN_W)),
                _full((ATTN_W, D_MODEL)), _full((GMLP_W, D_MODEL)), _full((1, D_MODEL)),
                _full((2 * PEER_HEADS * PEER_HALF, D_MODEL)), _full((2 * PEER_HEADS, PEER_KEYS, PEER_HALF)),
                _full((_N_CAND, LANES))]
    stat = lambda dt: jax.ShapeDtypeStruct((PEER_HEADS, PEER_KEYS, t), dt)
    out_shape = (jax.ShapeDtypeStruct((t, D_MODEL), F32), jax.ShapeDtypeStruct((t, D_MODEL), BF16),
                 stat(F32), stat(F32), stat(BF16), stat(BF16))
    out_specs = (row_tile(D_MODEL), row_tile(D_MODEL), head_tile, head_tile, head_tile, head_tile)
    return pl.pallas_call(
        functools.partial(_route_kernel, attn_transposed=attn_transposed),
        grid=(t // tt,), in_specs=in_specs, out_specs=out_specs, out_shape=out_shape,
        scratch_shapes=[pltpu.VMEM((2 * PEER_HEADS, PEER_KEYS, tt), F32),
                        pltpu.VMEM((2 * PEER_HEADS, PEER_KEYS, tt), F32),
                        pltpu.VMEM((2 * PEER_HEADS, PEER_TOPK, tt), F32),
                        pltpu.VMEM((tt // LANES, PEER_TOPK, LANES), F32),
                        pltpu.VMEM((tt // LANES, _N_CAND, LANES), F32)],
        compiler_params=_params("parallel"), name="route",
    )(x, attn, gm, w["aog"], w["wo_attn"], w["wo_gmlp"], w["g2"], w["wqt"], w["keys"], w["cand_idx"])


def _peer_kernel(xn2_ref, u_ref, vt_ref, m1_ref, c1_ref, r2_ref, e2_ref, h_ref, y_ref, acc_s, act_even_s, act_odd_s,
                 gated_s, *, keys_per_step):
    c = pl.program_id(1)
    n_blocks = pl.num_programs(1) - 1

    @pl.when(c == 0)
    def _zero():
        acc_s[...] = jnp.zeros(acc_s.shape, F32)
        act_odd_s[...] = jnp.zeros(act_odd_s.shape, BF16)

    def step(act_write, act_read):
        act_write[...] = _gelu_sigmoid(_dot_nt(u_ref[...], xn2_ref[...]).astype(BF16))
        first_key = jnp.maximum(c - 1, 0) * keys_per_step
        for ii in range(keys_per_step):
            i = first_key + ii
            rows = slice(PEER_KEYS * ii, PEER_KEYS * (ii + 1))
            g = None
            for hd in range(PEER_HEADS):
                partners = m1_ref[hd, pl.ds(i, 1), :].astype(BF16)
                weight = c1_ref[hd, pl.ds(i, 1), :].astype(BF16)
                term = jnp.where(r2_ref[hd] < partners, e2_ref[hd], jnp.zeros((), BF16)) * weight
                g = term if g is None else g + term
            gated_s[rows, :] = g * act_read[rows, :]
        acc_s[...] += _dot(vt_ref[...], gated_s[...])

    @pl.when((c & 1) == 0)
    def _even():
        step(act_even_s, act_odd_s)

    @pl.when((c & 1) == 1)
    def _odd():
        step(act_odd_s, act_even_s)

    @pl.when(c == n_blocks)
    def _finish():
        y_ref[...] = h_ref[...] + acc_s[...].T


def _peer(xn2, h, m1, c1, r2, e2, w, *, tt, keys_per_step=8):
    t = xn2.shape[0]
    ne = keys_per_step * PEER_KEYS
    n_blocks = N_EXPERTS // ne
    head_tile = pl.BlockSpec((PEER_HEADS, PEER_KEYS, tt), lambda j, c: (0, 0, j))
    row_tile = pl.BlockSpec((tt, D_MODEL), lambda j, c: (j, 0))
    return pl.pallas_call(
        functools.partial(_peer_kernel, keys_per_step=keys_per_step),
        grid=(t // tt, n_blocks + 1),
        in_specs=[row_tile, pl.BlockSpec((ne, D_MODEL), lambda j, c: (jnp.minimum(c, n_blocks - 1), 0)),
                  pl.BlockSpec((D_MODEL, ne), lambda j, c: (0, jnp.maximum(c - 1, 0))),
                  head_tile, head_tile, head_tile, head_tile, row_tile],
        out_specs=row_tile, out_shape=jax.ShapeDtypeStruct((t, D_MODEL), F32),
        scratch_shapes=[pltpu.VMEM((D_MODEL, tt), F32)] + [pltpu.VMEM((ne, tt), BF16)] * 3,
        compiler_params=_params("parallel", "arbitrary"), name="peer",
    )(xn2, w["u"], w["vt"], m1, c1, r2, e2, h)


def _proj_sample_kernel(x_ref, g1_ref, win_ref, bones_ref, qgn_ref, kgn_ref, gvg_ref, wsc_ref, bsc_ref, gog_ref,
                        k_ref, v_ref, gv_ref, gm_ref, q_ref, qt_ref):
    xn = (_rms(x_ref[...]) * g1_ref[...]).astype(BF16)
    hn = _dot(xn, win_ref[...])

    def head_norm(z, g_ref):
        hi, lo = _split(z * z)
        ss = _dot(hi, bones_ref[...]) + _dot(lo, bones_ref[...])
        return z * lax.rsqrt(ss * (1.0 / HEAD_DIM) + EPS) * g_ref[...]

    qn = head_norm(hn[:, 0:ATTN_W], qgn_ref)
    kn = head_norm(hn[:, ATTN_W:2 * ATTN_W], kgn_ref)
    k_ref[...] = kn
    v_ref[...] = hn[:, 2 * ATTN_W:3 * ATTN_W]
    gu = _gelu(hn[:, 3 * ATTN_W:3 * ATTN_W + GMLP_W])
    gvn = _gmlp_norm(_gelu(hn[:, 3 * ATTN_W + GMLP_W:]), bones_ref, gvg_ref)
    gv_ref[...] = gvn
    gm_ref[...] = (_rms(gu * (wsc_ref[...] * gvn + bsc_ref[...])) * gog_ref[...]).astype(BF16)
    qs = qn * QK_SCALE
    q_ref[...] = qs
    qt_ref[...] = qs.T


def _proj_sample(x, w):
    nb = x.shape[0]
    out_shape = (jax.ShapeDtypeStruct((nb, ATTN_W), F32), jax.ShapeDtypeStruct((nb, ATTN_W), F32),
                 jax.ShapeDtypeStruct((nb, GMLP_W), F32), jax.ShapeDtypeStruct((nb, GMLP_W), BF16),
                 jax.ShapeDtypeStruct((nb, ATTN_W), F32), jax.ShapeDtypeStruct((ATTN_W, nb), F32))
    return pl.pallas_call(
        _proj_sample_kernel, out_shape=out_shape, compiler_params=_params(), name="proj_sample",
    )(x, w["g1"], w["win"], w["bones"], w["qg_nat"], w["kg_nat"], w["gvg"], w["ws_one"], w["b_one"], w["gog"])


_PAGES_PER_STEP = 32
_PAGES_PER_BLOCK = MOBA_BLOCK // PAGE
_PAGE_BLOCK = (None, N_HEADS, HEAD_DIM, PAGE)


def _sample_scores_kernel(pt_ref, qt_ref, q_ref, knew_ref, *refs, n_steps, past_len):
    k_refs = refs[:_PAGES_PER_STEP]
    p_ref, pself_ref, sel_ref, qcol_s, sc_s, gate_s = refs[_PAGES_PER_STEP:]
    d = pl.program_id(0)
    c = pl.program_id(1)
    n_seq = qt_ref.shape[1]
    lane = _iota((N_HEADS, LANES), 1)
    blocks_per_step = _PAGES_PER_STEP // _PAGES_PER_BLOCK

    @pl.when(c == 0)
    def _init():
        pick = jnp.where(_iota((n_seq, LANES), 0) == d, 1.0, 0.0)
        qcol_s[...] = _dot3(qt_ref[...], pick)
        gate_s[...] = jnp.zeros(gate_s.shape, F32)

    gates = gate_s[...]
    for b in range(blocks_per_step):
        blk = None
        for g in range(_PAGES_PER_BLOCK):
            r = b * _PAGES_PER_BLOCK + g
            rows = [jnp.sum(k_refs[r][hd] * qcol_s[HEAD_DIM * hd:HEAD_DIM * (hd + 1), :], axis=0, keepdims=True)
                    for hd in range(N_HEADS)]
            s_page = jnp.concatenate(rows, axis=0)
            sc_s[:, pl.ds(pl.multiple_of((c * _PAGES_PER_STEP + r) * PAGE, PAGE), PAGE)] = s_page
            blk = s_page if blk is None else blk + s_page
        gate = jnp.sum(blk, axis=-1, keepdims=True) * (1.0 / MOBA_BLOCK)
        gates = jnp.where(lane == c * blocks_per_step + b, gate, gates)
    gate_s[...] = gates

    @pl.when(c == n_steps - 1)
    def _select():
        n_blocks = n_steps * blocks_per_step
        lane_f = lane.astype(F32)
        cur = jnp.where(lane < n_blocks, gates, -jnp.inf)
        sel = jnp.zeros_like(cur)
        picked = jnp.zeros_like(cur)
        for r in range(MOBA_TOPK):
            m = jnp.max(cur, axis=-1, keepdims=True)
            first = jnp.min(jnp.where(cur == m, lane_f, float(LANES)), axis=-1, keepdims=True)
            hit = lane_f == first
            sel = jnp.where(hit, 1.0, sel)
            picked = jnp.where(lane == r, first, picked)
            cur = jnp.where(hit, -jnp.inf, cur)
        sel_ref[...] = picked

        n_pos = n_blocks * MOBA_BLOCK
        chosen = jnp.concatenate([jnp.broadcast_to(sel[:, n:n + 1], (N_HEADS, MOBA_BLOCK)) for n in range(n_blocks)],
                                 axis=1)
        slopes = jnp.exp2(-(_iota((N_HEADS, 1), 0) + 1).astype(F32))
        distance = float(past_len) - _iota((1, n_pos), 1).astype(F32)
        logit = jnp.where(chosen > 0.0, sc_s[...] - slopes * distance, NEG)
        self_logit = jnp.sum(q_ref[...] * knew_ref[...], axis=-1, keepdims=True)
        top = jnp.maximum(jnp.max(logit, axis=-1, keepdims=True), self_logit)
        p = jnp.exp(logit - top)
        p_self = jnp.exp(self_logit - top)
        inv = 1.0 / (jnp.sum(p, axis=-1, keepdims=True) + p_self)
        p_ref[...] = p * inv
        pself_ref[...] = jnp.broadcast_to(p_self * inv, pself_ref.shape)


def _sample_scores(page_table, cache_kt, qt, q, k_new, past_len):
    nb, n_pages = page_table.shape
    n_steps = n_pages // _PAGES_PER_STEP
    n_blocks = n_pages // _PAGES_PER_BLOCK
    assert n_blocks <= LANES
    page_spec = lambda r: pl.BlockSpec(
        _PAGE_BLOCK, lambda d, c, pt, r=r: (pt[d, c * _PAGES_PER_STEP + r], 0, 0, 0))
    per_seq = lambda shape: pl.BlockSpec((None,) + shape, lambda d, c, pt: (d,) + (0,) * len(shape))
    grid_spec = pltpu.PrefetchScalarGridSpec(
        num_scalar_prefetch=1, grid=(nb, n_steps),
        in_specs=[pl.BlockSpec((ATTN_W, nb), lambda d, c, pt: (0, 0)), per_seq((N_HEADS, HEAD_DIM)),
                  per_seq((N_HEADS, HEAD_DIM))] + [page_spec(r) for r in range(_PAGES_PER_STEP)],
        out_specs=(per_seq((N_HEADS, n_pages * PAGE)), per_seq((N_HEADS, HEAD_DIM)), per_seq((N_HEADS, LANES))),
        scratch_shapes=[pltpu.VMEM((ATTN_W, LANES), F32), pltpu.VMEM((N_HEADS, n_pages * PAGE), F32),
                        pltpu.VMEM((N_HEADS, LANES), F32)])
    out_shape = (jax.ShapeDtypeStruct((nb, N_HEADS, n_pages * PAGE), F32),
                 jax.ShapeDtypeStruct((nb, N_HEADS, HEAD_DIM), F32), jax.ShapeDtypeStruct((nb, N_HEADS, LANES), F32))
    return pl.pallas_call(
        functools.partial(_sample_scores_kernel, n_steps=n_steps, past_len=past_len),
        grid_spec=grid_spec, out_shape=out_shape,
        compiler_params=_params("parallel", "arbitrary"), name="sample_scores",
    )(page_table, qt, q, k_new, *([cache_kt] * _PAGES_PER_STEP))


_SLABS_PER_HEAD = MOBA_TOPK * _PAGES_PER_BLOCK


def _sample_values_kernel(phys_ref, logical_ref, p_ref, pself_ref, vnew_ref, *refs):
    n_slabs = N_HEADS * _SLABS_PER_HEAD
    v_refs = refs[:n_slabs]
    out_ref = refs[n_slabs]
    d = pl.program_id(0)
    cols = []
    for hd in range(N_HEADS):
        acc = None
        for k in range(_SLABS_PER_HEAD):
            i = hd * _SLABS_PER_HEAD + k
            start = pl.multiple_of(logical_ref[d, i] * PAGE, PAGE)
            term = v_refs[i][...] * p_ref[hd:hd + 1, pl.ds(start, PAGE)]
            acc = term if acc is None else acc + term
        cols.append(jnp.sum(acc, axis=-1, keepdims=True))
    ctx_t = jnp.concatenate(cols + [jnp.zeros((HEAD_DIM, LANES - N_HEADS), F32)], axis=1)
    out_ref[...] = ctx_t.T[0:N_HEADS, :] + pself_ref[...] * vnew_ref[...]


def _sample_values(slab_phys, slab_logical, cache_vt, p, pself, v_new):
    nb, n_slabs = slab_phys.shape
    assert n_slabs == N_HEADS * _SLABS_PER_HEAD
    slab_spec = lambda i: pl.BlockSpec(
        (None, None, HEAD_DIM, PAGE), lambda d, ph, lg, i=i: (ph[d, i], i // _SLABS_PER_HEAD, 0, 0))
    per_seq = lambda shape: pl.BlockSpec((None,) + shape, lambda d, ph, lg: (d,) + (0,) * len(shape))
    head_rows = per_seq((N_HEADS, HEAD_DIM))
    grid_spec = pltpu.PrefetchScalarGridSpec(
        num_scalar_prefetch=2, grid=(nb,),
        in_specs=[per_seq((N_HEADS, p.shape[2])), head_rows, head_rows] + [slab_spec(i) for i in range(n_slabs)],
        out_specs=head_rows)
    return pl.pallas_call(
        _sample_values_kernel, grid_spec=grid_spec, out_shape=jax.ShapeDtypeStruct((nb, N_HEADS, HEAD_DIM), F32),
        compiler_params=_params("parallel"), name="sample_values",
    )(slab_phys, slab_logical, p, pself, v_new, *([cache_vt] * n_slabs))


def _prepare(norm1_g, w_in, q_norm_g, k_norm_g, gmlp_v_norm_g, gmlp_ws, gmlp_b, attn_out_norm_g, gmlp_out_norm_g,
             w_out, norm2_g, peer_wq, peer_keys, peer_u, peer_v):
    wq, wk, wv, wgu, wgv = jnp.split(w_in, [ATTN_W, 2 * ATTN_W, 3 * ATTN_W, 3 * ATTN_W + GMLP_W], axis=1)
    wk_pad = jnp.pad(wk.reshape(D_MODEL, N_HEADS, HEAD_DIM), ((0, 0), (0, 0), (0, LANES - HEAD_DIM)))
    feat = np.arange(ATTN_W)
    w = {
        "g1": norm1_g.reshape(1, D_MODEL),
        "win": w_in.astype(BF16),
        "wnat": jnp.concatenate([wk_pad.reshape(D_MODEL, N_HEADS * LANES), wk, wv, wgu, wgv], axis=1).astype(BF16),
        "wt": jnp.concatenate([wq, wv], axis=1).T.astype(BF16),
        "qg_col": q_norm_g.reshape(HEAD_DIM, 1),
        "kg_pad": jnp.pad(k_norm_g, (0, LANES - HEAD_DIM)).reshape(1, LANES),
        "qg_nat": jnp.tile(q_norm_g, N_HEADS).reshape(1, ATTN_W),
        "kg_nat": jnp.tile(k_norm_g, N_HEADS).reshape(1, ATTN_W),
        "gvg": gmlp_v_norm_g.reshape(1, GMLP_W),
        "bones": jnp.asarray((feat[:, None] // GROUP_DIM) == (feat[None, :] // GROUP_DIM), BF16),
        "wcat": jnp.transpose(gmlp_ws, (1, 0, 2)).reshape(GMLP_CHUNK, N_GROUPS * GMLP_CHUNK),
        "gbias": jnp.repeat(gmlp_b.T, GROUP_DIM, axis=1),
        "ws_one": jnp.repeat(gmlp_ws[:, 0, 0], GROUP_DIM).reshape(1, GMLP_W),
        "b_one": jnp.repeat(gmlp_b[:, 0], GROUP_DIM).reshape(1, GMLP_W),
        "gog": gmlp_out_norm_g.reshape(1, GMLP_W),
        "aog": attn_out_norm_g.reshape(1, ATTN_W),
        "wo_attn": w_out[:ATTN_W].astype(BF16),
        "wo_gmlp": w_out[ATTN_W:].astype(BF16),
        "g2": norm2_g.reshape(1, D_MODEL),
        "wqt": peer_wq.T.astype(BF16),
        "keys": peer_keys.reshape(2 * PEER_HEADS, PEER_KEYS, PEER_HALF),
        "u": peer_u.astype(BF16),
        "vt": peer_v.T.astype(BF16),
        "cand_idx": _cand_index_table(),
    }
    return w


def _sample_attention(page_table, cache_k, cache_v, q, qt, k_new, v_new, past_len):
    nb = q.shape[0]
    heads = lambda a: a.reshape(nb, N_HEADS, HEAD_DIM)
    as_stored = lambda cache: jnp.transpose(cache, (0, 2, 3, 1))
    p, pself, picked = _sample_scores(page_table, as_stored(cache_k), qt, heads(q), heads(k_new), past_len)
    blocks = picked[:, :, :MOBA_TOPK].astype(jnp.int32)
    logical = (blocks[..., None] * _PAGES_PER_BLOCK + jnp.arange(_PAGES_PER_BLOCK, dtype=jnp.int32)).reshape(nb, -1)
    phys = jnp.take_along_axis(page_table, logical, axis=1)
    ctx = _sample_values(phys, logical.astype(jnp.int32), as_stored(cache_v), p, pself, heads(v_new))
    return ctx.reshape(nb, ATTN_W)


def _layer_tail(x, attn, gm, w, *, route_tile, peer_tile, attn_transposed):
    h, xn2, m1, c1, r2, e2 = _route(x, attn, gm, w, tt=route_tile, attn_transposed=attn_transposed)
    return _peer(xn2, h, m1, c1, r2, e2, w, tt=peer_tile)


def kernel(x_prompt, x_sample, cache_k, cache_v, page_table, norm1_g, w_in, q_norm_g, k_norm_g, gmlp_v_norm_g,
           gmlp_ws, gmlp_b, attn_out_norm_g, gmlp_out_norm_g, w_out, norm2_g, peer_wq, peer_keys, peer_u, peer_v):
    w = _prepare(norm1_g, w_in, q_norm_g, k_norm_g, gmlp_v_norm_g, gmlp_ws, gmlp_b, attn_out_norm_g,
                 gmlp_out_norm_g, w_out, norm2_g, peer_wq, peer_keys, peer_u, peer_v)
    b, t, _ = x_prompt.shape
    assert b == 1
    xp = x_prompt.reshape(t, D_MODEL)
    k_p, v_p, kaug, qaug, vt, gm_p, gv_last = _proj_prompt(xp, w)
    attn_t = _moba_prompt(kaug, qaug, vt)
    y_p = _layer_tail(xp, attn_t, gm_p, w, route_tile=512, peer_tile=256, attn_transposed=True)

    nb, ds, _ = x_sample.shape
    assert ds == 1
    xs = x_sample.reshape(nb, D_MODEL)
    past_len = page_table.shape[1] * PAGE
    assert past_len % MOBA_BLOCK == 0
    k_s, v_s, gv_s, gm_s, q_s, qt_s = _proj_sample(xs, w)
    attn_s = _sample_attention(page_table, cache_k, cache_v, q_s, qt_s, k_s, v_s, past_len)
    y_s = _layer_tail(xs, attn_s, gm_s, w, route_tile=nb, peer_tile=nb, attn_transposed=False)

    return (y_p.reshape(1, t, D_MODEL), y_s.reshape(nb, 1, D_MODEL),
            k_p.reshape(1, t, N_HEADS, HEAD_DIM), v_p.reshape(1, t, N_HEADS, HEAD_DIM),
            gv_last.reshape(1, GMLP_CHUNK, GMLP_W),
            k_s.reshape(nb, 1, N_HEADS, HEAD_DIM), v_s.reshape(nb, 1, N_HEADS, HEAD_DIM),
            gv_s.reshape(nb, 1, GMLP_W))
```

```python
import functools
import math

import jax
import jax.numpy as jnp
import numpy as np
from jax import lax
from jax.experimental import pallas as pl
from jax.experimental.pallas import tpu as pltpu

F32 = jnp.float32
BF16 = jnp.bfloat16

D_MODEL = 1024
N_HEADS = 8
HEAD_DIM = 64
ATTN_W = N_HEADS * HEAD_DIM
GMLP_W = 512
N_GROUPS = 8
GROUP_DIM = GMLP_W // N_GROUPS
GMLP_CHUNK = 128
MOBA_BLOCK = 256
MOBA_TOPK = 3
PAGE = 128
PEER_HEADS = 8
PEER_KEYS = 128
PEER_HALF = 128
PEER_TOPK = 16
N_EXPERTS = PEER_KEYS * PEER_KEYS
EPS = 1e-6
NEG = -1e30
QK_SCALE = HEAD_DIM ** -0.5
GELU_C = math.sqrt(2.0 / math.pi)


def _bf16_pieces(x, n):
    out, rest = [], np.float32(x)
    for _ in range(n):
        piece = np.float32(rest.astype(jnp.bfloat16))
        out.append(float(piece))
        rest = np.float32(rest - piece)
    return tuple(out)


LOG2E = float(np.float32(math.log2(math.e)))
_LOG2E_PIECES = _bf16_pieces(LOG2E, 3)
_GELU_A = _bf16_pieces(-2.0 * GELU_C * 0.044715, 2)
_GELU_B = _bf16_pieces(-2.0 * GELU_C, 2)

LANES = 128
VMEM_LIMIT = 56 * 1024 * 1024

_CAND_ROWS = tuple((a, PEER_TOPK // (a + 1)) for a in range(8))


def _dot(a, b):
    return jnp.dot(a, b, preferred_element_type=F32)


def _dot_nt(a, b):
    return lax.dot_general(a, b, (((1,), (1,)), ((), ())), preferred_element_type=F32)


def _split(a):
    hi = a.astype(BF16)
    lo = (a - hi.astype(F32)).astype(BF16)
    return hi, lo


def _dot3(a, b):
    ah, al = _split(a)
    bh, bl = _split(b)
    return _dot(ah, bh) + (_dot(ah, bl) + _dot(al, bh))


def _rms(x, axis=-1):
    return x * lax.rsqrt(jnp.mean(x * x, axis=axis, keepdims=True) + EPS)


def _gelu(x):
    return x * (0.5 * (1.0 + jnp.tanh(GELU_C * (x + 0.044715 * (x * x * x)))))


def _gelu_sigmoid(x):
    t = x * x
    if x.dtype == BF16:
        poly = (_GELU_A[0] * t + _GELU_B[0]) + (_GELU_A[1] * t + _GELU_B[1])
    else:
        poly = (-2.0 * GELU_C * 0.044715) * t - 2.0 * GELU_C
    return x / (1.0 + jnp.exp(x * poly))


def _iota(shape, dim, dtype=jnp.int32):
    return lax.broadcasted_iota(dtype, shape, dim)


def _params(*sem):
    return pltpu.CompilerParams(dimension_semantics=sem, vmem_limit_bytes=VMEM_LIMIT)


def _full(shape):
    nd = len(shape)
    return pl.BlockSpec(shape, lambda *_: (0,) * nd)


_KP0, _KN0, _V0, _GU0, _GV0, _NAT_COLS = 0, 1024, 1536, 2048, 2560, 3072
_VT_ROWS = HEAD_DIM + 16


def _gmlp_norm(gvr, bones_ref, gvg_ref):
    hi, lo = _split(gvr * gvr)
    ss = _dot(hi, bones_ref[...]) + _dot(lo, bones_ref[...])
    return gvr * lax.rsqrt(ss * (1.0 / GROUP_DIM) + EPS) * gvg_ref[...]


def _proj_prompt_kernel(x_ref, g1_ref, wnat_ref, wt_ref, qg_ref, kgp_ref, kgn_ref, gvg_ref, bones_ref,
                        wcat_ref, gbias_ref, gog_ref,
                        k_ref, v_ref, kaug_ref, qaug_ref, vt_ref, gm_ref, gvl_ref,
                        kmean_s, wtril_s):
    i = pl.program_id(0)
    tm = x_ref.shape[0]
    nblk = kmean_s.shape[1]

    @pl.when(i == 0)
    def _init():
        row = _iota((GMLP_CHUNK, N_GROUPS * GMLP_CHUNK), 0)
        col = _iota((GMLP_CHUNK, N_GROUPS * GMLP_CHUNK), 1) & (GMLP_CHUNK - 1)
        wtril_s[...] = jnp.where(col <= row, wcat_ref[...], 0.0).astype(BF16)
        kmean_s[...] = jnp.zeros(kmean_s.shape, F32)

    xn = (_rms(x_ref[...]) * g1_ref[...]).astype(BF16)
    hn = _dot(xn, wnat_ref[...])
    ht = _dot_nt(wt_ref[...], xn)

    lane = _iota((tm, LANES), 1)
    rowf = _iota((tm, LANES), 0).astype(F32)
    blk_row = _iota((nblk, tm), 0)
    blk_rowf = blk_row.astype(F32)
    piece_row = _iota((LANES, tm), 0)
    i_f = i.astype(F32)
    k_hi = jnp.where(lane < 3, rowf, jnp.where(lane < 6, float(MOBA_BLOCK) * i_f, 0.0))
    ones_rows = jnp.where(_iota((_VT_ROWS - HEAD_DIM, tm), 0) == 0, 1.0, 0.0)

    rs_heads = []
    for h in range(N_HEADS):
        kp = hn[:, _KP0 + LANES * h:_KP0 + LANES * (h + 1)]
        rs = lax.rsqrt(jnp.sum(kp * kp, axis=-1, keepdims=True) * (1.0 / HEAD_DIM) + EPS)
        rs_heads.append(rs)
        kn = kp * rs * kgp_ref[...]
        kmean_s[h, pl.ds(i, 1), :] = jnp.mean(kn, axis=0, keepdims=True)

        qt = ht[HEAD_DIM * h:HEAD_DIM * (h + 1), :]
        qn = qt * lax.rsqrt(jnp.sum(qt * qt, axis=0, keepdims=True) * (1.0 / HEAD_DIM) + EPS) * qg_ref[...]
        gate = _dot3(kmean_s[h], jnp.concatenate([qn, jnp.zeros_like(qn)], axis=0))
        cur = jnp.where(blk_row < i, gate, NEG)
        sel = jnp.zeros_like(cur)
        for _ in range(MOBA_TOPK):
            m = jnp.max(cur, axis=0, keepdims=True)
            first = jnp.min(jnp.where(cur == m, blk_rowf, float(nblk)), axis=0, keepdims=True)
            hit = blk_rowf == first
            sel = jnp.where(hit, 1.0, sel)
            cur = jnp.where(hit, -jnp.inf, cur)
        keep = jnp.where(blk_row < i, sel, jnp.where(blk_row == i, 1.0, 0.0))
        sel_bias = jnp.where(keep > 0.0, 0.0, NEG)
        pieces = [qn * (QK_SCALE * LOG2E), sel_bias]
        if nblk < HEAD_DIM:
            pieces.append(jnp.zeros((HEAD_DIM - nblk, tm), F32))
        slope = 2.0 ** -(h + 1)
        slope_rows = jnp.zeros((LANES, tm), F32)
        for r, piece in enumerate(_LOG2E_PIECES * 2):
            slope_rows = jnp.where(piece_row == r, slope * piece, slope_rows)
        qaug_ref[h] = jnp.concatenate(pieces + [slope_rows], axis=0).astype(BF16)

        k_lo = jnp.where(lane == HEAD_DIM + i, 1.0, kn)
        kaug_ref[h] = jnp.concatenate([k_lo, k_hi], axis=1).astype(BF16)
        vt_ref[h] = jnp.concatenate([ht[ATTN_W + HEAD_DIM * h:ATTN_W + HEAD_DIM * (h + 1), :], ones_rows],
                                    axis=0).astype(BF16)

    cols = []
    for c in range(ATTN_W // LANES):
        sc = jnp.where(lane < HEAD_DIM, rs_heads[2 * c], rs_heads[2 * c + 1])
        cols.append(hn[:, _KN0 + LANES * c:_KN0 + LANES * (c + 1)] * sc)
    k_ref[...] = jnp.concatenate(cols, axis=1) * kgn_ref[...]
    v_ref[...] = hn[:, _V0:_V0 + ATTN_W]

    gu = _gelu(hn[:, _GU0:_GU0 + GMLP_W])
    gvn = _gmlp_norm(_gelu(hn[:, _GV0:_GV0 + GMLP_W]), bones_ref, gvg_ref)
    gvl_ref[...] = gvn[tm - GMLP_CHUNK:, :]
    group_of_lane = _iota((1, GMLP_W), 1) >> 6
    outs = []
    for c in range(tm // GMLP_CHUNK):
        gc = gvn[GMLP_CHUNK * c:GMLP_CHUNK * (c + 1), :]
        stacked = jnp.concatenate([jnp.where(group_of_lane == g, gc, 0.0) for g in range(N_GROUPS)], axis=0)
        mixed = _dot(wtril_s[...], stacked.astype(BF16)) + gbias_ref[...]
        outs.append(gu[GMLP_CHUNK * c:GMLP_CHUNK * (c + 1), :] * mixed)
    gm_ref[...] = (_rms(jnp.concatenate(outs, axis=0)) * gog_ref[...]).astype(BF16)


def _proj_prompt(x, w, tm=MOBA_BLOCK):
    t = x.shape[0]
    nblk = t // tm
    assert nblk <= HEAD_DIM
    row_tile = lambda width: pl.BlockSpec((tm, width), lambda i: (i, 0))
    in_specs = [row_tile(D_MODEL), _full((1, D_MODEL)), _full((D_MODEL, _NAT_COLS)), _full((D_MODEL, D_MODEL)),
                _full((HEAD_DIM, 1)), _full((1, LANES)), _full((1, ATTN_W)), _full((1, GMLP_W)),
                _full((GMLP_W, GMLP_W)), _full((GMLP_CHUNK, N_GROUPS * GMLP_CHUNK)), _full((GMLP_CHUNK, GMLP_W)),
                _full((1, GMLP_W))]
    out_shape = (jax.ShapeDtypeStruct((t, ATTN_W), F32), jax.ShapeDtypeStruct((t, ATTN_W), F32),
                 jax.ShapeDtypeStruct((N_HEADS, t, 2 * LANES), BF16),
                 jax.ShapeDtypeStruct((N_HEADS, 2 * LANES, t), BF16),
                 jax.ShapeDtypeStruct((N_HEADS, _VT_ROWS, t), BF16),
                 jax.ShapeDtypeStruct((t, GMLP_W), BF16), jax.ShapeDtypeStruct((GMLP_CHUNK, GMLP_W), F32))
    out_specs = (row_tile(ATTN_W), row_tile(ATTN_W),
                 pl.BlockSpec((N_HEADS, tm, 2 * LANES), lambda i: (0, i, 0)),
                 pl.BlockSpec((N_HEADS, 2 * LANES, tm), lambda i: (0, 0, i)),
                 pl.BlockSpec((N_HEADS, _VT_ROWS, tm), lambda i: (0, 0, i)),
                 row_tile(GMLP_W), _full((GMLP_CHUNK, GMLP_W)))
    return pl.pallas_call(
        _proj_prompt_kernel, grid=(nblk,), in_specs=in_specs, out_specs=out_specs, out_shape=out_shape,
        scratch_shapes=[pltpu.VMEM((N_HEADS, nblk, LANES), F32),
                        pltpu.VMEM((GMLP_CHUNK, N_GROUPS * GMLP_CHUNK), BF16)],
        compiler_params=_params("arbitrary"), name="proj_prompt",
    )(x, w["g1"], w["wnat"], w["wt"], w["qg_col"], w["kg_pad"], w["kg_nat"], w["gvg"], w["bones"],
      w["wcat"], w["gbias"], w["gog"])


_MOBA_HEADS_PER_STEP = 4


def _moba_kernel(kaug_ref, qaug_ref, vt_ref, out_ref, s_even, s_odd):
    j = pl.program_id(1)
    tq = qaug_ref.shape[2]
    heads = range(_MOBA_HEADS_PER_STEP)

    def block_start(n):
        return pl.multiple_of(n * MOBA_BLOCK, MOBA_BLOCK)

    def produce(buf, n):
        for hh in heads:
            buf[hh] = _dot(kaug_ref[hh, pl.ds(block_start(n), MOBA_BLOCK), :], qaug_ref[hh])

    def fold(state, s, hh, n):
        m, acc = state
        m_new = jnp.maximum(m, jnp.max(s, axis=0, keepdims=True))
        alpha = jnp.exp2(m - m_new)
        p = jnp.exp2(s - m_new).astype(BF16)
        acc = acc * alpha + _dot(vt_ref[hh, :, pl.ds(block_start(n), MOBA_BLOCK)], p)
        return m_new, acc

    def body(i, states):
        produce(s_odd, 2 * i + 1)
        states = tuple(fold(states[hh], s_even[hh], hh, 2 * i) for hh in heads)
        produce(s_even, 2 * i + 2)
        return tuple(fold(states[hh], s_odd[hh], hh, 2 * i + 1) for hh in heads)

    init = tuple((jnp.full((1, tq), -jnp.inf, F32), jnp.zeros((_VT_ROWS, tq), F32)) for _ in heads)
    produce(s_even, 0)
    states = lax.fori_loop(0, j // 2, body, init)

    causal = _iota((MOBA_BLOCK, tq), 0) <= _iota((MOBA_BLOCK, tq), 1)
    j_odd = (j & 1) == 1
    produce(s_odd, j)
    for hh in heads:
        first = jnp.where(j_odd, s_even[hh], jnp.where(causal, s_even[hh], NEG))
        state = fold(states[hh], first, hh, 2 * (j // 2))
        second = jnp.where(j_odd, jnp.where(causal, s_odd[hh], NEG), NEG)
        _, acc = fold(state, second, hh, j)
        out_ref[HEAD_DIM * hh:HEAD_DIM * (hh + 1), :] = acc[:HEAD_DIM, :] / acc[HEAD_DIM:HEAD_DIM + 1, :]


def _moba_prompt(kaug, qaug, vt):
    t = kaug.shape[1]
    tq = MOBA_BLOCK
    hs = _MOBA_HEADS_PER_STEP
    once = pl.Buffered(1)
    return pl.pallas_call(
        _moba_kernel, grid=(N_HEADS // hs, t // tq),
        in_specs=[pl.BlockSpec((hs, t, 2 * LANES), lambda g, j: (g, 0, 0), pipeline_mode=once),
                  pl.BlockSpec((hs, 2 * LANES, tq), lambda g, j: (g, 0, j)),
                  pl.BlockSpec((hs, _VT_ROWS, t), lambda g, j: (g, 0, 0), pipeline_mode=once)],
        out_specs=pl.BlockSpec((hs * HEAD_DIM, tq), lambda g, j: (g, j)),
        out_shape=jax.ShapeDtypeStruct((ATTN_W, t), F32),
        scratch_shapes=[pltpu.VMEM((hs, MOBA_BLOCK, tq), F32), pltpu.VMEM((hs, MOBA_BLOCK, tq), F32)],
        compiler_params=_params("parallel", "arbitrary"), name="moba_prompt",
    )(kaug, qaug, vt)


def _cand_index_table():
    rows = []
    for a, nb in _CAND_ROWS:
        for c0 in range(0, max(nb, 8), 8):
            rows.append([a * PEER_TOPK + c0 + b for b in range(8)])
    rows.append([(8 + a) * PEER_TOPK for a in range(8)])
    flat = np.asarray(rows, np.float32).reshape(-1, 1)
    return jnp.asarray(np.broadcast_to(flat, (flat.shape[0], LANES)).copy())


_N_CAND = 8 * (sum(max(nb, 8) // 8 for _, nb in _CAND_ROWS) + 1)


def _extract_top(cur, idx, n, on_hit):
    big = float(1 << 20)
    for r in range(n):
        m = jnp.max(cur, axis=0, keepdims=True)
        first = jnp.min(jnp.where(cur == m, idx, big), axis=0, keepdims=True)
        hit = idx == first
        cur = jnp.where(hit, -jnp.inf, cur)
        on_hit(r, m, hit)
    return cur


def _route_kernel(x_ref, attn_ref, gm_ref, aog_ref, woa_ref, wog_ref, g2_ref, wqt_ref, keys_ref, cidx_ref,
                  h_ref, xn2_ref, m1_ref, c1_ref, r2_ref, e2_ref,
                  s_s, rank_s, sv_s, na_s, chosen_s, *, attn_transposed):
    tt = x_ref.shape[0]
    at = attn_ref[...]
    if attn_transposed:
        at = at.T
    an = (_rms(at) * aog_ref[...]).astype(BF16)
    h = x_ref[...] + _dot(an, woa_ref[...]) + _dot(gm_ref[...], wog_ref[...])
    h_ref[...] = h
    xn2 = (_rms(h) * g2_ref[...]).astype(BF16)
    xn2_ref[...] = xn2
    qt = _dot_nt(wqt_ref[...], xn2)
    for hp in range(2 * PEER_HEADS):
        s_s[hp] = _dot3(keys_ref[hp], qt[PEER_HALF * hp:PEER_HALF * (hp + 1), :])

    key_idx = _iota((PEER_KEYS, LANES), 0).astype(F32)
    row8 = _iota((8, LANES), 0)

    def rank_distinct(hp, lanes):
        cur = s_s[hp, :, lanes]
        rank = jnp.full((PEER_KEYS, LANES), float(PEER_TOPK), F32)
        for r in range(PEER_TOPK):
            m = jnp.max(cur, axis=0, keepdims=True)
            hit = cur == m
            cur = jnp.where(hit, -jnp.inf, cur)
            rank = jnp.where(hit, float(r), rank)
            sv_s[hp, r:r + 1, lanes] = m
        rank_s[hp, :, lanes] = rank
        return jnp.sum(jnp.where(rank < float(PEER_TOPK), 1.0, 0.0), axis=0, keepdims=True)

    def rank_exact(hp, lanes):
        rank = [jnp.full((PEER_KEYS, LANES), float(PEER_TOPK), F32)]

        def on_hit(r, m, hit):
            rank[0] = jnp.where(hit, float(r), rank[0])
            sv_s[hp, r:r + 1, lanes] = m

        _extract_top(s_s[hp, :, lanes], key_idx, PEER_TOPK, on_hit)
        rank_s[hp, :, lanes] = rank[0]

    def per_head(hd, carry):
        ranked = []
        for sl in range(tt // LANES):
            lanes = slice(LANES * sl, LANES * (sl + 1))
            for hp in (2 * hd, 2 * hd + 1):
                ranked.append((hp, lanes, rank_distinct(hp, lanes)))
        off_count = functools.reduce(jnp.maximum, [jnp.abs(n - float(PEER_TOPK)) for _, _, n in ranked])

        @pl.when(jnp.max(off_count) > 0.0)
        def _redo_ranks():
            for hp, lanes, _ in ranked:
                rank_exact(hp, lanes)
        def candidates(lanes):
            sv1 = sv_s[2 * hd, :, lanes]
            sv2 = sv_s[2 * hd + 1, :, lanes]
            pieces = []
            for a, nb in _CAND_ROWS:
                for c0 in range(0, max(nb, 8), 8):
                    piece = sv1[a:a + 1, :] + sv2[c0:c0 + 8, :]
                    pieces.append(piece if nb >= 8 else jnp.where(row8 < nb, piece, -jnp.inf))
            pieces.append(sv1[8:16, :] + sv2[0:1, :])
            return sv1, sv2, jnp.concatenate(pieces, axis=0)

        n_chosen = []
        for sl in range(tt // LANES):
            _, _, cur = candidates(slice(LANES * sl, LANES * (sl + 1)))
            chosen = jnp.zeros_like(cur)
            for _ in range(PEER_TOPK):
                hit = cur == jnp.max(cur, axis=0, keepdims=True)
                cur = jnp.where(hit, -jnp.inf, cur)
                chosen = jnp.where(hit, 1.0, chosen)
            chosen_s[sl] = chosen
            n_chosen.append(jnp.sum(chosen, axis=0, keepdims=True))
        off_count = functools.reduce(jnp.maximum, [jnp.abs(n - float(PEER_TOPK)) for n in n_chosen])

        @pl.when(jnp.max(off_count) > 0.0)
        def _redo_sums():
            for sl in range(tt // LANES):
                _, _, cand = candidates(slice(LANES * sl, LANES * (sl + 1)))
                chosen = [jnp.zeros_like(cand)]

                def on_hit(r, m, hit):
                    chosen[0] = jnp.where(hit, 1.0, chosen[0])

                _extract_top(cand, cidx_ref[...], PEER_TOPK, on_hit)
                chosen_s[sl] = chosen[0]

        for sl in range(tt // LANES):
            lanes = slice(LANES * sl, LANES * (sl + 1))
            sv1, sv2, cand = candidates(lanes)
            chosen = chosen_s[sl]
            top = sv1[0:1, :] + sv2[0:1, :]
            z = jnp.sum(jnp.where(chosen > 0.0, jnp.exp(cand - top), 0.0), axis=0, keepdims=True)
            row = 0
            for a, nb in _CAND_ROWS:
                nrows = max(nb, 8)
                na_s[sl, a:a + 1, :] = jnp.sum(chosen[row:row + nrows, :], axis=0, keepdims=True)
                row += nrows
            na_s[sl, 8:16, :] = chosen[row:row + 8, :]
            na = na_s[sl]
            rank1 = rank_s[2 * hd, :, lanes]
            m1 = jnp.zeros((PEER_KEYS, LANES), F32)
            for a in range(PEER_TOPK):
                m1 = jnp.where(rank1 == float(a), na[a:a + 1, :], m1)
            m1_ref[hd, :, lanes] = m1
            c1_ref[hd, :, lanes] = jnp.exp(s_s[2 * hd, :, lanes] - sv1[0:1, :]) / z
            r2_ref[hd, :, lanes] = rank_s[2 * hd + 1, :, lanes].astype(BF16)
            e2_ref[hd, :, lanes] = jnp.exp(s_s[2 * hd + 1, :, lanes] - sv2[0:1, :]).astype(BF16)
        return carry

    lax.fori_loop(0, PEER_HEADS, per_head, 0)


def _route(x, attn, gm, w, *, tt, attn_transposed):
    t = x.shape[0]
    row_tile = lambda width: pl.BlockSpec((tt, width), lambda i: (i, 0))
    attn_spec = pl.BlockSpec((ATTN_W, tt), lambda i: (0, i)) if attn_transposed else row_tile(ATTN_W)
    head_tile = pl.BlockSpec((PEER_HEADS, PEER_KEYS, tt), lambda i: (0, 0, i))
    in_specs = [row_tile(D_MODEL), attn_spec, row_tile(GMLP_W), _full((1, ATTN_W)),
                _full((ATTN_W, D_MODEL)), _full((GMLP_W, D_MODEL)), _full((1, D_MODEL)),
                _full((2 * PEER_HEADS * PEER_HALF, D_MODEL)), _full((2 * PEER_HEADS, PEER_KEYS, PEER_HALF)),
                _full((_N_CAND, LANES))]
    stat = lambda dt: jax.ShapeDtypeStruct((PEER_HEADS, PEER_KEYS, t), dt)
    out_shape = (jax.ShapeDtypeStruct((t, D_MODEL), F32), jax.ShapeDtypeStruct((t, D_MODEL), BF16),
                 stat(F32), stat(F32), stat(BF16), stat(BF16))
    out_specs = (row_tile(D_MODEL), row_tile(D_MODEL), head_tile, head_tile, head_tile, head_tile)
    return pl.pallas_call(
        functools.partial(_route_kernel, attn_transposed=attn_transposed),
        grid=(t // tt,), in_specs=in_specs, out_specs=out_specs, out_shape=out_shape,
        scratch_shapes=[pltpu.VMEM((2 * PEER_HEADS, PEER_KEYS, tt), F32),
                        pltpu.VMEM((2 * PEER_HEADS, PEER_KEYS, tt), F32),
                        pltpu.VMEM((2 * PEER_HEADS, PEER_TOPK, tt), F32),
                        pltpu.VMEM((tt // LANES, PEER_TOPK, LANES), F32),
                        pltpu.VMEM((tt // LANES, _N_CAND, LANES), F32)],
        compiler_params=_params("parallel"), name="route",
    )(x, attn, gm, w["aog"], w["wo_attn"], w["wo_gmlp"], w["g2"], w["wqt"], w["keys"], w["cand_idx"])


def _peer_kernel(xn2_ref, u_ref, vt_ref, *refs, keys_per_step, n_sub):
    tables = [refs[4 * s:4 * s + 4] for s in range(n_sub)]
    h_ref, y_ref, acc_s, act_even_s, act_odd_s, gated_s = refs[4 * n_sub:]
    c = pl.program_id(1)
    n_blocks = pl.num_programs(1) - 1
    sub = xn2_ref.shape[0] // n_sub

    @pl.when(c == 0)
    def _zero():
        acc_s[...] = jnp.zeros(acc_s.shape, F32)
        act_odd_s[...] = jnp.zeros(act_odd_s.shape, BF16)

    def step(act_write, act_read):
        first_key = jnp.maximum(c - 1, 0) * keys_per_step
        for s, (m1_ref, c1_ref, r2_ref, e2_ref) in enumerate(tables):
            cols = slice(sub * s, sub * (s + 1))
            act_write[:, cols] = _gelu_sigmoid(_dot_nt(u_ref[...], xn2_ref[cols, :]).astype(BF16))
            for ii in range(keys_per_step):
                i = first_key + ii
                rows = slice(PEER_KEYS * ii, PEER_KEYS * (ii + 1))
                g = None
                for hd in range(PEER_HEADS):
                    partners = m1_ref[hd, pl.ds(i, 1), :].astype(BF16)
                    weight = c1_ref[hd, pl.ds(i, 1), :].astype(BF16)
                    term = jnp.where(r2_ref[hd] < partners, e2_ref[hd], jnp.zeros((), BF16)) * weight
                    g = term if g is None else g + term
                gated_s[rows, cols] = g * act_read[rows, cols]
            acc_s[:, cols] += _dot(vt_ref[...], gated_s[:, cols])

    @pl.when((c & 1) == 0)
    def _even():
        step(act_even_s, act_odd_s)

    @pl.when((c & 1) == 1)
    def _odd():
        step(act_odd_s, act_even_s)

    @pl.when(c == n_blocks)
    def _finish():
        y_ref[...] = h_ref[...] + acc_s[...].T


_PEER_SUBTILE = 256


def _peer(xn2, h, m1, c1, r2, e2, w, *, tt, keys_per_step=8):
    t = xn2.shape[0]
    ne = keys_per_step * PEER_KEYS
    n_blocks = N_EXPERTS // ne
    sub = min(tt, _PEER_SUBTILE)
    n_sub = tt // sub
    row_tile = pl.BlockSpec((tt, D_MODEL), lambda j, c: (j, 0))
    table_specs, table_args = [], []
    for s in range(n_sub):
        spec = pl.BlockSpec((PEER_HEADS, PEER_KEYS, sub), lambda j, c, s=s: (0, 0, j * n_sub + s))
        table_specs += [spec] * 4
        table_args += [m1, c1, r2, e2]
    return pl.pallas_call(
        functools.partial(_peer_kernel, keys_per_step=keys_per_step, n_sub=n_sub),
        grid=(t // tt, n_blocks + 1),
        in_specs=[row_tile, pl.BlockSpec((ne, D_MODEL), lambda j, c: (jnp.minimum(c, n_blocks - 1), 0)),
                  pl.BlockSpec((D_MODEL, ne), lambda j, c: (0, jnp.maximum(c - 1, 0)))] + table_specs + [row_tile],
        out_specs=row_tile, out_shape=jax.ShapeDtypeStruct((t, D_MODEL), F32),
        scratch_shapes=[pltpu.VMEM((D_MODEL, tt), F32)] + [pltpu.VMEM((ne, tt), BF16)] * 3,
        compiler_params=_params("parallel", "arbitrary"), name="peer",
    )(xn2, w["u"], w["vt"], *table_args, h)


def _proj_sample_kernel(x_ref, g1_ref, win_ref, bones_ref, qgn_ref, kgn_ref, gvg_ref, wsc_ref, bsc_ref, gog_ref,
                        k_ref, v_ref, gv_ref, gm_ref, q_ref, qt_ref):
    xn = (_rms(x_ref[...]) * g1_ref[...]).astype(BF16)
    hn = _dot(xn, win_ref[...])

    def head_norm(z, g_ref):
        hi, lo = _split(z * z)
        ss = _dot(hi, bones_ref[...]) + _dot(lo, bones_ref[...])
        return z * lax.rsqrt(ss * (1.0 / HEAD_DIM) + EPS) * g_ref[...]

    qn = head_norm(hn[:, 0:ATTN_W], qgn_ref)
    kn = head_norm(hn[:, ATTN_W:2 * ATTN_W], kgn_ref)
    k_ref[...] = kn
    v_ref[...] = hn[:, 2 * ATTN_W:3 * ATTN_W]
    gu = _gelu(hn[:, 3 * ATTN_W:3 * ATTN_W + GMLP_W])
    gvn = _gmlp_norm(_gelu(hn[:, 3 * ATTN_W + GMLP_W:]), bones_ref, gvg_ref)
    gv_ref[...] = gvn
    gm_ref[...] = (_rms(gu * (wsc_ref[...] * gvn + bsc_ref[...])) * gog_ref[...]).astype(BF16)
    qs = qn * QK_SCALE
    q_ref[...] = qs
    qt_ref[...] = qs.T


def _proj_sample(x, w):
    nb = x.shape[0]
    out_shape = (jax.ShapeDtypeStruct((nb, ATTN_W), F32), jax.ShapeDtypeStruct((nb, ATTN_W), F32),
                 jax.ShapeDtypeStruct((nb, GMLP_W), F32), jax.ShapeDtypeStruct((nb, GMLP_W), BF16),
                 jax.ShapeDtypeStruct((nb, ATTN_W), F32), jax.ShapeDtypeStruct((ATTN_W, nb), F32))
    return pl.pallas_call(
        _proj_sample_kernel, out_shape=out_shape, compiler_params=_params(), name="proj_sample",
    )(x, w["g1"], w["win"], w["bones"], w["qg_nat"], w["kg_nat"], w["gvg"], w["ws_one"], w["b_one"], w["gog"])


_PAGES_PER_STEP = 32
_PAGES_PER_BLOCK = MOBA_BLOCK // PAGE
_PAGE_BLOCK = (None, N_HEADS, HEAD_DIM, PAGE)


def _sample_scores_kernel(pt_ref, qt_ref, q_ref, knew_ref, *refs, n_steps, past_len):
    k_refs = refs[:_PAGES_PER_STEP]
    p_ref, pself_ref, sel_ref, qcol_s, sc_s, gate_s = refs[_PAGES_PER_STEP:]
    d = pl.program_id(0)
    c = pl.program_id(1)
    n_seq = qt_ref.shape[1]
    lane = _iota((N_HEADS, LANES), 1)
    blocks_per_step = _PAGES_PER_STEP // _PAGES_PER_BLOCK

    @pl.when(c == 0)
    def _init():
        pick = jnp.where(_iota((n_seq, LANES), 0) == d, 1.0, 0.0)
        qcol_s[...] = _dot3(qt_ref[...], pick)
        gate_s[...] = jnp.zeros(gate_s.shape, F32)

    gates = gate_s[...]
    for b in range(blocks_per_step):
        blk = None
        for g in range(_PAGES_PER_BLOCK):
            r = b * _PAGES_PER_BLOCK + g
            rows = [jnp.sum(k_refs[r][hd] * qcol_s[HEAD_DIM * hd:HEAD_DIM * (hd + 1), :], axis=0, keepdims=True)
                    for hd in range(N_HEADS)]
            s_page = jnp.concatenate(rows, axis=0)
            sc_s[:, pl.ds(pl.multiple_of((c * _PAGES_PER_STEP + r) * PAGE, PAGE), PAGE)] = s_page
            blk = s_page if blk is None else blk + s_page
        gate = jnp.sum(blk, axis=-1, keepdims=True) * (1.0 / MOBA_BLOCK)
        gates = jnp.where(lane == c * blocks_per_step + b, gate, gates)
    gate_s[...] = gates

    @pl.when(c == n_steps - 1)
    def _select():
        n_blocks = n_steps * blocks_per_step
        lane_f = lane.astype(F32)
        cur = jnp.where(lane < n_blocks, gates, -jnp.inf)
        sel = jnp.zeros_like(cur)
        picked = jnp.zeros_like(cur)
        for r in range(MOBA_TOPK):
            m = jnp.max(cur, axis=-1, keepdims=True)
            first = jnp.min(jnp.where(cur == m, lane_f, float(LANES)), axis=-1, keepdims=True)
            hit = lane_f == first
            sel = jnp.where(hit, 1.0, sel)
            picked = jnp.where(lane == r, first, picked)
            cur = jnp.where(hit, -jnp.inf, cur)
        sel_ref[...] = picked

        n_pos = n_blocks * MOBA_BLOCK
        chosen = jnp.concatenate([jnp.broadcast_to(sel[:, n:n + 1], (N_HEADS, MOBA_BLOCK)) for n in range(n_blocks)],
                                 axis=1)
        slopes = jnp.exp2(-(_iota((N_HEADS, 1), 0) + 1).astype(F32))
        distance = float(past_len) - _iota((1, n_pos), 1).astype(F32)
        logit = jnp.where(chosen > 0.0, sc_s[...] - slopes * distance, NEG)
        self_logit = jnp.sum(q_ref[...] * knew_ref[...], axis=-1, keepdims=True)
        top = jnp.maximum(jnp.max(logit, axis=-1, keepdims=True), self_logit)
        p = jnp.exp(logit - top)
        p_self = jnp.exp(self_logit - top)
        inv = 1.0 / (jnp.sum(p, axis=-1, keepdims=True) + p_self)
        p_ref[...] = p * inv
        pself_ref[...] = jnp.broadcast_to(p_self * inv, pself_ref.shape)


def _sample_scores(page_table, cache_kt, qt, q, k_new, past_len):
    nb, n_pages = page_table.shape
    n_steps = n_pages // _PAGES_PER_STEP
    n_blocks = n_pages // _PAGES_PER_BLOCK
    assert n_blocks <= LANES
    page_spec = lambda r: pl.BlockSpec(
        _PAGE_BLOCK, lambda d, c, pt, r=r: (pt[d, c * _PAGES_PER_STEP + r], 0, 0, 0))
    per_seq = lambda shape: pl.BlockSpec((None,) + shape, lambda d, c, pt: (d,) + (0,) * len(shape))
    grid_spec = pltpu.PrefetchScalarGridSpec(
        num_scalar_prefetch=1, grid=(nb, n_steps),
        in_specs=[pl.BlockSpec((ATTN_W, nb), lambda d, c, pt: (0, 0)), per_seq((N_HEADS, HEAD_DIM)),
                  per_seq((N_HEADS, HEAD_DIM))] + [page_spec(r) for r in range(_PAGES_PER_STEP)],
        out_specs=(per_seq((N_HEADS, n_pages * PAGE)), per_seq((N_HEADS, HEAD_DIM)), per_seq((N_HEADS, LANES))),
        scratch_shapes=[pltpu.VMEM((ATTN_W, LANES), F32), pltpu.VMEM((N_HEADS, n_pages * PAGE), F32),
                        pltpu.VMEM((N_HEADS, LANES), F32)])
    out_shape = (jax.ShapeDtypeStruct((nb, N_HEADS, n_pages * PAGE), F32),
                 jax.ShapeDtypeStruct((nb, N_HEADS, HEAD_DIM), F32), jax.ShapeDtypeStruct((nb, N_HEADS, LANES), F32))
    return pl.pallas_call(
        functools.partial(_sample_scores_kernel, n_steps=n_steps, past_len=past_len),
        grid_spec=grid_spec, out_shape=out_shape,
        compiler_params=_params("parallel", "arbitrary"), name="sample_scores",
    )(page_table, qt, q, k_new, *([cache_kt] * _PAGES_PER_STEP))


_SLABS_PER_HEAD = MOBA_TOPK * _PAGES_PER_BLOCK


def _sample_values_kernel(phys_ref, logical_ref, p_ref, pself_ref, vnew_ref, *refs):
    n_slabs = N_HEADS * _SLABS_PER_HEAD
    v_refs = refs[:n_slabs]
    out_ref = refs[n_slabs]
    d = pl.program_id(0)
    cols = []
    for hd in range(N_HEADS):
        acc = None
        for k in range(_SLABS_PER_HEAD):
            i = hd * _SLABS_PER_HEAD + k
            start = pl.multiple_of(logical_ref[d, i] * PAGE, PAGE)
            term = v_refs[i][...] * p_ref[hd:hd + 1, pl.ds(start, PAGE)]
            acc = term if acc is None else acc + term
        cols.append(jnp.sum(acc, axis=-1, keepdims=True))
    ctx_t = jnp.concatenate(cols + [jnp.zeros((HEAD_DIM, LANES - N_HEADS), F32)], axis=1)
    out_ref[...] = ctx_t.T[0:N_HEADS, :] + pself_ref[...] * vnew_ref[...]


def _sample_values(slab_phys, slab_logical, cache_vt, p, pself, v_new):
    nb, n_slabs = slab_phys.shape
    assert n_slabs == N_HEADS * _SLABS_PER_HEAD
    slab_spec = lambda i: pl.BlockSpec(
        (None, None, HEAD_DIM, PAGE), lambda d, ph, lg, i=i: (ph[d, i], i // _SLABS_PER_HEAD, 0, 0))
    per_seq = lambda shape: pl.BlockSpec((None,) + shape, lambda d, ph, lg: (d,) + (0,) * len(shape))
    head_rows = per_seq((N_HEADS, HEAD_DIM))
    grid_spec = pltpu.PrefetchScalarGridSpec(
        num_scalar_prefetch=2, grid=(nb,),
        in_specs=[per_seq((N_HEADS, p.shape[2])), head_rows, head_rows] + [slab_spec(i) for i in range(n_slabs)],
        out_specs=head_rows)
    return pl.pallas_call(
        _sample_values_kernel, grid_spec=grid_spec, out_shape=jax.ShapeDtypeStruct((nb, N_HEADS, HEAD_DIM), F32),
        compiler_params=_params("parallel"), name="sample_values",
    )(slab_phys, slab_logical, p, pself, v_new, *([cache_vt] * n_slabs))


def _prepare(norm1_g, w_in, q_norm_g, k_norm_g, gmlp_v_norm_g, gmlp_ws, gmlp_b, attn_out_norm_g, gmlp_out_norm_g,
             w_out, norm2_g, peer_wq, peer_keys, peer_u, peer_v):
    wq, wk, wv, wgu, wgv = jnp.split(w_in, [ATTN_W, 2 * ATTN_W, 3 * ATTN_W, 3 * ATTN_W + GMLP_W], axis=1)
    wk_pad = jnp.pad(wk.reshape(D_MODEL, N_HEADS, HEAD_DIM), ((0, 0), (0, 0), (0, LANES - HEAD_DIM)))
    feat = np.arange(ATTN_W)
    w = {
        "g1": norm1_g.reshape(1, D_MODEL),
        "win": w_in.astype(BF16),
        "wnat": jnp.concatenate([wk_pad.reshape(D_MODEL, N_HEADS * LANES), wk, wv, wgu, wgv], axis=1).astype(BF16),
        "wt": jnp.concatenate([wq, wv], axis=1).T.astype(BF16),
        "qg_col": q_norm_g.reshape(HEAD_DIM, 1),
        "kg_pad": jnp.pad(k_norm_g, (0, LANES - HEAD_DIM)).reshape(1, LANES),
        "qg_nat": jnp.tile(q_norm_g, N_HEADS).reshape(1, ATTN_W),
        "kg_nat": jnp.tile(k_norm_g, N_HEADS).reshape(1, ATTN_W),
        "gvg": gmlp_v_norm_g.reshape(1, GMLP_W),
        "bones": jnp.asarray((feat[:, None] // GROUP_DIM) == (feat[None, :] // GROUP_DIM), BF16),
        "wcat": jnp.transpose(gmlp_ws, (1, 0, 2)).reshape(GMLP_CHUNK, N_GROUPS * GMLP_CHUNK),
        "gbias": jnp.repeat(gmlp_b.T, GROUP_DIM, axis=1),
        "ws_one": jnp.repeat(gmlp_ws[:, 0, 0], GROUP_DIM).reshape(1, GMLP_W),
        "b_one": jnp.repeat(gmlp_b[:, 0], GROUP_DIM).reshape(1, GMLP_W),
        "gog": gmlp_out_norm_g.reshape(1, GMLP_W),
        "aog": attn_out_norm_g.reshape(1, ATTN_W),
        "wo_attn": w_out[:ATTN_W].astype(BF16),
        "wo_gmlp": w_out[ATTN_W:].astype(BF16),
        "g2": norm2_g.reshape(1, D_MODEL),
        "wqt": peer_wq.T.astype(BF16),
        "keys": peer_keys.reshape(2 * PEER_HEADS, PEER_KEYS, PEER_HALF),
        "u": peer_u.astype(BF16),
        "vt": peer_v.T.astype(BF16),
        "cand_idx": _cand_index_table(),
    }
    return w


def _sample_attention(page_table, cache_k, cache_v, q, qt, k_new, v_new, past_len):
    nb = q.shape[0]
    heads = lambda a: a.reshape(nb, N_HEADS, HEAD_DIM)
    as_stored = lambda cache: jnp.transpose(cache, (0, 2, 3, 1))
    p, pself, picked = _sample_scores(page_table, as_stored(cache_k), qt, heads(q), heads(k_new), past_len)
    blocks = picked[:, :, :MOBA_TOPK].astype(jnp.int32)
    logical = (blocks[..., None] * _PAGES_PER_BLOCK + jnp.arange(_PAGES_PER_BLOCK, dtype=jnp.int32)).reshape(nb, -1)
    phys = jnp.take_along_axis(page_table, logical, axis=1)
    ctx = _sample_values(phys, logical.astype(jnp.int32), as_stored(cache_v), p, pself, heads(v_new))
    return ctx.reshape(nb, ATTN_W)


def _layer_tail(x, attn, gm, w, *, route_tile, peer_tile, attn_transposed):
    h, xn2, m1, c1, r2, e2 = _route(x, attn, gm, w, tt=route_tile, attn_transposed=attn_transposed)
    return _peer(xn2, h, m1, c1, r2, e2, w, tt=peer_tile)


def kernel(x_prompt, x_sample, cache_k, cache_v, page_table, norm1_g, w_in, q_norm_g, k_norm_g, gmlp_v_norm_g,
           gmlp_ws, gmlp_b, attn_out_norm_g, gmlp_out_norm_g, w_out, norm2_g, peer_wq, peer_keys, peer_u, peer_v):
    w = _prepare(norm1_g, w_in, q_norm_g, k_norm_g, gmlp_v_norm_g, gmlp_ws, gmlp_b, attn_out_norm_g,
                 gmlp_out_norm_g, w_out, norm2_g, peer_wq, peer_keys, peer_u, peer_v)
    b, t, _ = x_prompt.shape
    assert b == 1
    xp = x_prompt.reshape(t, D_MODEL)
    k_p, v_p, kaug, qaug, vt, gm_p, gv_last = _proj_prompt(xp, w)
    attn_t = _moba_prompt(kaug, qaug, vt)
    y_p = _layer_tail(xp, attn_t, gm_p, w, route_tile=512, peer_tile=512, attn_transposed=True)

    nb, ds, _ = x_sample.shape
    assert ds == 1
    xs = x_sample.reshape(nb, D_MODEL)
    past_len = page_table.shape[1] * PAGE
    assert past_len % MOBA_BLOCK == 0
    k_s, v_s, gv_s, gm_s, q_s, qt_s = _proj_sample(xs, w)
    attn_s = _sample_attention(page_table, cache_k, cache_v, q_s, qt_s, k_s, v_s, past_len)
    y_s = _layer_tail(xs, attn_s, gm_s, w, route_tile=nb, peer_tile=nb, attn_transposed=False)

    return (y_p.reshape(1, t, D_MODEL), y_s.reshape(nb, 1, D_MODEL),
            k_p.reshape(1, t, N_HEADS, HEAD_DIM), v_p.reshape(1, t, N_HEADS, HEAD_DIM),
            gv_last.reshape(1, GMLP_CHUNK, GMLP_W),
            k_s.reshape(nb, 1, N_HEADS, HEAD_DIM), v_s.reshape(nb, 1, N_HEADS, HEAD_DIM),
            gv_s.reshape(nb, 1, GMLP_W))
```

```python
import functools
import math

import jax
import jax.numpy as jnp
import numpy as np
from jax import lax
from jax.experimental import pallas as pl
from jax.experimental.pallas import tpu as pltpu

F32 = jnp.float32
BF16 = jnp.bfloat16

D_MODEL = 1024
N_HEADS = 8
HEAD_DIM = 64
ATTN_W = N_HEADS * HEAD_DIM
GMLP_W = 512
N_GROUPS = 8
GROUP_DIM = GMLP_W // N_GROUPS
GMLP_CHUNK = 128
MOBA_BLOCK = 256
MOBA_TOPK = 3
PAGE = 128
PEER_HEADS = 8
PEER_KEYS = 128
PEER_HALF = 128
PEER_TOPK = 16
N_EXPERTS = PEER_KEYS * PEER_KEYS
EPS = 1e-6
NEG = -1e30
QK_SCALE = HEAD_DIM ** -0.5
GELU_C = math.sqrt(2.0 / math.pi)


def _bf16_pieces(x, n):
    out, rest = [], np.float32(x)
    for _ in range(n):
        piece = np.float32(rest.astype(jnp.bfloat16))
        out.append(float(piece))
        rest = np.float32(rest - piece)
    return tuple(out)


LOG2E = float(np.float32(math.log2(math.e)))
_LOG2E_PIECES = _bf16_pieces(LOG2E, 3)
_GELU_A = _bf16_pieces(-2.0 * GELU_C * 0.044715, 2)
_GELU_B = _bf16_pieces(-2.0 * GELU_C, 2)

LANES = 128
VMEM_LIMIT = 56 * 1024 * 1024

_CAND_ROWS = tuple((a, PEER_TOPK // (a + 1)) for a in range(8))


def _dot(a, b):
    return jnp.dot(a, b, preferred_element_type=F32)


def _dot_nt(a, b):
    return lax.dot_general(a, b, (((1,), (1,)), ((), ())), preferred_element_type=F32)


def _split(a):
    hi = a.astype(BF16)
    lo = (a - hi.astype(F32)).astype(BF16)
    return hi, lo


def _dot3(a, b):
    ah, al = _split(a)
    bh, bl = _split(b)
    return _dot(ah, bh) + (_dot(ah, bl) + _dot(al, bh))


def _rms(x, axis=-1):
    return x * lax.rsqrt(jnp.mean(x * x, axis=axis, keepdims=True) + EPS)


def _gelu(x):
    return x * (0.5 * (1.0 + jnp.tanh(GELU_C * (x + 0.044715 * (x * x * x)))))


def _gelu_sigmoid(x):
    t = x * x
    if x.dtype == BF16:
        poly = (_GELU_A[0] * t + _GELU_B[0]) + (_GELU_A[1] * t + _GELU_B[1])
    else:
        poly = (-2.0 * GELU_C * 0.044715) * t - 2.0 * GELU_C
    return x / (1.0 + jnp.exp(x * poly))


def _iota(shape, dim, dtype=jnp.int32):
    return lax.broadcasted_iota(dtype, shape, dim)


def _params(*sem):
    return pltpu.CompilerParams(dimension_semantics=sem, vmem_limit_bytes=VMEM_LIMIT)


def _full(shape):
    nd = len(shape)
    return pl.BlockSpec(shape, lambda *_: (0,) * nd)


_KP0, _KN0, _V0, _GU0, _GV0, _NAT_COLS = 0, 1024, 1536, 2048, 2560, 3072
_VT_ROWS = HEAD_DIM + 16


def _gmlp_norm(gvr, bones_ref, gvg_ref):
    hi, lo = _split(gvr * gvr)
    ss = _dot(hi, bones_ref[...]) + _dot(lo, bones_ref[...])
    return gvr * lax.rsqrt(ss * (1.0 / GROUP_DIM) + EPS) * gvg_ref[...]


def _proj_prompt_kernel(x_ref, g1_ref, wnat_ref, wt_ref, qg_ref, kgp_ref, kgn_ref, gvg_ref, bones_ref,
                        wcat_ref, gbias_ref, gog_ref,
                        k_ref, v_ref, kaug_ref, qaug_ref, vt_ref, gm_ref, gvl_ref,
                        kmean_s, wtril_s):
    i = pl.program_id(0)
    tm = x_ref.shape[0]
    nblk = kmean_s.shape[1]

    @pl.when(i == 0)
    def _init():
        row = _iota((GMLP_CHUNK, N_GROUPS * GMLP_CHUNK), 0)
        col = _iota((GMLP_CHUNK, N_GROUPS * GMLP_CHUNK), 1) & (GMLP_CHUNK - 1)
        wtril_s[...] = jnp.where(col <= row, wcat_ref[...], 0.0).astype(BF16)
        kmean_s[...] = jnp.zeros(kmean_s.shape, F32)

    xn = (_rms(x_ref[...]) * g1_ref[...]).astype(BF16)
    hn = _dot(xn, wnat_ref[...])
    ht = _dot_nt(wt_ref[...], xn)

    lane = _iota((tm, LANES), 1)
    rowf = _iota((tm, LANES), 0).astype(F32)
    blk_row = _iota((nblk, tm), 0)
    blk_rowf = blk_row.astype(F32)
    piece_row = _iota((LANES, tm), 0)
    i_f = i.astype(F32)
    k_hi = jnp.where(lane < 3, rowf, jnp.where(lane < 6, float(MOBA_BLOCK) * i_f, 0.0))
    ones_rows = jnp.where(_iota((_VT_ROWS - HEAD_DIM, tm), 0) == 0, 1.0, 0.0)

    rs_heads = []
    for h in range(N_HEADS):
        kp = hn[:, _KP0 + LANES * h:_KP0 + LANES * (h + 1)]
        rs = lax.rsqrt(jnp.sum(kp * kp, axis=-1, keepdims=True) * (1.0 / HEAD_DIM) + EPS)
        rs_heads.append(rs)
        kn = kp * rs * kgp_ref[...]
        kmean_s[h, pl.ds(i, 1), :] = jnp.mean(kn, axis=0, keepdims=True)

        qt = ht[HEAD_DIM * h:HEAD_DIM * (h + 1), :]
        qn = qt * lax.rsqrt(jnp.sum(qt * qt, axis=0, keepdims=True) * (1.0 / HEAD_DIM) + EPS) * qg_ref[...]
        gate = _dot3(kmean_s[h], jnp.concatenate([qn, jnp.zeros_like(qn)], axis=0))
        cur = jnp.where(blk_row < i, gate, NEG)
        sel = jnp.zeros_like(cur)
        for _ in range(MOBA_TOPK):
            m = jnp.max(cur, axis=0, keepdims=True)
            first = jnp.min(jnp.where(cur == m, blk_rowf, float(nblk)), axis=0, keepdims=True)
            hit = blk_rowf == first
            sel = jnp.where(hit, 1.0, sel)
            cur = jnp.where(hit, -jnp.inf, cur)
        keep = jnp.where(blk_row < i, sel, jnp.where(blk_row == i, 1.0, 0.0))
        sel_bias = jnp.where(keep > 0.0, 0.0, NEG)
        pieces = [qn * (QK_SCALE * LOG2E), sel_bias]
        if nblk < HEAD_DIM:
            pieces.append(jnp.zeros((HEAD_DIM - nblk, tm), F32))
        slope = 2.0 ** -(h + 1)
        slope_rows = jnp.zeros((LANES, tm), F32)
        for r, piece in enumerate(_LOG2E_PIECES * 2):
            slope_rows = jnp.where(piece_row == r, slope * piece, slope_rows)
        qaug_ref[h] = jnp.concatenate(pieces + [slope_rows], axis=0).astype(BF16)

        k_lo = jnp.where(lane == HEAD_DIM + i, 1.0, kn)
        kaug_ref[h] = jnp.concatenate([k_lo, k_hi], axis=1).astype(BF16)
        vt_ref[h] = jnp.concatenate([ht[ATTN_W + HEAD_DIM * h:ATTN_W + HEAD_DIM * (h + 1), :], ones_rows],
                                    axis=0).astype(BF16)

    cols = []
    for c in range(ATTN_W // LANES):
        sc = jnp.where(lane < HEAD_DIM, rs_heads[2 * c], rs_heads[2 * c + 1])
        cols.append(hn[:, _KN0 + LANES * c:_KN0 + LANES * (c + 1)] * sc)
    k_ref[...] = jnp.concatenate(cols, axis=1) * kgn_ref[...]
    v_ref[...] = hn[:, _V0:_V0 + ATTN_W]

    gu = _gelu(hn[:, _GU0:_GU0 + GMLP_W])
    gvn = _gmlp_norm(_gelu(hn[:, _GV0:_GV0 + GMLP_W]), bones_ref, gvg_ref)
    gvl_ref[...] = gvn[tm - GMLP_CHUNK:, :]
    group_of_lane = _iota((1, GMLP_W), 1) >> 6
    outs = []
    for c in range(tm // GMLP_CHUNK):
        gc = gvn[GMLP_CHUNK * c:GMLP_CHUNK * (c + 1), :]
        stacked = jnp.concatenate([jnp.where(group_of_lane == g, gc, 0.0) for g in range(N_GROUPS)], axis=0)
        mixed = _dot(wtril_s[...], stacked.astype(BF16)) + gbias_ref[...]
        outs.append(gu[GMLP_CHUNK * c:GMLP_CHUNK * (c + 1), :] * mixed)
    gm_ref[...] = (_rms(jnp.concatenate(outs, axis=0)) * gog_ref[...]).astype(BF16)


def _proj_prompt(x, w, tm=MOBA_BLOCK):
    t = x.shape[0]
    nblk = t // tm
    assert nblk <= HEAD_DIM
    row_tile = lambda width: pl.BlockSpec((tm, width), lambda i: (i, 0))
    in_specs = [row_tile(D_MODEL), _full((1, D_MODEL)), _full((D_MODEL, _NAT_COLS)), _full((D_MODEL, D_MODEL)),
                _full((HEAD_DIM, 1)), _full((1, LANES)), _full((1, ATTN_W)), _full((1, GMLP_W)),
                _full((GMLP_W, GMLP_W)), _full((GMLP_CHUNK, N_GROUPS * GMLP_CHUNK)), _full((GMLP_CHUNK, GMLP_W)),
                _full((1, GMLP_W))]
    out_shape = (jax.ShapeDtypeStruct((t, ATTN_W), F32), jax.ShapeDtypeStruct((t, ATTN_W), F32),
                 jax.ShapeDtypeStruct((N_HEADS, t, 2 * LANES), BF16),
                 jax.ShapeDtypeStruct((N_HEADS, 2 * LANES, t), BF16),
                 jax.ShapeDtypeStruct((N_HEADS, _VT_ROWS, t), BF16),
                 jax.ShapeDtypeStruct((t, GMLP_W), BF16), jax.ShapeDtypeStruct((GMLP_CHUNK, GMLP_W), F32))
    out_specs = (row_tile(ATTN_W), row_tile(ATTN_W),
                 pl.BlockSpec((N_HEADS, tm, 2 * LANES), lambda i: (0, i, 0)),
                 pl.BlockSpec((N_HEADS, 2 * LANES, tm), lambda i: (0, 0, i)),
                 pl.BlockSpec((N_HEADS, _VT_ROWS, tm), lambda i: (0, 0, i)),
                 row_tile(GMLP_W), _full((GMLP_CHUNK, GMLP_W)))
    return pl.pallas_call(
        _proj_prompt_kernel, grid=(nblk,), in_specs=in_specs, out_specs=out_specs, out_shape=out_shape,
        scratch_shapes=[pltpu.VMEM((N_HEADS, nblk, LANES), F32),
                        pltpu.VMEM((GMLP_CHUNK, N_GROUPS * GMLP_CHUNK), BF16)],
        compiler_params=_params("arbitrary"), name="proj_prompt",
    )(x, w["g1"], w["wnat"], w["wt"], w["qg_col"], w["kg_pad"], w["kg_nat"], w["gvg"], w["bones"],
      w["wcat"], w["gbias"], w["gog"])


_MOBA_HEADS_PER_STEP = 4


def _moba_kernel(kaug_ref, qaug_ref, vt_ref, out_ref, s_even, s_odd):
    j = pl.program_id(1)
    tq = qaug_ref.shape[2]
    heads = range(_MOBA_HEADS_PER_STEP)

    def block_start(n):
        return pl.multiple_of(n * MOBA_BLOCK, MOBA_BLOCK)

    def produce(buf, n):
        for hh in heads:
            buf[hh] = _dot(kaug_ref[hh, pl.ds(block_start(n), MOBA_BLOCK), :], qaug_ref[hh])

    def fold(state, s, hh, n):
        m, acc = state
        m_new = jnp.maximum(m, jnp.max(s, axis=0, keepdims=True))
        alpha = jnp.exp2(m - m_new)
        p = jnp.exp2(s - m_new).astype(BF16)
        acc = acc * alpha + _dot(vt_ref[hh, :, pl.ds(block_start(n), MOBA_BLOCK)], p)
        return m_new, acc

    def body(i, states):
        produce(s_odd, 2 * i + 1)
        states = tuple(fold(states[hh], s_even[hh], hh, 2 * i) for hh in heads)
        produce(s_even, 2 * i + 2)
        return tuple(fold(states[hh], s_odd[hh], hh, 2 * i + 1) for hh in heads)

    init = tuple((jnp.full((1, tq), -jnp.inf, F32), jnp.zeros((_VT_ROWS, tq), F32)) for _ in heads)
    produce(s_even, 0)
    n_pairs = j // 2
    states = lax.fori_loop(0, n_pairs // 2, lambda k, st: body(2 * k + 1, body(2 * k, st)), init)
    states = lax.cond((n_pairs & 1) == 1, lambda st: body(n_pairs - 1, st), lambda st: st, states)

    causal = _iota((MOBA_BLOCK, tq), 0) <= _iota((MOBA_BLOCK, tq), 1)
    j_odd = (j & 1) == 1
    produce(s_odd, j)
    for hh in heads:
        first = jnp.where(j_odd, s_even[hh], jnp.where(causal, s_even[hh], NEG))
        state = fold(states[hh], first, hh, 2 * (j // 2))
        second = jnp.where(j_odd, jnp.where(causal, s_odd[hh], NEG), NEG)
        _, acc = fold(state, second, hh, j)
        out_ref[HEAD_DIM * hh:HEAD_DIM * (hh + 1), :] = acc[:HEAD_DIM, :] / acc[HEAD_DIM:HEAD_DIM + 1, :]


def _moba_prompt(kaug, qaug, vt):
    t = kaug.shape[1]
    tq = MOBA_BLOCK
    hs = _MOBA_HEADS_PER_STEP
    once = pl.Buffered(1)
    return pl.pallas_call(
        _moba_kernel, grid=(N_HEADS // hs, t // tq),
        in_specs=[pl.BlockSpec((hs, t, 2 * LANES), lambda g, j: (g, 0, 0), pipeline_mode=once),
                  pl.BlockSpec((hs, 2 * LANES, tq), lambda g, j: (g, 0, j)),
                  pl.BlockSpec((hs, _VT_ROWS, t), lambda g, j: (g, 0, 0), pipeline_mode=once)],
        out_specs=pl.BlockSpec((hs * HEAD_DIM, tq), lambda g, j: (g, j)),
        out_shape=jax.ShapeDtypeStruct((ATTN_W, t), F32),
        scratch_shapes=[pltpu.VMEM((hs, MOBA_BLOCK, tq), F32), pltpu.VMEM((hs, MOBA_BLOCK, tq), F32)],
        compiler_params=_params("parallel", "arbitrary"), name="moba_prompt",
    )(kaug, qaug, vt)


def _cand_index_table():
    rows = []
    for a, nb in _CAND_ROWS:
        for c0 in range(0, max(nb, 8), 8):
            rows.append([a * PEER_TOPK + c0 + b for b in range(8)])
    rows.append([(8 + a) * PEER_TOPK for a in range(8)])
    flat = np.asarray(rows, np.float32).reshape(-1, 1)
    return jnp.asarray(np.broadcast_to(flat, (flat.shape[0], LANES)).copy())


_N_CAND = 8 * (sum(max(nb, 8) // 8 for _, nb in _CAND_ROWS) + 1)


def _extract_top(cur, idx, n, on_hit):
    big = float(1 << 20)
    for r in range(n):
        m = jnp.max(cur, axis=0, keepdims=True)
        first = jnp.min(jnp.where(cur == m, idx, big), axis=0, keepdims=True)
        hit = idx == first
        cur = jnp.where(hit, -jnp.inf, cur)
        on_hit(r, m, hit)
    return cur


def _route_kernel(x_ref, attn_ref, gm_ref, aog_ref, woa_ref, wog_ref, g2_ref, wqt_ref, keys_ref, cidx_ref,
                  h_ref, xn2_ref, m1_ref, c1_ref, r2_ref, e2_ref,
                  s_s, rank_s, sv_s, na_s, chosen_s, *, attn_transposed):
    tt = x_ref.shape[0]
    at = attn_ref[...]
    if attn_transposed:
        at = at.T
    an = (_rms(at) * aog_ref[...]).astype(BF16)
    h = x_ref[...] + _dot(an, woa_ref[...]) + _dot(gm_ref[...], wog_ref[...])
    h_ref[...] = h
    xn2 = (_rms(h) * g2_ref[...]).astype(BF16)
    xn2_ref[...] = xn2
    qt = _dot_nt(wqt_ref[...], xn2)
    for hp in range(2 * PEER_HEADS):
        s_s[hp] = _dot3(keys_ref[hp], qt[PEER_HALF * hp:PEER_HALF * (hp + 1), :])

    key_idx = _iota((PEER_KEYS, LANES), 0).astype(F32)
    row8 = _iota((8, LANES), 0)

    def rank_distinct(hp, lanes):
        cur = s_s[hp, :, lanes]
        rank = jnp.full((PEER_KEYS, LANES), float(PEER_TOPK), F32)
        for r in range(PEER_TOPK):
            m = jnp.max(cur, axis=0, keepdims=True)
            hit = cur == m
            cur = jnp.where(hit, -jnp.inf, cur)
            rank = jnp.where(hit, float(r), rank)
            sv_s[hp, r:r + 1, lanes] = m
        rank_s[hp, :, lanes] = rank
        return jnp.sum(jnp.where(rank < float(PEER_TOPK), 1.0, 0.0), axis=0, keepdims=True)

    def rank_exact(hp, lanes):
        rank = [jnp.full((PEER_KEYS, LANES), float(PEER_TOPK), F32)]

        def on_hit(r, m, hit):
            rank[0] = jnp.where(hit, float(r), rank[0])
            sv_s[hp, r:r + 1, lanes] = m

        _extract_top(s_s[hp, :, lanes], key_idx, PEER_TOPK, on_hit)
        rank_s[hp, :, lanes] = rank[0]

    def per_head(hd, carry):
        ranked = []
        for sl in range(tt // LANES):
            lanes = slice(LANES * sl, LANES * (sl + 1))
            for hp in (2 * hd, 2 * hd + 1):
                ranked.append((hp, lanes, rank_distinct(hp, lanes)))
        off_count = functools.reduce(jnp.maximum, [jnp.abs(n - float(PEER_TOPK)) for _, _, n in ranked])

        @pl.when(jnp.max(off_count) > 0.0)
        def _redo_ranks():
            for hp, lanes, _ in ranked:
                rank_exact(hp, lanes)
        def candidates(lanes):
            sv1 = sv_s[2 * hd, :, lanes]
            sv2 = sv_s[2 * hd + 1, :, lanes]
            pieces = []
            for a, nb in _CAND_ROWS:
                for c0 in range(0, max(nb, 8), 8):
                    piece = sv1[a:a + 1, :] + sv2[c0:c0 + 8, :]
                    pieces.append(piece if nb >= 8 else jnp.where(row8 < nb, piece, -jnp.inf))
            pieces.append(sv1[8:16, :] + sv2[0:1, :])
            return sv1, sv2, jnp.concatenate(pieces, axis=0)

        n_chosen = []
        for sl in range(tt // LANES):
            _, _, cur = candidates(slice(LANES * sl, LANES * (sl + 1)))
            chosen = jnp.zeros_like(cur)
            for _ in range(PEER_TOPK):
                hit = cur == jnp.max(cur, axis=0, keepdims=True)
                cur = jnp.where(hit, -jnp.inf, cur)
                chosen = jnp.where(hit, 1.0, chosen)
            chosen_s[sl] = chosen
            n_chosen.append(jnp.sum(chosen, axis=0, keepdims=True))
        off_count = functools.reduce(jnp.maximum, [jnp.abs(n - float(PEER_TOPK)) for n in n_chosen])

        @pl.when(jnp.max(off_count) > 0.0)
        def _redo_sums():
            for sl in range(tt // LANES):
                _, _, cand = candidates(slice(LANES * sl, LANES * (sl + 1)))
                chosen = [jnp.zeros_like(cand)]

                def on_hit(r, m, hit):
                    chosen[0] = jnp.where(hit, 1.0, chosen[0])

                _extract_top(cand, cidx_ref[...], PEER_TOPK, on_hit)
                chosen_s[sl] = chosen[0]

        for sl in range(tt // LANES):
            lanes = slice(LANES * sl, LANES * (sl + 1))
            sv1, sv2, cand = candidates(lanes)
            chosen = chosen_s[sl]
            top = sv1[0:1, :] + sv2[0:1, :]
            z = jnp.sum(jnp.where(chosen > 0.0, jnp.exp(cand - top), 0.0), axis=0, keepdims=True)
            row = 0
            for a, nb in _CAND_ROWS:
                nrows = max(nb, 8)
                na_s[sl, a:a + 1, :] = jnp.sum(chosen[row:row + nrows, :], axis=0, keepdims=True)
                row += nrows
            na_s[sl, 8:16, :] = chosen[row:row + 8, :]
            na = na_s[sl]
            rank1 = rank_s[2 * hd, :, lanes]
            m1 = jnp.zeros((PEER_KEYS, LANES), F32)
            for a in range(PEER_TOPK):
                m1 = jnp.where(rank1 == float(a), na[a:a + 1, :], m1)
            m1_ref[hd, :, lanes] = m1
            c1_ref[hd, :, lanes] = jnp.exp(s_s[2 * hd, :, lanes] - sv1[0:1, :]) / z
            r2_ref[hd, :, lanes] = rank_s[2 * hd + 1, :, lanes].astype(BF16)
            e2_ref[hd, :, lanes] = jnp.exp(s_s[2 * hd + 1, :, lanes] - sv2[0:1, :]).astype(BF16)
        return carry

    lax.fori_loop(0, PEER_HEADS, per_head, 0)


def _route(x, attn, gm, w, *, tt, attn_transposed):
    t = x.shape[0]
    row_tile = lambda width: pl.BlockSpec((tt, width), lambda i: (i, 0))
    attn_spec = pl.BlockSpec((ATTN_W, tt), lambda i: (0, i)) if attn_transposed else row_tile(ATTN_W)
    head_tile = pl.BlockSpec((PEER_HEADS, PEER_KEYS, tt), lambda i: (0, 0, i))
    in_specs = [row_tile(D_MODEL), attn_spec, row_tile(GMLP_W), _full((1, ATTN_W)),
                _full((ATTN_W, D_MODEL)), _full((GMLP_W, D_MODEL)), _full((1, D_MODEL)),
                _full((2 * PEER_HEADS * PEER_HALF, D_MODEL)), _full((2 * PEER_HEADS, PEER_KEYS, PEER_HALF)),
                _full((_N_CAND, LANES))]
    stat = lambda dt: jax.ShapeDtypeStruct((PEER_HEADS, PEER_KEYS, t), dt)
    out_shape = (jax.ShapeDtypeStruct((t, D_MODEL), F32), jax.ShapeDtypeStruct((t, D_MODEL), BF16),
                 stat(F32), stat(F32), stat(BF16), stat(BF16))
    out_specs = (row_tile(D_MODEL), row_tile(D_MODEL), head_tile, head_tile, head_tile, head_tile)
    return pl.pallas_call(
        functools.partial(_route_kernel, attn_transposed=attn_transposed),
        grid=(t // tt,), in_specs=in_specs, out_specs=out_specs, out_shape=out_shape,
        scratch_shapes=[pltpu.VMEM((2 * PEER_HEADS, PEER_KEYS, tt), F32),
                        pltpu.VMEM((2 * PEER_HEADS, PEER_KEYS, tt), F32),
                        pltpu.VMEM((2 * PEER_HEADS, PEER_TOPK, tt), F32),
                        pltpu.VMEM((tt // LANES, PEER_TOPK, LANES), F32),
                        pltpu.VMEM((tt // LANES, _N_CAND, LANES), F32)],
        compiler_params=_params("parallel"), name="route",
    )(x, attn, gm, w["aog"], w["wo_attn"], w["wo_gmlp"], w["g2"], w["wqt"], w["keys"], w["cand_idx"])


def _peer_kernel(xn2_ref, u_ref, vt_ref, *refs, keys_per_step, n_sub):
    tables = [refs[4 * s:4 * s + 4] for s in range(n_sub)]
    h_ref, y_ref, acc_s, act_even_s, act_odd_s, gated_s = refs[4 * n_sub:]
    c = pl.program_id(1)
    n_blocks = pl.num_programs(1) - 1
    sub = xn2_ref.shape[0] // n_sub

    @pl.when(c == 0)
    def _zero():
        acc_s[...] = jnp.zeros(acc_s.shape, F32)
        act_odd_s[...] = jnp.zeros(act_odd_s.shape, BF16)

    def step(act_write, act_read):
        first_key = jnp.maximum(c - 1, 0) * keys_per_step
        for s, (m1_ref, c1_ref, r2_ref, e2_ref) in enumerate(tables):
            cols = slice(sub * s, sub * (s + 1))
            act_write[:, cols] = _gelu_sigmoid(_dot_nt(u_ref[...], xn2_ref[cols, :]).astype(BF16))
            for ii in range(keys_per_step):
                i = first_key + ii
                rows = slice(PEER_KEYS * ii, PEER_KEYS * (ii + 1))
                g = None
                for hd in range(PEER_HEADS):
                    partners = m1_ref[hd, pl.ds(i, 1), :].astype(BF16)
                    weight = c1_ref[hd, pl.ds(i, 1), :].astype(BF16)
                    term = jnp.where(r2_ref[hd] < partners, e2_ref[hd], jnp.zeros((), BF16)) * weight
                    g = term if g is None else g + term
                gated_s[rows, cols] = g * act_read[rows, cols]
            acc_s[:, cols] += _dot(vt_ref[...], gated_s[:, cols])

    @pl.when((c & 1) == 0)
    def _even():
        step(act_even_s, act_odd_s)

    @pl.when((c & 1) == 1)
    def _odd():
        step(act_odd_s, act_even_s)

    @pl.when(c == n_blocks)
    def _finish():
        y_ref[...] = h_ref[...] + acc_s[...].T


_PEER_SUBTILE = 256


def _peer(xn2, h, m1, c1, r2, e2, w, *, tt, keys_per_step=8):
    t = xn2.shape[0]
    ne = keys_per_step * PEER_KEYS
    n_blocks = N_EXPERTS // ne
    sub = min(tt, _PEER_SUBTILE)
    n_sub = tt // sub
    row_tile = pl.BlockSpec((tt, D_MODEL), lambda j, c: (j, 0))
    table_specs, table_args = [], []
    for s in range(n_sub):
        spec = pl.BlockSpec((PEER_HEADS, PEER_KEYS, sub), lambda j, c, s=s: (0, 0, j * n_sub + s))
        table_specs += [spec] * 4
        table_args += [m1, c1, r2, e2]
    return pl.pallas_call(
        functools.partial(_peer_kernel, keys_per_step=keys_per_step, n_sub=n_sub),
        grid=(t // tt, n_blocks + 1),
        in_specs=[row_tile, pl.BlockSpec((ne, D_MODEL), lambda j, c: (jnp.minimum(c, n_blocks - 1), 0)),
                  pl.BlockSpec((D_MODEL, ne), lambda j, c: (0, jnp.maximum(c - 1, 0)))] + table_specs + [row_tile],
        out_specs=row_tile, out_shape=jax.ShapeDtypeStruct((t, D_MODEL), F32),
        scratch_shapes=[pltpu.VMEM((D_MODEL, tt), F32)] + [pltpu.VMEM((ne, tt), BF16)] * 3,
        compiler_params=_params("parallel", "arbitrary"), name="peer",
    )(xn2, w["u"], w["vt"], *table_args, h)


def _proj_sample_kernel(x_ref, g1_ref, win_ref, bones_ref, qgn_ref, kgn_ref, gvg_ref, wsc_ref, bsc_ref, gog_ref,
                        k_ref, v_ref, gv_ref, gm_ref, q_ref, qt_ref):
    xn = (_rms(x_ref[...]) * g1_ref[...]).astype(BF16)
    hn = _dot(xn, win_ref[...])

    def head_norm(z, g_ref):
        hi, lo = _split(z * z)
        ss = _dot(hi, bones_ref[...]) + _dot(lo, bones_ref[...])
        return z * lax.rsqrt(ss * (1.0 / HEAD_DIM) + EPS) * g_ref[...]

    qn = head_norm(hn[:, 0:ATTN_W], qgn_ref)
    kn = head_norm(hn[:, ATTN_W:2 * ATTN_W], kgn_ref)
    k_ref[...] = kn
    v_ref[...] = hn[:, 2 * ATTN_W:3 * ATTN_W]
    gu = _gelu(hn[:, 3 * ATTN_W:3 * ATTN_W + GMLP_W])
    gvn = _gmlp_norm(_gelu(hn[:, 3 * ATTN_W + GMLP_W:]), bones_ref, gvg_ref)
    gv_ref[...] = gvn
    gm_ref[...] = (_rms(gu * (wsc_ref[...] * gvn + bsc_ref[...])) * gog_ref[...]).astype(BF16)
    qs = qn * QK_SCALE
    q_ref[...] = qs
    qt_ref[...] = qs.T


def _proj_sample(x, w):
    nb = x.shape[0]
    out_shape = (jax.ShapeDtypeStruct((nb, ATTN_W), F32), jax.ShapeDtypeStruct((nb, ATTN_W), F32),
                 jax.ShapeDtypeStruct((nb, GMLP_W), F32), jax.ShapeDtypeStruct((nb, GMLP_W), BF16),
                 jax.ShapeDtypeStruct((nb, ATTN_W), F32), jax.ShapeDtypeStruct((ATTN_W, nb), F32))
    return pl.pallas_call(
        _proj_sample_kernel, out_shape=out_shape, compiler_params=_params(), name="proj_sample",
    )(x, w["g1"], w["win"], w["bones"], w["qg_nat"], w["kg_nat"], w["gvg"], w["ws_one"], w["b_one"], w["gog"])


_PAGES_PER_STEP = 32
_PAGES_PER_BLOCK = MOBA_BLOCK // PAGE
_PAGE_BLOCK = (None, N_HEADS, HEAD_DIM, PAGE)


def _sample_scores_kernel(pt_ref, qt_ref, q_ref, knew_ref, *refs, n_steps, past_len):
    k_refs = refs[:_PAGES_PER_STEP]
    p_ref, pself_ref, sel_ref, qcol_s, sc_s, gate_s = refs[_PAGES_PER_STEP:]
    d = pl.program_id(0)
    c = pl.program_id(1)
    n_seq = qt_ref.shape[1]
    lane = _iota((N_HEADS, LANES), 1)
    blocks_per_step = _PAGES_PER_STEP // _PAGES_PER_BLOCK

    @pl.when(c == 0)
    def _init():
        pick = jnp.where(_iota((n_seq, LANES), 0) == d, 1.0, 0.0)
        qcol_s[...] = _dot3(qt_ref[...], pick)
        gate_s[...] = jnp.zeros(gate_s.shape, F32)

    gates = gate_s[...]
    for b in range(blocks_per_step):
        blk = None
        for g in range(_PAGES_PER_BLOCK):
            r = b * _PAGES_PER_BLOCK + g
            rows = [jnp.sum(k_refs[r][hd] * qcol_s[HEAD_DIM * hd:HEAD_DIM * (hd + 1), :], axis=0, keepdims=True)
                    for hd in range(N_HEADS)]
            s_page = jnp.concatenate(rows, axis=0)
            sc_s[:, pl.ds(pl.multiple_of((c * _PAGES_PER_STEP + r) * PAGE, PAGE), PAGE)] = s_page
            blk = s_page if blk is None else blk + s_page
        gate = jnp.sum(blk, axis=-1, keepdims=True) * (1.0 / MOBA_BLOCK)
        gates = jnp.where(lane == c * blocks_per_step + b, gate, gates)
    gate_s[...] = gates

    @pl.when(c == n_steps - 1)
    def _select():
        n_blocks = n_steps * blocks_per_step
        lane_f = lane.astype(F32)
        cur = jnp.where(lane < n_blocks, gates, -jnp.inf)
        sel = jnp.zeros_like(cur)
        picked = jnp.zeros_like(cur)
        for r in range(MOBA_TOPK):
            m = jnp.max(cur, axis=-1, keepdims=True)
            first = jnp.min(jnp.where(cur == m, lane_f, float(LANES)), axis=-1, keepdims=True)
            hit = lane_f == first
            sel = jnp.where(hit, 1.0, sel)
            picked = jnp.where(lane == r, first, picked)
            cur = jnp.where(hit, -jnp.inf, cur)
        sel_ref[...] = picked

        n_pos = n_blocks * MOBA_BLOCK
        chosen = jnp.concatenate([jnp.broadcast_to(sel[:, n:n + 1], (N_HEADS, MOBA_BLOCK)) for n in range(n_blocks)],
                                 axis=1)
        slopes = jnp.exp2(-(_iota((N_HEADS, 1), 0) + 1).astype(F32))
        distance = float(past_len) - _iota((1, n_pos), 1).astype(F32)
        logit = jnp.where(chosen > 0.0, sc_s[...] - slopes * distance, NEG)
        self_logit = jnp.sum(q_ref[...] * knew_ref[...], axis=-1, keepdims=True)
        top = jnp.maximum(jnp.max(logit, axis=-1, keepdims=True), self_logit)
        p = jnp.exp(logit - top)
        p_self = jnp.exp(self_logit - top)
        inv = 1.0 / (jnp.sum(p, axis=-1, keepdims=True) + p_self)
        p_ref[...] = p * inv
        pself_ref[...] = jnp.broadcast_to(p_self * inv, pself_ref.shape)


def _sample_scores(page_table, cache_kt, qt, q, k_new, past_len):
    nb, n_pages = page_table.shape
    n_steps = n_pages // _PAGES_PER_STEP
    n_blocks = n_pages // _PAGES_PER_BLOCK
    assert n_blocks <= LANES
    page_spec = lambda r: pl.BlockSpec(
        _PAGE_BLOCK, lambda d, c, pt, r=r: (pt[d, c * _PAGES_PER_STEP + r], 0, 0, 0))
    per_seq = lambda shape: pl.BlockSpec((None,) + shape, lambda d, c, pt: (d,) + (0,) * len(shape))
    grid_spec = pltpu.PrefetchScalarGridSpec(
        num_scalar_prefetch=1, grid=(nb, n_steps),
        in_specs=[pl.BlockSpec((ATTN_W, nb), lambda d, c, pt: (0, 0)), per_seq((N_HEADS, HEAD_DIM)),
                  per_seq((N_HEADS, HEAD_DIM))] + [page_spec(r) for r in range(_PAGES_PER_STEP)],
        out_specs=(per_seq((N_HEADS, n_pages * PAGE)), per_seq((N_HEADS, HEAD_DIM)), per_seq((N_HEADS, LANES))),
        scratch_shapes=[pltpu.VMEM((ATTN_W, LANES), F32), pltpu.VMEM((N_HEADS, n_pages * PAGE), F32),
                        pltpu.VMEM((N_HEADS, LANES), F32)])
    out_shape = (jax.ShapeDtypeStruct((nb, N_HEADS, n_pages * PAGE), F32),
                 jax.ShapeDtypeStruct((nb, N_HEADS, HEAD_DIM), F32), jax.ShapeDtypeStruct((nb, N_HEADS, LANES), F32))
    return pl.pallas_call(
        functools.partial(_sample_scores_kernel, n_steps=n_steps, past_len=past_len),
        grid_spec=grid_spec, out_shape=out_shape,
        compiler_params=_params("parallel", "arbitrary"), name="sample_scores",
    )(page_table, qt, q, k_new, *([cache_kt] * _PAGES_PER_STEP))


_SLABS_PER_HEAD = MOBA_TOPK * _PAGES_PER_BLOCK


def _sample_values_kernel(phys_ref, logical_ref, p_ref, pself_ref, vnew_ref, *refs):
    n_slabs = N_HEADS * _SLABS_PER_HEAD
    v_refs = refs[:n_slabs]
    out_ref = refs[n_slabs]
    d = pl.program_id(0)
    cols = []
    for hd in range(N_HEADS):
        acc = None
        for k in range(_SLABS_PER_HEAD):
            i = hd * _SLABS_PER_HEAD + k
            start = pl.multiple_of(logical_ref[d, i] * PAGE, PAGE)
            term = v_refs[i][...] * p_ref[hd:hd + 1, pl.ds(start, PAGE)]
            acc = term if acc is None else acc + term
        cols.append(jnp.sum(acc, axis=-1, keepdims=True))
    ctx_t = jnp.concatenate(cols + [jnp.zeros((HEAD_DIM, LANES - N_HEADS), F32)], axis=1)
    out_ref[...] = ctx_t.T[0:N_HEADS, :] + pself_ref[...] * vnew_ref[...]


def _sample_values(slab_phys, slab_logical, cache_vt, p, pself, v_new):
    nb, n_slabs = slab_phys.shape
    assert n_slabs == N_HEADS * _SLABS_PER_HEAD
    slab_spec = lambda i: pl.BlockSpec(
        (None, None, HEAD_DIM, PAGE), lambda d, ph, lg, i=i: (ph[d, i], i // _SLABS_PER_HEAD, 0, 0))
    per_seq = lambda shape: pl.BlockSpec((None,) + shape, lambda d, ph, lg: (d,) + (0,) * len(shape))
    head_rows = per_seq((N_HEADS, HEAD_DIM))
    grid_spec = pltpu.PrefetchScalarGridSpec(
        num_scalar_prefetch=2, grid=(nb,),
        in_specs=[per_seq((N_HEADS, p.shape[2])), head_rows, head_rows] + [slab_spec(i) for i in range(n_slabs)],
        out_specs=head_rows)
    return pl.pallas_call(
        _sample_values_kernel, grid_spec=grid_spec, out_shape=jax.ShapeDtypeStruct((nb, N_HEADS, HEAD_DIM), F32),
        compiler_params=_params("parallel"), name="sample_values",
    )(slab_phys, slab_logical, p, pself, v_new, *([cache_vt] * n_slabs))


def _prepare(norm1_g, w_in, q_norm_g, k_norm_g, gmlp_v_norm_g, gmlp_ws, gmlp_b, attn_out_norm_g, gmlp_out_norm_g,
             w_out, norm2_g, peer_wq, peer_keys, peer_u, peer_v):
    wq, wk, wv, wgu, wgv = jnp.split(w_in, [ATTN_W, 2 * ATTN_W, 3 * ATTN_W, 3 * ATTN_W + GMLP_W], axis=1)
    wk_pad = jnp.pad(wk.reshape(D_MODEL, N_HEADS, HEAD_DIM), ((0, 0), (0, 0), (0, LANES - HEAD_DIM)))
    feat = np.arange(ATTN_W)
    w = {
        "g1": norm1_g.reshape(1, D_MODEL),
        "win": w_in.astype(BF16),
        "wnat": jnp.concatenate([wk_pad.reshape(D_MODEL, N_HEADS * LANES), wk, wv, wgu, wgv], axis=1).astype(BF16),
        "wt": jnp.concatenate([wq, wv], axis=1).T.astype(BF16),
        "qg_col": q_norm_g.reshape(HEAD_DIM, 1),
        "kg_pad": jnp.pad(k_norm_g, (0, LANES - HEAD_DIM)).reshape(1, LANES),
        "qg_nat": jnp.tile(q_norm_g, N_HEADS).reshape(1, ATTN_W),
        "kg_nat": jnp.tile(k_norm_g, N_HEADS).reshape(1, ATTN_W),
        "gvg": gmlp_v_norm_g.reshape(1, GMLP_W),
        "bones": jnp.asarray((feat[:, None] // GROUP_DIM) == (feat[None, :] // GROUP_DIM), BF16),
        "wcat": jnp.transpose(gmlp_ws, (1, 0, 2)).reshape(GMLP_CHUNK, N_GROUPS * GMLP_CHUNK),
        "gbias": jnp.repeat(gmlp_b.T, GROUP_DIM, axis=1),
        "ws_one": jnp.repeat(gmlp_ws[:, 0, 0], GROUP_DIM).reshape(1, GMLP_W),
        "b_one": jnp.repeat(gmlp_b[:, 0], GROUP_DIM).reshape(1, GMLP_W),
        "gog": gmlp_out_norm_g.reshape(1, GMLP_W),
        "aog": attn_out_norm_g.reshape(1, ATTN_W),
        "wo_attn": w_out[:ATTN_W].astype(BF16),
        "wo_gmlp": w_out[ATTN_W:].astype(BF16),
        "g2": norm2_g.reshape(1, D_MODEL),
        "wqt": peer_wq.T.astype(BF16),
        "keys": peer_keys.reshape(2 * PEER_HEADS, PEER_KEYS, PEER_HALF),
        "u": peer_u.astype(BF16),
        "vt": peer_v.T.astype(BF16),
        "cand_idx": _cand_index_table(),
    }
    return w


def _sample_attention(page_table, cache_k, cache_v, q, qt, k_new, v_new, past_len):
    nb = q.shape[0]
    heads = lambda a: a.reshape(nb, N_HEADS, HEAD_DIM)
    as_stored = lambda cache: jnp.transpose(cache, (0, 2, 3, 1))
    p, pself, picked = _sample_scores(page_table, as_stored(cache_k), qt, heads(q), heads(k_new), past_len)
    blocks = picked[:, :, :MOBA_TOPK].astype(jnp.int32)
    logical = (blocks[..., None] * _PAGES_PER_BLOCK + jnp.arange(_PAGES_PER_BLOCK, dtype=jnp.int32)).reshape(nb, -1)
    phys = jnp.take_along_axis(page_table, logical, axis=1)
    ctx = _sample_values(phys, logical.astype(jnp.int32), as_stored(cache_v), p, pself, heads(v_new))
    return ctx.reshape(nb, ATTN_W)


def _layer_tail(x, attn, gm, w, *, route_tile, peer_tile, attn_transposed):
    h, xn2, m1, c1, r2, e2 = _route(x, attn, gm, w, tt=route_tile, attn_transposed=attn_transposed)
    return _peer(xn2, h, m1, c1, r2, e2, w, tt=peer_tile)


def kernel(x_prompt, x_sample, cache_k, cache_v, page_table, norm1_g, w_in, q_norm_g, k_norm_g, gmlp_v_norm_g,
           gmlp_ws, gmlp_b, attn_out_norm_g, gmlp_out_norm_g, w_out, norm2_g, peer_wq, peer_keys, peer_u, peer_v):
    w = _prepare(norm1_g, w_in, q_norm_g, k_norm_g, gmlp_v_norm_g, gmlp_ws, gmlp_b, attn_out_norm_g,
                 gmlp_out_norm_g, w_out, norm2_g, peer_wq, peer_keys, peer_u, peer_v)
    b, t, _ = x_prompt.shape
    assert b == 1
    xp = x_prompt.reshape(t, D_MODEL)
    k_p, v_p, kaug, qaug, vt, gm_p, gv_last = _proj_prompt(xp, w)
    attn_t = _moba_prompt(kaug, qaug, vt)
    y_p = _layer_tail(xp, attn_t, gm_p, w, route_tile=512, peer_tile=512, attn_transposed=True)

    nb, ds, _ = x_sample.shape
    assert ds == 1
    xs = x_sample.reshape(nb, D_MODEL)
    past_len = page_table.shape[1] * PAGE
    assert past_len % MOBA_BLOCK == 0
    k_s, v_s, gv_s, gm_s, q_s, qt_s = _proj_sample(xs, w)
    attn_s = _sample_attention(page_table, cache_k, cache_v, q_s, qt_s, k_s, v_s, past_len)
    y_s = _layer_tail(xs, attn_s, gm_s, w, route_tile=nb, peer_tile=nb, attn_transposed=False)

    return (y_p.reshape(1, t, D_MODEL), y_s.reshape(nb, 1, D_MODEL),
            k_p.reshape(1, t, N_HEADS, HEAD_DIM), v_p.reshape(1, t, N_HEADS, HEAD_DIM),
            gv_last.reshape(1, GMLP_CHUNK, GMLP_W),
            k_s.reshape(nb, 1, N_HEADS, HEAD_DIM), v_s.reshape(nb, 1, N_HEADS, HEAD_DIM),
            gv_s.reshape(nb, 1, GMLP_W))
```

```python
import functools
import math

import jax
import jax.numpy as jnp
import numpy as np
from jax import lax
from jax.experimental import pallas as pl
from jax.experimental.pallas import tpu as pltpu

F32 = jnp.float32
BF16 = jnp.bfloat16

D_MODEL = 1024
N_HEADS = 8
HEAD_DIM = 64
ATTN_W = N_HEADS * HEAD_DIM
GMLP_W = 512
N_GROUPS = 8
GROUP_DIM = GMLP_W // N_GROUPS
GMLP_CHUNK = 128
MOBA_BLOCK = 256
MOBA_TOPK = 3
PAGE = 128
PEER_HEADS = 8
PEER_KEYS = 128
PEER_HALF = 128
PEER_TOPK = 16
N_EXPERTS = PEER_KEYS * PEER_KEYS
EPS = 1e-6
NEG = -1e30
QK_SCALE = HEAD_DIM ** -0.5
GELU_C = math.sqrt(2.0 / math.pi)


def _bf16_pieces(x, n):
    out, rest = [], np.float32(x)
    for _ in range(n):
        piece = np.float32(rest.astype(jnp.bfloat16))
        out.append(float(piece))
        rest = np.float32(rest - piece)
    return tuple(out)


LOG2E = float(np.float32(math.log2(math.e)))
_LOG2E_PIECES = _bf16_pieces(LOG2E, 3)
_GELU_A = _bf16_pieces(-2.0 * GELU_C * 0.044715, 2)
_GELU_B = _bf16_pieces(-2.0 * GELU_C, 2)

LANES = 128
VMEM_LIMIT = 56 * 1024 * 1024

_CAND_ROWS = tuple((a, PEER_TOPK // (a + 1)) for a in range(8))


def _dot(a, b):
    return jnp.dot(a, b, preferred_element_type=F32)


def _dot_nt(a, b):
    return lax.dot_general(a, b, (((1,), (1,)), ((), ())), preferred_element_type=F32)


def _split(a):
    hi = a.astype(BF16)
    lo = (a - hi.astype(F32)).astype(BF16)
    return hi, lo


def _dot3(a, b):
    ah, al = _split(a)
    bh, bl = _split(b)
    return _dot(ah, bh) + (_dot(ah, bl) + _dot(al, bh))


def _rms(x, axis=-1):
    return x * lax.rsqrt(jnp.mean(x * x, axis=axis, keepdims=True) + EPS)


def _gelu(x):
    return x * (0.5 * (1.0 + jnp.tanh(GELU_C * (x + 0.044715 * (x * x * x)))))


def _gelu_sigmoid(x):
    t = x * x
    if x.dtype == BF16:
        poly = (_GELU_A[0] * t + _GELU_B[0]) + (_GELU_A[1] * t + _GELU_B[1])
    else:
        poly = (-2.0 * GELU_C * 0.044715) * t - 2.0 * GELU_C
    return x / (1.0 + jnp.exp(x * poly))


def _iota(shape, dim, dtype=jnp.int32):
    return lax.broadcasted_iota(dtype, shape, dim)


def _params(*sem):
    return pltpu.CompilerParams(dimension_semantics=sem, vmem_limit_bytes=VMEM_LIMIT)


def _full(shape):
    nd = len(shape)
    return pl.BlockSpec(shape, lambda *_: (0,) * nd)


_KP0, _KN0, _V0, _GU0, _GV0, _NAT_COLS = 0, 1024, 1536, 2048, 2560, 3072
_VT_ROWS = HEAD_DIM + 16


def _gmlp_norm(gvr, bones_ref, gvg_ref):
    hi, lo = _split(gvr * gvr)
    ss = _dot(hi, bones_ref[...]) + _dot(lo, bones_ref[...])
    return gvr * lax.rsqrt(ss * (1.0 / GROUP_DIM) + EPS) * gvg_ref[...]


def _proj_prompt_kernel(x_ref, g1_ref, wnat_ref, wt_ref, qg_ref, kgp_ref, kgn_ref, gvg_ref, bones_ref,
                        wcat_ref, gbias_ref, gog_ref,
                        k_ref, v_ref, kaug_ref, qaug_ref, vt_ref, gm_ref, gvl_ref,
                        kmean_s, wtril_s):
    i = pl.program_id(0)
    tm = x_ref.shape[0]
    nblk = kmean_s.shape[1]

    @pl.when(i == 0)
    def _init():
        row = _iota((GMLP_CHUNK, N_GROUPS * GMLP_CHUNK), 0)
        col = _iota((GMLP_CHUNK, N_GROUPS * GMLP_CHUNK), 1) & (GMLP_CHUNK - 1)
        wtril_s[...] = jnp.where(col <= row, wcat_ref[...], 0.0).astype(BF16)
        kmean_s[...] = jnp.zeros(kmean_s.shape, F32)

    xn = (_rms(x_ref[...]) * g1_ref[...]).astype(BF16)
    hn = _dot(xn, wnat_ref[...])
    ht = _dot_nt(wt_ref[...], xn)

    lane = _iota((tm, LANES), 1)
    rowf = _iota((tm, LANES), 0).astype(F32)
    blk_row = _iota((nblk, tm), 0)
    blk_rowf = blk_row.astype(F32)
    piece_row = _iota((LANES, tm), 0)
    i_f = i.astype(F32)
    k_hi = jnp.where(lane < 3, rowf, jnp.where(lane < 6, float(MOBA_BLOCK) * i_f, 0.0))
    ones_rows = jnp.where(_iota((_VT_ROWS - HEAD_DIM, tm), 0) == 0, 1.0, 0.0)

    rs_heads = []
    for h in range(N_HEADS):
        kp = hn[:, _KP0 + LANES * h:_KP0 + LANES * (h + 1)]
        rs = lax.rsqrt(jnp.sum(kp * kp, axis=-1, keepdims=True) * (1.0 / HEAD_DIM) + EPS)
        rs_heads.append(rs)
        kn = kp * rs * kgp_ref[...]
        kmean_s[h, pl.ds(i, 1), :] = jnp.mean(kn, axis=0, keepdims=True)

        qt = ht[HEAD_DIM * h:HEAD_DIM * (h + 1), :]
        qn = qt * lax.rsqrt(jnp.sum(qt * qt, axis=0, keepdims=True) * (1.0 / HEAD_DIM) + EPS) * qg_ref[...]
        gate = _dot3(kmean_s[h], jnp.concatenate([qn, jnp.zeros_like(qn)], axis=0))
        cur = jnp.where(blk_row < i, gate, NEG)
        sel = jnp.zeros_like(cur)
        for _ in range(MOBA_TOPK):
            m = jnp.max(cur, axis=0, keepdims=True)
            first = jnp.min(jnp.where(cur == m, blk_rowf, float(nblk)), axis=0, keepdims=True)
            hit = blk_rowf == first
            sel = jnp.where(hit, 1.0, sel)
            cur = jnp.where(hit, -jnp.inf, cur)
        keep = jnp.where(blk_row < i, sel, jnp.where(blk_row == i, 1.0, 0.0))
        sel_bias = jnp.where(keep > 0.0, 0.0, NEG)
        pieces = [qn * (QK_SCALE * LOG2E), sel_bias]
        if nblk < HEAD_DIM:
            pieces.append(jnp.zeros((HEAD_DIM - nblk, tm), F32))
        slope = 2.0 ** -(h + 1)
        slope_rows = jnp.zeros((LANES, tm), F32)
        for r, piece in enumerate(_LOG2E_PIECES * 2):
            slope_rows = jnp.where(piece_row == r, slope * piece, slope_rows)
        qaug_ref[h] = jnp.concatenate(pieces + [slope_rows], axis=0).astype(BF16)

        k_lo = jnp.where(lane == HEAD_DIM + i, 1.0, kn)
        kaug_ref[h] = jnp.concatenate([k_lo, k_hi], axis=1).astype(BF16)
        vt_ref[h] = jnp.concatenate([ht[ATTN_W + HEAD_DIM * h:ATTN_W + HEAD_DIM * (h + 1), :], ones_rows],
                                    axis=0).astype(BF16)

    cols = []
    for c in range(ATTN_W // LANES):
        sc = jnp.where(lane < HEAD_DIM, rs_heads[2 * c], rs_heads[2 * c + 1])
        cols.append(hn[:, _KN0 + LANES * c:_KN0 + LANES * (c + 1)] * sc)
    k_ref[...] = jnp.concatenate(cols, axis=1) * kgn_ref[...]
    v_ref[...] = hn[:, _V0:_V0 + ATTN_W]

    gu = _gelu(hn[:, _GU0:_GU0 + GMLP_W])
    gvn = _gmlp_norm(_gelu(hn[:, _GV0:_GV0 + GMLP_W]), bones_ref, gvg_ref)
    gvl_ref[...] = gvn[tm - GMLP_CHUNK:, :]
    group_of_lane = _iota((1, GMLP_W), 1) >> 6
    outs = []
    for c in range(tm // GMLP_CHUNK):
        gc = gvn[GMLP_CHUNK * c:GMLP_CHUNK * (c + 1), :]
        stacked = jnp.concatenate([jnp.where(group_of_lane == g, gc, 0.0) for g in range(N_GROUPS)], axis=0)
        mixed = _dot(wtril_s[...], stacked.astype(BF16)) + gbias_ref[...]
        outs.append(gu[GMLP_CHUNK * c:GMLP_CHUNK * (c + 1), :] * mixed)
    gm_ref[...] = (_rms(jnp.concatenate(outs, axis=0)) * gog_ref[...]).astype(BF16)


def _proj_prompt(x, w, tm=MOBA_BLOCK):
    t = x.shape[0]
    nblk = t // tm
    assert nblk <= HEAD_DIM
    row_tile = lambda width: pl.BlockSpec((tm, width), lambda i: (i, 0))
    in_specs = [row_tile(D_MODEL), _full((1, D_MODEL)), _full((D_MODEL, _NAT_COLS)), _full((D_MODEL, D_MODEL)),
                _full((HEAD_DIM, 1)), _full((1, LANES)), _full((1, ATTN_W)), _full((1, GMLP_W)),
                _full((GMLP_W, GMLP_W)), _full((GMLP_CHUNK, N_GROUPS * GMLP_CHUNK)), _full((GMLP_CHUNK, GMLP_W)),
                _full((1, GMLP_W))]
    out_shape = (jax.ShapeDtypeStruct((t, ATTN_W), F32), jax.ShapeDtypeStruct((t, ATTN_W), F32),
                 jax.ShapeDtypeStruct((N_HEADS, t, 2 * LANES), BF16),
                 jax.ShapeDtypeStruct((N_HEADS, 2 * LANES, t), BF16),
                 jax.ShapeDtypeStruct((N_HEADS, _VT_ROWS, t), BF16),
                 jax.ShapeDtypeStruct((t, GMLP_W), BF16), jax.ShapeDtypeStruct((GMLP_CHUNK, GMLP_W), F32))
    out_specs = (row_tile(ATTN_W), row_tile(ATTN_W),
                 pl.BlockSpec((N_HEADS, tm, 2 * LANES), lambda i: (0, i, 0)),
                 pl.BlockSpec((N_HEADS, 2 * LANES, tm), lambda i: (0, 0, i)),
                 pl.BlockSpec((N_HEADS, _VT_ROWS, tm), lambda i: (0, 0, i)),
                 row_tile(GMLP_W), _full((GMLP_CHUNK, GMLP_W)))
    return pl.pallas_call(
        _proj_prompt_kernel, grid=(nblk,), in_specs=in_specs, out_specs=out_specs, out_shape=out_shape,
        scratch_shapes=[pltpu.VMEM((N_HEADS, nblk, LANES), F32),
                        pltpu.VMEM((GMLP_CHUNK, N_GROUPS * GMLP_CHUNK), BF16)],
        compiler_params=_params("arbitrary"), name="proj_prompt",
    )(x, w["g1"], w["wnat"], w["wt"], w["qg_col"], w["kg_pad"], w["kg_nat"], w["gvg"], w["bones"],
      w["wcat"], w["gbias"], w["gog"])


_MOBA_HEADS_PER_STEP = 4


def _moba_kernel(kaug_ref, qaug_ref, vt_ref, out_ref, s_even, s_odd):
    j = pl.program_id(1)
    tq = qaug_ref.shape[2]
    heads = range(_MOBA_HEADS_PER_STEP)

    def block_start(n):
        return pl.multiple_of(n * MOBA_BLOCK, MOBA_BLOCK)

    def produce(buf, n):
        for hh in heads:
            buf[hh] = _dot(kaug_ref[hh, pl.ds(block_start(n), MOBA_BLOCK), :], qaug_ref[hh])

    def fold(state, s, hh, n):
        m, acc = state
        m_new = jnp.maximum(m, jnp.max(s, axis=0, keepdims=True))
        alpha = jnp.exp2(m - m_new)
        p = jnp.exp2(s - m_new).astype(BF16)
        acc = acc * alpha + _dot(vt_ref[hh, :, pl.ds(block_start(n), MOBA_BLOCK)], p)
        return m_new, acc

    def body(i, states):
        produce(s_odd, 2 * i + 1)
        states = tuple(fold(states[hh], s_even[hh], hh, 2 * i) for hh in heads)
        produce(s_even, 2 * i + 2)
        return tuple(fold(states[hh], s_odd[hh], hh, 2 * i + 1) for hh in heads)

    init = tuple((jnp.full((1, tq), -jnp.inf, F32), jnp.zeros((_VT_ROWS, tq), F32)) for _ in heads)
    produce(s_even, 0)
    n_pairs = j // 2
    def quad(k, st):
        for r in range(4):
            st = body(4 * k + r, st)
        return st

    states = lax.fori_loop(0, n_pairs // 4, quad, init)
    states = lax.fori_loop((n_pairs // 4) * 4, n_pairs, body, states)

    causal = _iota((MOBA_BLOCK, tq), 0) <= _iota((MOBA_BLOCK, tq), 1)
    j_odd = (j & 1) == 1
    produce(s_odd, j)
    for hh in heads:
        first = jnp.where(j_odd, s_even[hh], jnp.where(causal, s_even[hh], NEG))
        state = fold(states[hh], first, hh, 2 * (j // 2))
        second = jnp.where(j_odd, jnp.where(causal, s_odd[hh], NEG), NEG)
        _, acc = fold(state, second, hh, j)
        out_ref[HEAD_DIM * hh:HEAD_DIM * (hh + 1), :] = acc[:HEAD_DIM, :] / acc[HEAD_DIM:HEAD_DIM + 1, :]


def _moba_prompt(kaug, qaug, vt):
    t = kaug.shape[1]
    tq = MOBA_BLOCK
    hs = _MOBA_HEADS_PER_STEP
    once = pl.Buffered(1)
    return pl.pallas_call(
        _moba_kernel, grid=(N_HEADS // hs, t // tq),
        in_specs=[pl.BlockSpec((hs, t, 2 * LANES), lambda g, j: (g, 0, 0), pipeline_mode=once),
                  pl.BlockSpec((hs, 2 * LANES, tq), lambda g, j: (g, 0, j)),
                  pl.BlockSpec((hs, _VT_ROWS, t), lambda g, j: (g, 0, 0), pipeline_mode=once)],
        out_specs=pl.BlockSpec((hs * HEAD_DIM, tq), lambda g, j: (g, j)),
        out_shape=jax.ShapeDtypeStruct((ATTN_W, t), F32),
        scratch_shapes=[pltpu.VMEM((hs, MOBA_BLOCK, tq), F32), pltpu.VMEM((hs, MOBA_BLOCK, tq), F32)],
        compiler_params=_params("parallel", "arbitrary"), name="moba_prompt",
    )(kaug, qaug, vt)


def _cand_index_table():
    rows = []
    for a, nb in _CAND_ROWS:
        for c0 in range(0, max(nb, 8), 8):
            rows.append([a * PEER_TOPK + c0 + b for b in range(8)])
    rows.append([(8 + a) * PEER_TOPK for a in range(8)])
    flat = np.asarray(rows, np.float32).reshape(-1, 1)
    return jnp.asarray(np.broadcast_to(flat, (flat.shape[0], LANES)).copy())


_N_CAND = 8 * (sum(max(nb, 8) // 8 for _, nb in _CAND_ROWS) + 1)


def _extract_top(cur, idx, n, on_hit):
    big = float(1 << 20)
    for r in range(n):
        m = jnp.max(cur, axis=0, keepdims=True)
        first = jnp.min(jnp.where(cur == m, idx, big), axis=0, keepdims=True)
        hit = idx == first
        cur = jnp.where(hit, -jnp.inf, cur)
        on_hit(r, m, hit)
    return cur


def _route_kernel(x_ref, attn_ref, gm_ref, aog_ref, woa_ref, wog_ref, g2_ref, wqt_ref, keys_ref, cidx_ref,
                  h_ref, xn2_ref, m1_ref, c1_ref, r2_ref, e2_ref,
                  s_s, rank_s, sv_s, na_s, chosen_s, *, attn_transposed):
    tt = x_ref.shape[0]
    at = attn_ref[...]
    if attn_transposed:
        at = at.T
    an = (_rms(at) * aog_ref[...]).astype(BF16)
    h = x_ref[...] + _dot(an, woa_ref[...]) + _dot(gm_ref[...], wog_ref[...])
    h_ref[...] = h
    xn2 = (_rms(h) * g2_ref[...]).astype(BF16)
    xn2_ref[...] = xn2
    qt = _dot_nt(wqt_ref[...], xn2)
    for hp in range(2 * PEER_HEADS):
        s_s[hp] = _dot3(keys_ref[hp], qt[PEER_HALF * hp:PEER_HALF * (hp + 1), :])

    key_idx = _iota((PEER_KEYS, LANES), 0).astype(F32)
    row8 = _iota((8, LANES), 0)

    def rank_distinct(hp, lanes):
        cur = s_s[hp, :, lanes]
        rank = jnp.full((PEER_KEYS, LANES), float(PEER_TOPK), F32)
        for r in range(PEER_TOPK):
            m = jnp.max(cur, axis=0, keepdims=True)
            hit = cur == m
            cur = jnp.where(hit, -jnp.inf, cur)
            rank = jnp.where(hit, float(r), rank)
            sv_s[hp, r:r + 1, lanes] = m
        rank_s[hp, :, lanes] = rank
        return jnp.sum(jnp.where(rank < float(PEER_TOPK), 1.0, 0.0), axis=0, keepdims=True)

    def rank_exact(hp, lanes):
        rank = [jnp.full((PEER_KEYS, LANES), float(PEER_TOPK), F32)]

        def on_hit(r, m, hit):
            rank[0] = jnp.where(hit, float(r), rank[0])
            sv_s[hp, r:r + 1, lanes] = m

        _extract_top(s_s[hp, :, lanes], key_idx, PEER_TOPK, on_hit)
        rank_s[hp, :, lanes] = rank[0]

    def per_head(hd, carry):
        ranked = []
        for sl in range(tt // LANES):
            lanes = slice(LANES * sl, LANES * (sl + 1))
            for hp in (2 * hd, 2 * hd + 1):
                ranked.append((hp, lanes, rank_distinct(hp, lanes)))
        off_count = functools.reduce(jnp.maximum, [jnp.abs(n - float(PEER_TOPK)) for _, _, n in ranked])

        @pl.when(jnp.max(off_count) > 0.0)
        def _redo_ranks():
            for hp, lanes, _ in ranked:
                rank_exact(hp, lanes)
        def candidates(lanes):
            sv1 = sv_s[2 * hd, :, lanes]
            sv2 = sv_s[2 * hd + 1, :, lanes]
            pieces = []
            for a, nb in _CAND_ROWS:
                for c0 in range(0, max(nb, 8), 8):
                    piece = sv1[a:a + 1, :] + sv2[c0:c0 + 8, :]
                    pieces.append(piece if nb >= 8 else jnp.where(row8 < nb, piece, -jnp.inf))
            pieces.append(sv1[8:16, :] + sv2[0:1, :])
            return sv1, sv2, jnp.concatenate(pieces, axis=0)

        n_chosen = []
        for sl in range(tt // LANES):
            _, _, cur = candidates(slice(LANES * sl, LANES * (sl + 1)))
            chosen = jnp.zeros_like(cur)
            for _ in range(PEER_TOPK):
                hit = cur == jnp.max(cur, axis=0, keepdims=True)
                cur = jnp.where(hit, -jnp.inf, cur)
                chosen = jnp.where(hit, 1.0, chosen)
            chosen_s[sl] = chosen
            n_chosen.append(jnp.sum(chosen, axis=0, keepdims=True))
        off_count = functools.reduce(jnp.maximum, [jnp.abs(n - float(PEER_TOPK)) for n in n_chosen])

        @pl.when(jnp.max(off_count) > 0.0)
        def _redo_sums():
            for sl in range(tt // LANES):
                _, _, cand = candidates(slice(LANES * sl, LANES * (sl + 1)))
                chosen = [jnp.zeros_like(cand)]

                def on_hit(r, m, hit):
                    chosen[0] = jnp.where(hit, 1.0, chosen[0])

                _extract_top(cand, cidx_ref[...], PEER_TOPK, on_hit)
                chosen_s[sl] = chosen[0]

        for sl in range(tt // LANES):
            lanes = slice(LANES * sl, LANES * (sl + 1))
            sv1, sv2, cand = candidates(lanes)
            chosen = chosen_s[sl]
            top = sv1[0:1, :] + sv2[0:1, :]
            z = jnp.sum(jnp.where(chosen > 0.0, jnp.exp(cand - top), 0.0), axis=0, keepdims=True)
            row = 0
            for a, nb in _CAND_ROWS:
                nrows = max(nb, 8)
                na_s[sl, a:a + 1, :] = jnp.sum(chosen[row:row + nrows, :], axis=0, keepdims=True)
                row += nrows
            na_s[sl, 8:16, :] = chosen[row:row + 8, :]
            na = na_s[sl]
            rank1 = rank_s[2 * hd, :, lanes]
            m1 = jnp.zeros((PEER_KEYS, LANES), F32)
            for a in range(PEER_TOPK):
                m1 = jnp.where(rank1 == float(a), na[a:a + 1, :], m1)
            m1_ref[hd, :, lanes] = m1
            c1_ref[hd, :, lanes] = jnp.exp(s_s[2 * hd, :, lanes] - sv1[0:1, :]) / z
            r2_ref[hd, :, lanes] = rank_s[2 * hd + 1, :, lanes].astype(BF16)
            e2_ref[hd, :, lanes] = jnp.exp(s_s[2 * hd + 1, :, lanes] - sv2[0:1, :]).astype(BF16)
        return carry

    lax.fori_loop(0, PEER_HEADS, per_head, 0)


def _route(x, attn, gm, w, *, tt, attn_transposed):
    t = x.shape[0]
    row_tile = lambda width: pl.BlockSpec((tt, width), lambda i: (i, 0))
    attn_spec = pl.BlockSpec((ATTN_W, tt), lambda i: (0, i)) if attn_transposed else row_tile(ATTN_W)
    head_tile = pl.BlockSpec((PEER_HEADS, PEER_KEYS, tt), lambda i: (0, 0, i))
    in_specs = [row_tile(D_MODEL), attn_spec, row_tile(GMLP_W), _full((1, ATTN_W)),
                _full((ATTN_W, D_MODEL)), _full((GMLP_W, D_MODEL)), _full((1, D_MODEL)),
                _full((2 * PEER_HEADS * PEER_HALF, D_MODEL)), _full((2 * PEER_HEADS, PEER_KEYS, PEER_HALF)),
                _full((_N_CAND, LANES))]
    stat = lambda dt: jax.ShapeDtypeStruct((PEER_HEADS, PEER_KEYS, t), dt)
    out_shape = (jax.ShapeDtypeStruct((t, D_MODEL), F32), jax.ShapeDtypeStruct((t, D_MODEL), BF16),
                 stat(F32), stat(F32), stat(BF16), stat(BF16))
    out_specs = (row_tile(D_MODEL), row_tile(D_MODEL), head_tile, head_tile, head_tile, head_tile)
    return pl.pallas_call(
        functools.partial(_route_kernel, attn_transposed=attn_transposed),
        grid=(t // tt,), in_specs=in_specs, out_specs=out_specs, out_shape=out_shape,
        scratch_shapes=[pltpu.VMEM((2 * PEER_HEADS, PEER_KEYS, tt), F32),
                        pltpu.VMEM((2 * PEER_HEADS, PEER_KEYS, tt), F32),
                        pltpu.VMEM((2 * PEER_HEADS, PEER_TOPK, tt), F32),
                        pltpu.VMEM((tt // LANES, PEER_TOPK, LANES), F32),
                        pltpu.VMEM((tt // LANES, _N_CAND, LANES), F32)],
        compiler_params=_params("parallel"), name="route",
    )(x, attn, gm, w["aog"], w["wo_attn"], w["wo_gmlp"], w["g2"], w["wqt"], w["keys"], w["cand_idx"])


def _peer_kernel(xn2_ref, u_ref, vt_ref, *refs, keys_per_step, n_sub):
    tables = [refs[4 * s:4 * s + 4] for s in range(n_sub)]
    h_ref, y_ref, acc_s, act_even_s, act_odd_s, gated_s = refs[4 * n_sub:]
    c = pl.program_id(1)
    n_blocks = pl.num_programs(1) - 1
    sub = xn2_ref.shape[0] // n_sub

    @pl.when(c == 0)
    def _zero():
        acc_s[...] = jnp.zeros(acc_s.shape, F32)
        act_odd_s[...] = jnp.zeros(act_odd_s.shape, BF16)

    def step(act_write, act_read):
        first_key = jnp.maximum(c - 1, 0) * keys_per_step
        for s, (m1_ref, c1_ref, r2_ref, e2_ref) in enumerate(tables):
            cols = slice(sub * s, sub * (s + 1))
            act_write[:, cols] = _gelu_sigmoid(_dot_nt(u_ref[...], xn2_ref[cols, :]).astype(BF16))
            for ii in range(keys_per_step):
                i = first_key + ii
                rows = slice(PEER_KEYS * ii, PEER_KEYS * (ii + 1))
                g = None
                for hd in range(PEER_HEADS):
                    partners = m1_ref[hd, pl.ds(i, 1), :].astype(BF16)
                    weight = c1_ref[hd, pl.ds(i, 1), :].astype(BF16)
                    term = jnp.where(r2_ref[hd] < partners, e2_ref[hd], jnp.zeros((), BF16)) * weight
                    g = term if g is None else g + term
                gated_s[rows, cols] = g * act_read[rows, cols]
            acc_s[:, cols] += _dot(vt_ref[...], gated_s[:, cols])

    @pl.when((c & 1) == 0)
    def _even():
        step(act_even_s, act_odd_s)

    @pl.when((c & 1) == 1)
    def _odd():
        step(act_odd_s, act_even_s)

    @pl.when(c == n_blocks)
    def _finish():
        y_ref[...] = h_ref[...] + acc_s[...].T


_PEER_SUBTILE = 256


def _peer(xn2, h, m1, c1, r2, e2, w, *, tt, keys_per_step=8):
    t = xn2.shape[0]
    ne = keys_per_step * PEER_KEYS
    n_blocks = N_EXPERTS // ne
    sub = min(tt, _PEER_SUBTILE)
    n_sub = tt // sub
    row_tile = pl.BlockSpec((tt, D_MODEL), lambda j, c: (j, 0))
    table_specs, table_args = [], []
    for s in range(n_sub):
        spec = pl.BlockSpec((PEER_HEADS, PEER_KEYS, sub), lambda j, c, s=s: (0, 0, j * n_sub + s))
        table_specs += [spec] * 4
        table_args += [m1, c1, r2, e2]
    return pl.pallas_call(
        functools.partial(_peer_kernel, keys_per_step=keys_per_step, n_sub=n_sub),
        grid=(t // tt, n_blocks + 1),
        in_specs=[row_tile, pl.BlockSpec((ne, D_MODEL), lambda j, c: (jnp.minimum(c, n_blocks - 1), 0)),
                  pl.BlockSpec((D_MODEL, ne), lambda j, c: (0, jnp.maximum(c - 1, 0)))] + table_specs + [row_tile],
        out_specs=row_tile, out_shape=jax.ShapeDtypeStruct((t, D_MODEL), F32),
        scratch_shapes=[pltpu.VMEM((D_MODEL, tt), F32)] + [pltpu.VMEM((ne, tt), BF16)] * 3,
        compiler_params=_params("parallel", "arbitrary"), name="peer",
    )(xn2, w["u"], w["vt"], *table_args, h)


def _proj_sample_kernel(x_ref, g1_ref, win_ref, bones_ref, qgn_ref, kgn_ref, gvg_ref, wsc_ref, bsc_ref, gog_ref,
                        k_ref, v_ref, gv_ref, gm_ref, q_ref, qt_ref):
    xn = (_rms(x_ref[...]) * g1_ref[...]).astype(BF16)
    hn = _dot(xn, win_ref[...])

    def head_norm(z, g_ref):
        hi, lo = _split(z * z)
        ss = _dot(hi, bones_ref[...]) + _dot(lo, bones_ref[...])
        return z * lax.rsqrt(ss * (1.0 / HEAD_DIM) + EPS) * g_ref[...]

    qn = head_norm(hn[:, 0:ATTN_W], qgn_ref)
    kn = head_norm(hn[:, ATTN_W:2 * ATTN_W], kgn_ref)
    k_ref[...] = kn
    v_ref[...] = hn[:, 2 * ATTN_W:3 * ATTN_W]
    gu = _gelu(hn[:, 3 * ATTN_W:3 * ATTN_W + GMLP_W])
    gvn = _gmlp_norm(_gelu(hn[:, 3 * ATTN_W + GMLP_W:]), bones_ref, gvg_ref)
    gv_ref[...] = gvn
    gm_ref[...] = (_rms(gu * (wsc_ref[...] * gvn + bsc_ref[...])) * gog_ref[...]).astype(BF16)
    qs = qn * QK_SCALE
    q_ref[...] = qs
    qt_ref[...] = qs.T


def _proj_sample(x, w):
    nb = x.shape[0]
    out_shape = (jax.ShapeDtypeStruct((nb, ATTN_W), F32), jax.ShapeDtypeStruct((nb, ATTN_W), F32),
                 jax.ShapeDtypeStruct((nb, GMLP_W), F32), jax.ShapeDtypeStruct((nb, GMLP_W), BF16),
                 jax.ShapeDtypeStruct((nb, ATTN_W), F32), jax.ShapeDtypeStruct((ATTN_W, nb), F32))
    return pl.pallas_call(
        _proj_sample_kernel, out_shape=out_shape, compiler_params=_params(), name="proj_sample",
    )(x, w["g1"], w["win"], w["bones"], w["qg_nat"], w["kg_nat"], w["gvg"], w["ws_one"], w["b_one"], w["gog"])


_PAGES_PER_STEP = 32
_PAGES_PER_BLOCK = MOBA_BLOCK // PAGE
_PAGE_BLOCK = (None, N_HEADS, HEAD_DIM, PAGE)


def _sample_scores_kernel(pt_ref, qt_ref, q_ref, knew_ref, *refs, n_steps, past_len):
    k_refs = refs[:_PAGES_PER_STEP]
    p_ref, pself_ref, sel_ref, qcol_s, sc_s, gate_s = refs[_PAGES_PER_STEP:]
    d = pl.program_id(0)
    c = pl.program_id(1)
    n_seq = qt_ref.shape[1]
    lane = _iota((N_HEADS, LANES), 1)
    blocks_per_step = _PAGES_PER_STEP // _PAGES_PER_BLOCK

    @pl.when(c == 0)
    def _init():
        pick = jnp.where(_iota((n_seq, LANES), 0) == d, 1.0, 0.0)
        qcol_s[...] = _dot3(qt_ref[...], pick)
        gate_s[...] = jnp.zeros(gate_s.shape, F32)

    gates = gate_s[...]
    for b in range(blocks_per_step):
        blk = None
        for g in range(_PAGES_PER_BLOCK):
            r = b * _PAGES_PER_BLOCK + g
            rows = [jnp.sum(k_refs[r][hd] * qcol_s[HEAD_DIM * hd:HEAD_DIM * (hd + 1), :], axis=0, keepdims=True)
                    for hd in range(N_HEADS)]
            s_page = jnp.concatenate(rows, axis=0)
            sc_s[:, pl.ds(pl.multiple_of((c * _PAGES_PER_STEP + r) * PAGE, PAGE), PAGE)] = s_page
            blk = s_page if blk is None else blk + s_page
        gate = jnp.sum(blk, axis=-1, keepdims=True) * (1.0 / MOBA_BLOCK)
        gates = jnp.where(lane == c * blocks_per_step + b, gate, gates)
    gate_s[...] = gates

    @pl.when(c == n_steps - 1)
    def _select():
        n_blocks = n_steps * blocks_per_step
        lane_f = lane.astype(F32)
        cur = jnp.where(lane < n_blocks, gates, -jnp.inf)
        sel = jnp.zeros_like(cur)
        picked = jnp.zeros_like(cur)
        for r in range(MOBA_TOPK):
            m = jnp.max(cur, axis=-1, keepdims=True)
            first = jnp.min(jnp.where(cur == m, lane_f, float(LANES)), axis=-1, keepdims=True)
            hit = lane_f == first
            sel = jnp.where(hit, 1.0, sel)
            picked = jnp.where(lane == r, first, picked)
            cur = jnp.where(hit, -jnp.inf, cur)
        sel_ref[...] = picked

        n_pos = n_blocks * MOBA_BLOCK
        chosen = jnp.concatenate([jnp.broadcast_to(sel[:, n:n + 1], (N_HEADS, MOBA_BLOCK)) for n in range(n_blocks)],
                                 axis=1)
        slopes = jnp.exp2(-(_iota((N_HEADS, 1), 0) + 1).astype(F32))
        distance = float(past_len) - _iota((1, n_pos), 1).astype(F32)
        logit = jnp.where(chosen > 0.0, sc_s[...] - slopes * distance, NEG)
        self_logit = jnp.sum(q_ref[...] * knew_ref[...], axis=-1, keepdims=True)
        top = jnp.maximum(jnp.max(logit, axis=-1, keepdims=True), self_logit)
        p = jnp.exp(logit - top)
        p_self = jnp.exp(self_logit - top)
        inv = 1.0 / (jnp.sum(p, axis=-1, keepdims=True) + p_self)
        p_ref[...] = p * inv
        pself_ref[...] = jnp.broadcast_to(p_self * inv, pself_ref.shape)


def _sample_scores(page_table, cache_kt, qt, q, k_new, past_len):
    nb, n_pages = page_table.shape
    n_steps = n_pages // _PAGES_PER_STEP
    n_blocks = n_pages // _PAGES_PER_BLOCK
    assert n_blocks <= LANES
    page_spec = lambda r: pl.BlockSpec(
        _PAGE_BLOCK, lambda d, c, pt, r=r: (pt[d, c * _PAGES_PER_STEP + r], 0, 0, 0))
    per_seq = lambda shape: pl.BlockSpec((None,) + shape, lambda d, c, pt: (d,) + (0,) * len(shape))
    grid_spec = pltpu.PrefetchScalarGridSpec(
        num_scalar_prefetch=1, grid=(nb, n_steps),
        in_specs=[pl.BlockSpec((ATTN_W, nb), lambda d, c, pt: (0, 0)), per_seq((N_HEADS, HEAD_DIM)),
                  per_seq((N_HEADS, HEAD_DIM))] + [page_spec(r) for r in range(_PAGES_PER_STEP)],
        out_specs=(per_seq((N_HEADS, n_pages * PAGE)), per_seq((N_HEADS, HEAD_DIM)), per_seq((N_HEADS, LANES))),
        scratch_shapes=[pltpu.VMEM((ATTN_W, LANES), F32), pltpu.VMEM((N_HEADS, n_pages * PAGE), F32),
                        pltpu.VMEM((N_HEADS, LANES), F32)])
    out_shape = (jax.ShapeDtypeStruct((nb, N_HEADS, n_pages * PAGE), F32),
                 jax.ShapeDtypeStruct((nb, N_HEADS, HEAD_DIM), F32), jax.ShapeDtypeStruct((nb, N_HEADS, LANES), F32))
    return pl.pallas_call(
        functools.partial(_sample_scores_kernel, n_steps=n_steps, past_len=past_len),
        grid_spec=grid_spec, out_shape=out_shape,
        compiler_params=_params("parallel", "arbitrary"), name="sample_scores",
    )(page_table, qt, q, k_new, *([cache_kt] * _PAGES_PER_STEP))


_SLABS_PER_HEAD = MOBA_TOPK * _PAGES_PER_BLOCK


def _sample_values_kernel(phys_ref, logical_ref, p_ref, pself_ref, vnew_ref, *refs):
    n_slabs = N_HEADS * _SLABS_PER_HEAD
    v_refs = refs[:n_slabs]
    out_ref = refs[n_slabs]
    d = pl.program_id(0)
    cols = []
    for hd in range(N_HEADS):
        acc = None
        for k in range(_SLABS_PER_HEAD):
            i = hd * _SLABS_PER_HEAD + k
            start = pl.multiple_of(logical_ref[d, i] * PAGE, PAGE)
            term = v_refs[i][...] * p_ref[hd:hd + 1, pl.ds(start, PAGE)]
            acc = term if acc is None else acc + term
        cols.append(jnp.sum(acc, axis=-1, keepdims=True))
    ctx_t = jnp.concatenate(cols + [jnp.zeros((HEAD_DIM, LANES - N_HEADS), F32)], axis=1)
    out_ref[...] = ctx_t.T[0:N_HEADS, :] + pself_ref[...] * vnew_ref[...]


def _sample_values(slab_phys, slab_logical, cache_vt, p, pself, v_new):
    nb, n_slabs = slab_phys.shape
    assert n_slabs == N_HEADS * _SLABS_PER_HEAD
    slab_spec = lambda i: pl.BlockSpec(
        (None, None, HEAD_DIM, PAGE), lambda d, ph, lg, i=i: (ph[d, i], i // _SLABS_PER_HEAD, 0, 0))
    per_seq = lambda shape: pl.BlockSpec((None,) + shape, lambda d, ph, lg: (d,) + (0,) * len(shape))
    head_rows = per_seq((N_HEADS, HEAD_DIM))
    grid_spec = pltpu.PrefetchScalarGridSpec(
        num_scalar_prefetch=2, grid=(nb,),
        in_specs=[per_seq((N_HEADS, p.shape[2])), head_rows, head_rows] + [slab_spec(i) for i in range(n_slabs)],
        out_specs=head_rows)
    return pl.pallas_call(
        _sample_values_kernel, grid_spec=grid_spec, out_shape=jax.ShapeDtypeStruct((nb, N_HEADS, HEAD_DIM), F32),
        compiler_params=_params("parallel"), name="sample_values",
    )(slab_phys, slab_logical, p, pself, v_new, *([cache_vt] * n_slabs))


def _prepare(norm1_g, w_in, q_norm_g, k_norm_g, gmlp_v_norm_g, gmlp_ws, gmlp_b, attn_out_norm_g, gmlp_out_norm_g,
             w_out, norm2_g, peer_wq, peer_keys, peer_u, peer_v):
    wq, wk, wv, wgu, wgv = jnp.split(w_in, [ATTN_W, 2 * ATTN_W, 3 * ATTN_W, 3 * ATTN_W + GMLP_W], axis=1)
    wk_pad = jnp.pad(wk.reshape(D_MODEL, N_HEADS, HEAD_DIM), ((0, 0), (0, 0), (0, LANES - HEAD_DIM)))
    feat = np.arange(ATTN_W)
    w = {
        "g1": norm1_g.reshape(1, D_MODEL),
        "win": w_in.astype(BF16),
        "wnat": jnp.concatenate([wk_pad.reshape(D_MODEL, N_HEADS * LANES), wk, wv, wgu, wgv], axis=1).astype(BF16),
        "wt": jnp.concatenate([wq, wv], axis=1).T.astype(BF16),
        "qg_col": q_norm_g.reshape(HEAD_DIM, 1),
        "kg_pad": jnp.pad(k_norm_g, (0, LANES - HEAD_DIM)).reshape(1, LANES),
        "qg_nat": jnp.tile(q_norm_g, N_HEADS).reshape(1, ATTN_W),
        "kg_nat": jnp.tile(k_norm_g, N_HEADS).reshape(1, ATTN_W),
        "gvg": gmlp_v_norm_g.reshape(1, GMLP_W),
        "bones": jnp.asarray((feat[:, None] // GROUP_DIM) == (feat[None, :] // GROUP_DIM), BF16),
        "wcat": jnp.transpose(gmlp_ws, (1, 0, 2)).reshape(GMLP_CHUNK, N_GROUPS * GMLP_CHUNK),
        "gbias": jnp.repeat(gmlp_b.T, GROUP_DIM, axis=1),
        "ws_one": jnp.repeat(gmlp_ws[:, 0, 0], GROUP_DIM).reshape(1, GMLP_W),
        "b_one": jnp.repeat(gmlp_b[:, 0], GROUP_DIM).reshape(1, GMLP_W),
        "gog": gmlp_out_norm_g.reshape(1, GMLP_W),
        "aog": attn_out_norm_g.reshape(1, ATTN_W),
        "wo_attn": w_out[:ATTN_W].astype(BF16),
        "wo_gmlp": w_out[ATTN_W:].astype(BF16),
        "g2": norm2_g.reshape(1, D_MODEL),
        "wqt": peer_wq.T.astype(BF16),
        "keys": peer_keys.reshape(2 * PEER_HEADS, PEER_KEYS, PEER_HALF),
        "u": peer_u.astype(BF16),
        "vt": peer_v.T.astype(BF16),
        "cand_idx": _cand_index_table(),
    }
    return w


def _sample_attention(page_table, cache_k, cache_v, q, qt, k_new, v_new, past_len):
    nb = q.shape[0]
    heads = lambda a: a.reshape(nb, N_HEADS, HEAD_DIM)
    as_stored = lambda cache: jnp.transpose(cache, (0, 2, 3, 1))
    p, pself, picked = _sample_scores(page_table, as_stored(cache_k), qt, heads(q), heads(k_new), past_len)
    blocks = picked[:, :, :MOBA_TOPK].astype(jnp.int32)
    logical = (blocks[..., None] * _PAGES_PER_BLOCK + jnp.arange(_PAGES_PER_BLOCK, dtype=jnp.int32)).reshape(nb, -1)
    phys = jnp.take_along_axis(page_table, logical, axis=1)
    ctx = _sample_values(phys, logical.astype(jnp.int32), as_stored(cache_v), p, pself, heads(v_new))
    return ctx.reshape(nb, ATTN_W)


def _layer_tail(x, attn, gm, w, *, route_tile, peer_tile, attn_transposed):
    h, xn2, m1, c1, r2, e2 = _route(x, attn, gm, w, tt=route_tile, attn_transposed=attn_transposed)
    return _peer(xn2, h, m1, c1, r2, e2, w, tt=peer_tile)


def kernel(x_prompt, x_sample, cache_k, cache_v, page_table, norm1_g, w_in, q_norm_g, k_norm_g, gmlp_v_norm_g,
           gmlp_ws, gmlp_b, attn_out_norm_g, gmlp_out_norm_g, w_out, norm2_g, peer_wq, peer_keys, peer_u, peer_v):
    w = _prepare(norm1_g, w_in, q_norm_g, k_norm_g, gmlp_v_norm_g, gmlp_ws, gmlp_b, attn_out_norm_g,
                 gmlp_out_norm_g, w_out, norm2_g, peer_wq, peer_keys, peer_u, peer_v)
    b, t, _ = x_prompt.shape
    assert b == 1
    xp = x_prompt.reshape(t, D_MODEL)
    k_p, v_p, kaug, qaug, vt, gm_p, gv_last = _proj_prompt(xp, w)
    attn_t = _moba_prompt(kaug, qaug, vt)
    y_p = _layer_tail(xp, attn_t, gm_p, w, route_tile=512, peer_tile=512, attn_transposed=True)

    nb, ds, _ = x_sample.shape
    assert ds == 1
    xs = x_sample.reshape(nb, D_MODEL)
    past_len = page_table.shape[1] * PAGE
    assert past_len % MOBA_BLOCK == 0
    k_s, v_s, gv_s, gm_s, q_s, qt_s = _proj_sample(xs, w)
    attn_s = _sample_attention(page_table, cache_k, cache_v, q_s, qt_s, k_s, v_s, past_len)
    y_s = _layer_tail(xs, attn_s, gm_s, w, route_tile=nb, peer_tile=nb, attn_transposed=False)

    return (y_p.reshape(1, t, D_MODEL), y_s.reshape(nb, 1, D_MODEL),
            k_p.reshape(1, t, N_HEADS, HEAD_DIM), v_p.reshape(1, t, N_HEADS, HEAD_DIM),
            gv_last.reshape(1, GMLP_CHUNK, GMLP_W),
            k_s.reshape(nb, 1, N_HEADS, HEAD_DIM), v_s.reshape(nb, 1, N_HEADS, HEAD_DIM),
            gv_s.reshape(nb, 1, GMLP_W))
```

```python
import functools
import math

import jax
import jax.numpy as jnp
import numpy as np
from jax import lax
from jax.experimental import pallas as pl
from jax.experimental.pallas import tpu as pltpu

F32 = jnp.float32
BF16 = jnp.bfloat16

D_MODEL = 1024
N_HEADS = 8
HEAD_DIM = 64
ATTN_W = N_HEADS * HEAD_DIM
GMLP_W = 512
N_GROUPS = 8
GROUP_DIM = GMLP_W // N_GROUPS
GMLP_CHUNK = 128
MOBA_BLOCK = 256
MOBA_TOPK = 3
PAGE = 128
PEER_HEADS = 8
PEER_KEYS = 128
PEER_HALF = 128
PEER_TOPK = 16
N_EXPERTS = PEER_KEYS * PEER_KEYS
EPS = 1e-6
NEG = -1e30
QK_SCALE = HEAD_DIM ** -0.5
GELU_C = math.sqrt(2.0 / math.pi)


def _bf16_pieces(x, n):
    out, rest = [], np.float32(x)
    for _ in range(n):
        piece = np.float32(rest.astype(jnp.bfloat16))
        out.append(float(piece))
        rest = np.float32(rest - piece)
    return tuple(out)


LOG2E = float(np.float32(math.log2(math.e)))
_LOG2E_PIECES = _bf16_pieces(LOG2E, 3)
_GELU_A = _bf16_pieces(-2.0 * GELU_C * 0.044715, 2)
_GELU_B = _bf16_pieces(-2.0 * GELU_C, 2)

LANES = 128
VMEM_LIMIT = 56 * 1024 * 1024

_CAND_ROWS = tuple((a, PEER_TOPK // (a + 1)) for a in range(8))


def _dot(a, b):
    return jnp.dot(a, b, preferred_element_type=F32)


def _dot_nt(a, b):
    return lax.dot_general(a, b, (((1,), (1,)), ((), ())), preferred_element_type=F32)


def _split(a):
    hi = a.astype(BF16)
    lo = (a - hi.astype(F32)).astype(BF16)
    return hi, lo


def _dot3(a, b):
    ah, al = _split(a)
    bh, bl = _split(b)
    return _dot(ah, bh) + (_dot(ah, bl) + _dot(al, bh))


def _rms(x, axis=-1):
    return x * lax.rsqrt(jnp.mean(x * x, axis=axis, keepdims=True) + EPS)


def _gelu(x):
    return x * (0.5 * (1.0 + jnp.tanh(GELU_C * (x + 0.044715 * (x * x * x)))))


def _gelu_sigmoid(x):
    t = x * x
    if x.dtype == BF16:
        poly = (_GELU_A[0] * t + _GELU_B[0]) + (_GELU_A[1] * t + _GELU_B[1])
    else:
        poly = (-2.0 * GELU_C * 0.044715) * t - 2.0 * GELU_C
    return x / (1.0 + jnp.exp(x * poly))


def _iota(shape, dim, dtype=jnp.int32):
    return lax.broadcasted_iota(dtype, shape, dim)


def _params(*sem):
    return pltpu.CompilerParams(dimension_semantics=sem, vmem_limit_bytes=VMEM_LIMIT)


def _full(shape):
    nd = len(shape)
    return pl.BlockSpec(shape, lambda *_: (0,) * nd)


_KP0, _KN0, _V0, _GU0, _GV0, _NAT_COLS = 0, 1024, 1536, 2048, 2560, 3072
_VT_ROWS = HEAD_DIM + 16


def _gmlp_norm(gvr, bones_ref, gvg_ref):
    hi, lo = _split(gvr * gvr)
    ss = _dot(hi, bones_ref[...]) + _dot(lo, bones_ref[...])
    return gvr * lax.rsqrt(ss * (1.0 / GROUP_DIM) + EPS) * gvg_ref[...]


def _proj_prompt_kernel(x_ref, g1_ref, wnat_ref, wt_ref, qg_ref, kgp_ref, kgn_ref, gvg_ref, bones_ref,
                        wcat_ref, gbias_ref, gog_ref,
                        k_ref, v_ref, kaug_ref, qaug_ref, vt_ref, gm_ref, gvl_ref,
                        kmean_s, wtril_s):
    i = pl.program_id(0)
    tm = x_ref.shape[0]
    nblk = kmean_s.shape[1]

    @pl.when(i == 0)
    def _init():
        row = _iota((GMLP_CHUNK, N_GROUPS * GMLP_CHUNK), 0)
        col = _iota((GMLP_CHUNK, N_GROUPS * GMLP_CHUNK), 1) & (GMLP_CHUNK - 1)
        wtril_s[...] = jnp.where(col <= row, wcat_ref[...], 0.0).astype(BF16)
        kmean_s[...] = jnp.zeros(kmean_s.shape, F32)

    xn = (_rms(x_ref[...]) * g1_ref[...]).astype(BF16)
    hn = _dot(xn, wnat_ref[...])
    ht = _dot_nt(wt_ref[...], xn)

    lane = _iota((tm, LANES), 1)
    rowf = _iota((tm, LANES), 0).astype(F32)
    blk_row = _iota((nblk, tm), 0)
    blk_rowf = blk_row.astype(F32)
    piece_row = _iota((LANES, tm), 0)
    i_f = i.astype(F32)
    k_hi = jnp.where(lane < 3, rowf, jnp.where(lane < 6, float(MOBA_BLOCK) * i_f, 0.0))
    ones_rows = jnp.where(_iota((_VT_ROWS - HEAD_DIM, tm), 0) == 0, 1.0, 0.0)

    rs_heads = []
    for h in range(N_HEADS):
        kp = hn[:, _KP0 + LANES * h:_KP0 + LANES * (h + 1)]
        rs = lax.rsqrt(jnp.sum(kp * kp, axis=-1, keepdims=True) * (1.0 / HEAD_DIM) + EPS)
        rs_heads.append(rs)
        kn = kp * rs * kgp_ref[...]
        kmean_s[h, pl.ds(i, 1), :] = jnp.mean(kn, axis=0, keepdims=True)

        qt = ht[HEAD_DIM * h:HEAD_DIM * (h + 1), :]
        qn = qt * lax.rsqrt(jnp.sum(qt * qt, axis=0, keepdims=True) * (1.0 / HEAD_DIM) + EPS) * qg_ref[...]
        gate = _dot3(kmean_s[h], jnp.concatenate([qn, jnp.zeros_like(qn)], axis=0))
        cur = jnp.where(blk_row < i, gate, NEG)
        sel = jnp.zeros_like(cur)
        for _ in range(MOBA_TOPK):
            m = jnp.max(cur, axis=0, keepdims=True)
            first = jnp.min(jnp.where(cur == m, blk_rowf, float(nblk)), axis=0, keepdims=True)
            hit = blk_rowf == first
            sel = jnp.where(hit, 1.0, sel)
            cur = jnp.where(hit, -jnp.inf, cur)
        keep = jnp.where(blk_row < i, sel, jnp.where(blk_row == i, 1.0, 0.0))
        sel_bias = jnp.where(keep > 0.0, 0.0, NEG)
        pieces = [qn * (QK_SCALE * LOG2E), sel_bias]
        if nblk < HEAD_DIM:
            pieces.append(jnp.zeros((HEAD_DIM - nblk, tm), F32))
        slope = 2.0 ** -(h + 1)
        slope_rows = jnp.zeros((LANES, tm), F32)
        for r, piece in enumerate(_LOG2E_PIECES * 2):
            slope_rows = jnp.where(piece_row == r, slope * piece, slope_rows)
        qaug_ref[h] = jnp.concatenate(pieces + [slope_rows], axis=0).astype(BF16)

        k_lo = jnp.where(lane == HEAD_DIM + i, 1.0, kn)
        kaug_ref[h] = jnp.concatenate([k_lo, k_hi], axis=1).astype(BF16)
        vt_ref[h] = jnp.concatenate([ht[ATTN_W + HEAD_DIM * h:ATTN_W + HEAD_DIM * (h + 1), :], ones_rows],
                                    axis=0).astype(BF16)

    cols = []
    for c in range(ATTN_W // LANES):
        sc = jnp.where(lane < HEAD_DIM, rs_heads[2 * c], rs_heads[2 * c + 1])
        cols.append(hn[:, _KN0 + LANES * c:_KN0 + LANES * (c + 1)] * sc)
    k_ref[...] = jnp.concatenate(cols, axis=1) * kgn_ref[...]
    v_ref[...] = hn[:, _V0:_V0 + ATTN_W]

    gu = _gelu(hn[:, _GU0:_GU0 + GMLP_W])
    gvn = _gmlp_norm(_gelu(hn[:, _GV0:_GV0 + GMLP_W]), bones_ref, gvg_ref)
    gvl_ref[...] = gvn[tm - GMLP_CHUNK:, :]
    group_of_lane = _iota((1, GMLP_W), 1) >> 6
    outs = []
    for c in range(tm // GMLP_CHUNK):
        gc = gvn[GMLP_CHUNK * c:GMLP_CHUNK * (c + 1), :]
        stacked = jnp.concatenate([jnp.where(group_of_lane == g, gc, 0.0) for g in range(N_GROUPS)], axis=0)
        mixed = _dot(wtril_s[...], stacked.astype(BF16)) + gbias_ref[...]
        outs.append(gu[GMLP_CHUNK * c:GMLP_CHUNK * (c + 1), :] * mixed)
    gm_ref[...] = (_rms(jnp.concatenate(outs, axis=0)) * gog_ref[...]).astype(BF16)


def _proj_prompt(x, w, tm=MOBA_BLOCK):
    t = x.shape[0]
    nblk = t // tm
    assert nblk <= HEAD_DIM
    row_tile = lambda width: pl.BlockSpec((tm, width), lambda i: (i, 0))
    in_specs = [row_tile(D_MODEL), _full((1, D_MODEL)), _full((D_MODEL, _NAT_COLS)), _full((D_MODEL, D_MODEL)),
                _full((HEAD_DIM, 1)), _full((1, LANES)), _full((1, ATTN_W)), _full((1, GMLP_W)),
                _full((GMLP_W, GMLP_W)), _full((GMLP_CHUNK, N_GROUPS * GMLP_CHUNK)), _full((GMLP_CHUNK, GMLP_W)),
                _full((1, GMLP_W))]
    out_shape = (jax.ShapeDtypeStruct((t, ATTN_W), F32), jax.ShapeDtypeStruct((t, ATTN_W), F32),
                 jax.ShapeDtypeStruct((N_HEADS, t, 2 * LANES), BF16),
                 jax.ShapeDtypeStruct((N_HEADS, 2 * LANES, t), BF16),
                 jax.ShapeDtypeStruct((N_HEADS, _VT_ROWS, t), BF16),
                 jax.ShapeDtypeStruct((t, GMLP_W), BF16), jax.ShapeDtypeStruct((GMLP_CHUNK, GMLP_W), F32))
    out_specs = (row_tile(ATTN_W), row_tile(ATTN_W),
                 pl.BlockSpec((N_HEADS, tm, 2 * LANES), lambda i: (0, i, 0)),
                 pl.BlockSpec((N_HEADS, 2 * LANES, tm), lambda i: (0, 0, i)),
                 pl.BlockSpec((N_HEADS, _VT_ROWS, tm), lambda i: (0, 0, i)),
                 row_tile(GMLP_W), _full((GMLP_CHUNK, GMLP_W)))
    return pl.pallas_call(
        _proj_prompt_kernel, grid=(nblk,), in_specs=in_specs, out_specs=out_specs, out_shape=out_shape,
        scratch_shapes=[pltpu.VMEM((N_HEADS, nblk, LANES), F32),
                        pltpu.VMEM((GMLP_CHUNK, N_GROUPS * GMLP_CHUNK), BF16)],
        compiler_params=_params("arbitrary"), name="proj_prompt",
    )(x, w["g1"], w["wnat"], w["wt"], w["qg_col"], w["kg_pad"], w["kg_nat"], w["gvg"], w["bones"],
      w["wcat"], w["gbias"], w["gog"])


_MOBA_HEADS_PER_STEP = 4


def _moba_kernel(kaug_ref, qaug_ref, vt_ref, out_ref, s_even, s_odd):
    j = pl.program_id(1)
    tq = qaug_ref.shape[2]
    heads = range(_MOBA_HEADS_PER_STEP)

    def block_start(n):
        return pl.multiple_of(n * MOBA_BLOCK, MOBA_BLOCK)

    def produce(buf, n):
        for hh in heads:
            buf[hh] = _dot(kaug_ref[hh, pl.ds(block_start(n), MOBA_BLOCK), :], qaug_ref[hh])

    def fold(state, s, hh, n):
        m, acc = state
        m_new = jnp.maximum(m, jnp.max(s, axis=0, keepdims=True))
        alpha = jnp.exp2(m - m_new)
        p = jnp.exp2(s - m_new).astype(BF16)
        acc = acc * alpha + _dot(vt_ref[hh, :, pl.ds(block_start(n), MOBA_BLOCK)], p)
        return m_new, acc

    def body(i, states):
        produce(s_odd, 2 * i + 1)
        states = tuple(fold(states[hh], s_even[hh], hh, 2 * i) for hh in heads)
        produce(s_even, 2 * i + 2)
        return tuple(fold(states[hh], s_odd[hh], hh, 2 * i + 1) for hh in heads)

    init = tuple((jnp.full((1, tq), -jnp.inf, F32), jnp.zeros((_VT_ROWS, tq), F32)) for _ in heads)
    produce(s_even, 0)
    n_pairs = j // 2
    def quad(k, st):
        for r in range(4):
            st = body(4 * k + r, st)
        return st

    states = lax.fori_loop(0, n_pairs // 4, quad, init)
    states = lax.fori_loop((n_pairs // 4) * 4, n_pairs, body, states)

    causal = _iota((MOBA_BLOCK, tq), 0) <= _iota((MOBA_BLOCK, tq), 1)
    j_odd = (j & 1) == 1
    produce(s_odd, j)
    for hh in heads:
        first = jnp.where(j_odd, s_even[hh], jnp.where(causal, s_even[hh], NEG))
        state = fold(states[hh], first, hh, 2 * (j // 2))
        second = jnp.where(j_odd, jnp.where(causal, s_odd[hh], NEG), NEG)
        _, acc = fold(state, second, hh, j)
        out_ref[HEAD_DIM * hh:HEAD_DIM * (hh + 1), :] = acc[:HEAD_DIM, :] / acc[HEAD_DIM:HEAD_DIM + 1, :]


def _moba_prompt(kaug, qaug, vt):
    t = kaug.shape[1]
    tq = MOBA_BLOCK
    hs = _MOBA_HEADS_PER_STEP
    once = pl.Buffered(1)
    return pl.pallas_call(
        _moba_kernel, grid=(N_HEADS // hs, t // tq),
        in_specs=[pl.BlockSpec((hs, t, 2 * LANES), lambda g, j: (g, 0, 0), pipeline_mode=once),
                  pl.BlockSpec((hs, 2 * LANES, tq), lambda g, j: (g, 0, j)),
                  pl.BlockSpec((hs, _VT_ROWS, t), lambda g, j: (g, 0, 0), pipeline_mode=once)],
        out_specs=pl.BlockSpec((hs * HEAD_DIM, tq), lambda g, j: (g, j)),
        out_shape=jax.ShapeDtypeStruct((ATTN_W, t), F32),
        scratch_shapes=[pltpu.VMEM((hs, MOBA_BLOCK, tq), F32), pltpu.VMEM((hs, MOBA_BLOCK, tq), F32)],
        compiler_params=_params("parallel", "arbitrary"), name="moba_prompt",
    )(kaug, qaug, vt)


def _cand_index_table():
    rows = []
    for a, nb in _CAND_ROWS:
        for c0 in range(0, max(nb, 8), 8):
            rows.append([a * PEER_TOPK + c0 + b for b in range(8)])
    rows.append([(8 + a) * PEER_TOPK for a in range(8)])
    flat = np.asarray(rows, np.float32).reshape(-1, 1)
    return jnp.asarray(np.broadcast_to(flat, (flat.shape[0], LANES)).copy())


_N_CAND = 8 * (sum(max(nb, 8) // 8 for _, nb in _CAND_ROWS) + 1)


def _extract_top(cur, idx, n, on_hit):
    big = float(1 << 20)
    for r in range(n):
        m = jnp.max(cur, axis=0, keepdims=True)
        first = jnp.min(jnp.where(cur == m, idx, big), axis=0, keepdims=True)
        hit = idx == first
        cur = jnp.where(hit, -jnp.inf, cur)
        on_hit(r, m, hit)
    return cur


def _fold_keys_kernel(keys_ref, wqt_ref, out_ref):
    out_ref[...] = _dot3(keys_ref[...], wqt_ref[...]).astype(BF16)


def _fold_keys(keys, wqt):
    n_half = keys.shape[0]
    return pl.pallas_call(
        _fold_keys_kernel, grid=(n_half,),
        in_specs=[pl.BlockSpec((None, PEER_KEYS, PEER_HALF), lambda i: (i, 0, 0)),
                  pl.BlockSpec((PEER_HALF, D_MODEL), lambda i: (i, 0))],
        out_specs=pl.BlockSpec((PEER_KEYS, D_MODEL), lambda i: (i, 0)),
        out_shape=jax.ShapeDtypeStruct((n_half * PEER_KEYS, D_MODEL), BF16),
        compiler_params=_params("parallel"), name="fold_keys",
    )(keys, wqt)


def _route_kernel(x_ref, attn_ref, gm_ref, aog_ref, woa_ref, wog_ref, g2_ref, wks_ref, cidx_ref,
                  h_ref, xn2_ref, m1_ref, c1_ref, r2_ref, e2_ref,
                  s_s, rank_s, sv_s, na_s, chosen_s, *, attn_transposed):
    tt = x_ref.shape[0]
    at = attn_ref[...]
    if attn_transposed:
        at = at.T
    an = (_rms(at) * aog_ref[...]).astype(BF16)
    h = x_ref[...] + _dot(an, woa_ref[...]) + _dot(gm_ref[...], wog_ref[...])
    h_ref[...] = h
    xn2 = (_rms(h) * g2_ref[...]).astype(BF16)
    xn2_ref[...] = xn2
    scores = _dot_nt(wks_ref[...], xn2)
    for hp in range(2 * PEER_HEADS):
        s_s[hp] = scores[PEER_KEYS * hp:PEER_KEYS * (hp + 1), :]

    key_idx = _iota((PEER_KEYS, LANES), 0).astype(F32)
    row8 = _iota((8, LANES), 0)

    def rank_distinct(hp, lanes):
        cur = s_s[hp, :, lanes]
        rank = jnp.full((PEER_KEYS, LANES), float(PEER_TOPK), F32)
        for r in range(PEER_TOPK):
            m = jnp.max(cur, axis=0, keepdims=True)
            hit = cur == m
            cur = jnp.where(hit, -jnp.inf, cur)
            rank = jnp.where(hit, float(r), rank)
            sv_s[hp, r:r + 1, lanes] = m
        rank_s[hp, :, lanes] = rank
        return jnp.sum(jnp.where(rank < float(PEER_TOPK), 1.0, 0.0), axis=0, keepdims=True)

    def rank_exact(hp, lanes):
        rank = [jnp.full((PEER_KEYS, LANES), float(PEER_TOPK), F32)]

        def on_hit(r, m, hit):
            rank[0] = jnp.where(hit, float(r), rank[0])
            sv_s[hp, r:r + 1, lanes] = m

        _extract_top(s_s[hp, :, lanes], key_idx, PEER_TOPK, on_hit)
        rank_s[hp, :, lanes] = rank[0]

    def per_head(hd, carry):
        ranked = []
        for sl in range(tt // LANES):
            lanes = slice(LANES * sl, LANES * (sl + 1))
            for hp in (2 * hd, 2 * hd + 1):
                ranked.append((hp, lanes, rank_distinct(hp, lanes)))
        off_count = functools.reduce(jnp.maximum, [jnp.abs(n - float(PEER_TOPK)) for _, _, n in ranked])

        @pl.when(jnp.max(off_count) > 0.0)
        def _redo_ranks():
            for hp, lanes, _ in ranked:
                rank_exact(hp, lanes)
        def candidates(lanes):
            sv1 = sv_s[2 * hd, :, lanes]
            sv2 = sv_s[2 * hd + 1, :, lanes]
            pieces = []
            for a, nb in _CAND_ROWS:
                for c0 in range(0, max(nb, 8), 8):
                    piece = sv1[a:a + 1, :] + sv2[c0:c0 + 8, :]
                    pieces.append(piece if nb >= 8 else jnp.where(row8 < nb, piece, -jnp.inf))
            pieces.append(sv1[8:16, :] + sv2[0:1, :])
            return sv1, sv2, jnp.concatenate(pieces, axis=0)

        n_chosen = []
        for sl in range(tt // LANES):
            _, _, cur = candidates(slice(LANES * sl, LANES * (sl + 1)))
            chosen = jnp.zeros_like(cur)
            for _ in range(PEER_TOPK):
                hit = cur == jnp.max(cur, axis=0, keepdims=True)
                cur = jnp.where(hit, -jnp.inf, cur)
                chosen = jnp.where(hit, 1.0, chosen)
            chosen_s[sl] = chosen
            n_chosen.append(jnp.sum(chosen, axis=0, keepdims=True))
        off_count = functools.reduce(jnp.maximum, [jnp.abs(n - float(PEER_TOPK)) for n in n_chosen])

        @pl.when(jnp.max(off_count) > 0.0)
        def _redo_sums():
            for sl in range(tt // LANES):
                _, _, cand = candidates(slice(LANES * sl, LANES * (sl + 1)))
                chosen = [jnp.zeros_like(cand)]

                def on_hit(r, m, hit):
                    chosen[0] = jnp.where(hit, 1.0, chosen[0])

                _extract_top(cand, cidx_ref[...], PEER_TOPK, on_hit)
                chosen_s[sl] = chosen[0]

        for sl in range(tt // LANES):
            lanes = slice(LANES * sl, LANES * (sl + 1))
            sv1, sv2, cand = candidates(lanes)
            chosen = chosen_s[sl]
            top = sv1[0:1, :] + sv2[0:1, :]
            z = jnp.sum(jnp.where(chosen > 0.0, jnp.exp(cand - top), 0.0), axis=0, keepdims=True)
            row = 0
            for a, nb in _CAND_ROWS:
                nrows = max(nb, 8)
                na_s[sl, a:a + 1, :] = jnp.sum(chosen[row:row + nrows, :], axis=0, keepdims=True)
                row += nrows
            na_s[sl, 8:16, :] = chosen[row:row + 8, :]
            na = na_s[sl]
            rank1 = rank_s[2 * hd, :, lanes]
            m1 = jnp.zeros((PEER_KEYS, LANES), F32)
            for a in range(PEER_TOPK):
                m1 = jnp.where(rank1 == float(a), na[a:a + 1, :], m1)
            m1_ref[hd, :, lanes] = m1
            c1_ref[hd, :, lanes] = jnp.exp(s_s[2 * hd, :, lanes] - sv1[0:1, :]) / z
            r2_ref[hd, :, lanes] = rank_s[2 * hd + 1, :, lanes].astype(BF16)
            e2_ref[hd, :, lanes] = jnp.exp(s_s[2 * hd + 1, :, lanes] - sv2[0:1, :]).astype(BF16)
        return carry

    lax.fori_loop(0, PEER_HEADS, per_head, 0)


def _route(x, attn, gm, w, *, tt, attn_transposed):
    t = x.shape[0]
    row_tile = lambda width: pl.BlockSpec((tt, width), lambda i: (i, 0))
    attn_spec = pl.BlockSpec((ATTN_W, tt), lambda i: (0, i)) if attn_transposed else row_tile(ATTN_W)
    head_tile = pl.BlockSpec((PEER_HEADS, PEER_KEYS, tt), lambda i: (0, 0, i))
    in_specs = [row_tile(D_MODEL), attn_spec, row_tile(GMLP_W), _full((1, ATTN_W)),
                _full((ATTN_W, D_MODEL)), _full((GMLP_W, D_MODEL)), _full((1, D_MODEL)),
                _full((2 * PEER_HEADS * PEER_KEYS, D_MODEL)), _full((_N_CAND, LANES))]
    stat = lambda dt: jax.ShapeDtypeStruct((PEER_HEADS, PEER_KEYS, t), dt)
    out_shape = (jax.ShapeDtypeStruct((t, D_MODEL), F32), jax.ShapeDtypeStruct((t, D_MODEL), BF16),
                 stat(F32), stat(F32), stat(BF16), stat(BF16))
    out_specs = (row_tile(D_MODEL), row_tile(D_MODEL), head_tile, head_tile, head_tile, head_tile)
    return pl.pallas_call(
        functools.partial(_route_kernel, attn_transposed=attn_transposed),
        grid=(t // tt,), in_specs=in_specs, out_specs=out_specs, out_shape=out_shape,
        scratch_shapes=[pltpu.VMEM((2 * PEER_HEADS, PEER_KEYS, tt), F32),
                        pltpu.VMEM((2 * PEER_HEADS, PEER_KEYS, tt), F32),
                        pltpu.VMEM((2 * PEER_HEADS, PEER_TOPK, tt), F32),
                        pltpu.VMEM((tt // LANES, PEER_TOPK, LANES), F32),
                        pltpu.VMEM((tt // LANES, _N_CAND, LANES), F32)],
        compiler_params=_params("parallel"), name="route",
    )(x, attn, gm, w["aog"], w["wo_attn"], w["wo_gmlp"], w["g2"], w["wks"], w["cand_idx"])


def _peer_kernel(xn2_ref, u_ref, vt_ref, *refs, keys_per_step, n_sub):
    tables = [refs[4 * s:4 * s + 4] for s in range(n_sub)]
    h_ref, y_ref, acc_s, act_even_s, act_odd_s, gated_s = refs[4 * n_sub:]
    c = pl.program_id(1)
    n_blocks = pl.num_programs(1) - 1
    sub = xn2_ref.shape[0] // n_sub

    @pl.when(c == 0)
    def _zero():
        acc_s[...] = jnp.zeros(acc_s.shape, F32)
        act_odd_s[...] = jnp.zeros(act_odd_s.shape, BF16)

    def step(act_write, act_read):
        first_key = jnp.maximum(c - 1, 0) * keys_per_step
        for s, (m1_ref, c1_ref, r2_ref, e2_ref) in enumerate(tables):
            cols = slice(sub * s, sub * (s + 1))
            act_write[:, cols] = _gelu_sigmoid(_dot_nt(u_ref[...], xn2_ref[cols, :]).astype(BF16))
            for ii in range(keys_per_step):
                i = first_key + ii
                rows = slice(PEER_KEYS * ii, PEER_KEYS * (ii + 1))
                g = None
                for hd in range(PEER_HEADS):
                    partners = m1_ref[hd, pl.ds(i, 1), :].astype(BF16)
                    weight = c1_ref[hd, pl.ds(i, 1), :].astype(BF16)
                    term = jnp.where(r2_ref[hd] < partners, e2_ref[hd], jnp.zeros((), BF16)) * weight
                    g = term if g is None else g + term
                gated_s[rows, cols] = g * act_read[rows, cols]
            acc_s[:, cols] += _dot(vt_ref[...], gated_s[:, cols])

    @pl.when((c & 1) == 0)
    def _even():
        step(act_even_s, act_odd_s)

    @pl.when((c & 1) == 1)
    def _odd():
        step(act_odd_s, act_even_s)

    @pl.when(c == n_blocks)
    def _finish():
        y_ref[...] = h_ref[...] + acc_s[...].T


_PEER_SUBTILE = 256


def _peer(xn2, h, m1, c1, r2, e2, w, *, tt, keys_per_step=8):
    t = xn2.shape[0]
    ne = keys_per_step * PEER_KEYS
    n_blocks = N_EXPERTS // ne
    sub = min(tt, _PEER_SUBTILE)
    n_sub = tt // sub
    row_tile = pl.BlockSpec((tt, D_MODEL), lambda j, c: (j, 0))
    table_specs, table_args = [], []
    for s in range(n_sub):
        spec = pl.BlockSpec((PEER_HEADS, PEER_KEYS, sub), lambda j, c, s=s: (0, 0, j * n_sub + s))
        table_specs += [spec] * 4
        table_args += [m1, c1, r2, e2]
    return pl.pallas_call(
        functools.partial(_peer_kernel, keys_per_step=keys_per_step, n_sub=n_sub),
        grid=(t // tt, n_blocks + 1),
        in_specs=[row_tile, pl.BlockSpec((ne, D_MODEL), lambda j, c: (jnp.minimum(c, n_blocks - 1), 0)),
                  pl.BlockSpec((D_MODEL, ne), lambda j, c: (0, jnp.maximum(c - 1, 0)))] + table_specs + [row_tile],
        out_specs=row_tile, out_shape=jax.ShapeDtypeStruct((t, D_MODEL), F32),
        scratch_shapes=[pltpu.VMEM((D_MODEL, tt), F32)] + [pltpu.VMEM((ne, tt), BF16)] * 3,
        compiler_params=_params("parallel", "arbitrary"), name="peer",
    )(xn2, w["u"], w["vt"], *table_args, h)


def _proj_sample_kernel(x_ref, g1_ref, win_ref, bones_ref, qgn_ref, kgn_ref, gvg_ref, wsc_ref, bsc_ref, gog_ref,
                        k_ref, v_ref, gv_ref, gm_ref, q_ref, qt_ref):
    xn = (_rms(x_ref[...]) * g1_ref[...]).astype(BF16)
    hn = _dot(xn, win_ref[...])

    def head_norm(z, g_ref):
        hi, lo = _split(z * z)
        ss = _dot(hi, bones_ref[...]) + _dot(lo, bones_ref[...])
        return z * lax.rsqrt(ss * (1.0 / HEAD_DIM) + EPS) * g_ref[...]

    qn = head_norm(hn[:, 0:ATTN_W], qgn_ref)
    kn = head_norm(hn[:, ATTN_W:2 * ATTN_W], kgn_ref)
    k_ref[...] = kn
    v_ref[...] = hn[:, 2 * ATTN_W:3 * ATTN_W]
    gu = _gelu(hn[:, 3 * ATTN_W:3 * ATTN_W + GMLP_W])
    gvn = _gmlp_norm(_gelu(hn[:, 3 * ATTN_W + GMLP_W:]), bones_ref, gvg_ref)
    gv_ref[...] = gvn
    gm_ref[...] = (_rms(gu * (wsc_ref[...] * gvn + bsc_ref[...])) * gog_ref[...]).astype(BF16)
    qs = qn * QK_SCALE
    q_ref[...] = qs
    qt_ref[...] = qs.T


def _proj_sample(x, w):
    nb = x.shape[0]
    out_shape = (jax.ShapeDtypeStruct((nb, ATTN_W), F32), jax.ShapeDtypeStruct((nb, ATTN_W), F32),
                 jax.ShapeDtypeStruct((nb, GMLP_W), F32), jax.ShapeDtypeStruct((nb, GMLP_W), BF16),
                 jax.ShapeDtypeStruct((nb, ATTN_W), F32), jax.ShapeDtypeStruct((ATTN_W, nb), F32))
    return pl.pallas_call(
        _proj_sample_kernel, out_shape=out_shape, compiler_params=_params(), name="proj_sample",
    )(x, w["g1"], w["win"], w["bones"], w["qg_nat"], w["kg_nat"], w["gvg"], w["ws_one"], w["b_one"], w["gog"])


_PAGES_PER_STEP = 32
_PAGES_PER_BLOCK = MOBA_BLOCK // PAGE
_PAGE_BLOCK = (None, N_HEADS, HEAD_DIM, PAGE)


def _sample_scores_kernel(pt_ref, qt_ref, q_ref, knew_ref, *refs, n_steps, past_len):
    k_refs = refs[:_PAGES_PER_STEP]
    p_ref, pself_ref, sel_ref, qcol_s, sc_s, gate_s = refs[_PAGES_PER_STEP:]
    d = pl.program_id(0)
    c = pl.program_id(1)
    n_seq = qt_ref.shape[1]
    lane = _iota((N_HEADS, LANES), 1)
    blocks_per_step = _PAGES_PER_STEP // _PAGES_PER_BLOCK

    @pl.when(c == 0)
    def _init():
        pick = jnp.where(_iota((n_seq, LANES), 0) == d, 1.0, 0.0)
        qcol_s[...] = _dot3(qt_ref[...], pick)
        gate_s[...] = jnp.zeros(gate_s.shape, F32)

    gates = gate_s[...]
    for b in range(blocks_per_step):
        blk = None
        for g in range(_PAGES_PER_BLOCK):
            r = b * _PAGES_PER_BLOCK + g
            rows = [jnp.sum(k_refs[r][hd] * qcol_s[HEAD_DIM * hd:HEAD_DIM * (hd + 1), :], axis=0, keepdims=True)
                    for hd in range(N_HEADS)]
            s_page = jnp.concatenate(rows, axis=0)
            sc_s[:, pl.ds(pl.multiple_of((c * _PAGES_PER_STEP + r) * PAGE, PAGE), PAGE)] = s_page
            blk = s_page if blk is None else blk + s_page
        gate = jnp.sum(blk, axis=-1, keepdims=True) * (1.0 / MOBA_BLOCK)
        gates = jnp.where(lane == c * blocks_per_step + b, gate, gates)
    gate_s[...] = gates

    @pl.when(c == n_steps - 1)
    def _select():
        n_blocks = n_steps * blocks_per_step
        lane_f = lane.astype(F32)
        cur = jnp.where(lane < n_blocks, gates, -jnp.inf)
        sel = jnp.zeros_like(cur)
        picked = jnp.zeros_like(cur)
        for r in range(MOBA_TOPK):
            m = jnp.max(cur, axis=-1, keepdims=True)
            first = jnp.min(jnp.where(cur == m, lane_f, float(LANES)), axis=-1, keepdims=True)
            hit = lane_f == first
            sel = jnp.where(hit, 1.0, sel)
            picked = jnp.where(lane == r, first, picked)
            cur = jnp.where(hit, -jnp.inf, cur)
        sel_ref[...] = picked

        n_pos = n_blocks * MOBA_BLOCK
        chosen = jnp.concatenate([jnp.broadcast_to(sel[:, n:n + 1], (N_HEADS, MOBA_BLOCK)) for n in range(n_blocks)],
                                 axis=1)
        slopes = jnp.exp2(-(_iota((N_HEADS, 1), 0) + 1).astype(F32))
        distance = float(past_len) - _iota((1, n_pos), 1).astype(F32)
        logit = jnp.where(chosen > 0.0, sc_s[...] - slopes * distance, NEG)
        self_logit = jnp.sum(q_ref[...] * knew_ref[...], axis=-1, keepdims=True)
        top = jnp.maximum(jnp.max(logit, axis=-1, keepdims=True), self_logit)
        p = jnp.exp(logit - top)
        p_self = jnp.exp(self_logit - top)
        inv = 1.0 / (jnp.sum(p, axis=-1, keepdims=True) + p_self)
        p_ref[...] = p * inv
        pself_ref[...] = jnp.broadcast_to(p_self * inv, pself_ref.shape)


def _sample_scores(page_table, cache_kt, qt, q, k_new, past_len):
    nb, n_pages = page_table.shape
    n_steps = n_pages // _PAGES_PER_STEP
    n_blocks = n_pages // _PAGES_PER_BLOCK
    assert n_blocks <= LANES
    page_spec = lambda r: pl.BlockSpec(
        _PAGE_BLOCK, lambda d, c, pt, r=r: (pt[d, c * _PAGES_PER_STEP + r], 0, 0, 0))
    per_seq = lambda shape: pl.BlockSpec((None,) + shape, lambda d, c, pt: (d,) + (0,) * len(shape))
    grid_spec = pltpu.PrefetchScalarGridSpec(
        num_scalar_prefetch=1, grid=(nb, n_steps),
        in_specs=[pl.BlockSpec((ATTN_W, nb), lambda d, c, pt: (0, 0)), per_seq((N_HEADS, HEAD_DIM)),
                  per_seq((N_HEADS, HEAD_DIM))] + [page_spec(r) for r in range(_PAGES_PER_STEP)],
        out_specs=(per_seq((N_HEADS, n_pages * PAGE)), per_seq((N_HEADS, HEAD_DIM)), per_seq((N_HEADS, LANES))),
        scratch_shapes=[pltpu.VMEM((ATTN_W, LANES), F32), pltpu.VMEM((N_HEADS, n_pages * PAGE), F32),
                        pltpu.VMEM((N_HEADS, LANES), F32)])
    out_shape = (jax.ShapeDtypeStruct((nb, N_HEADS, n_pages * PAGE), F32),
                 jax.ShapeDtypeStruct((nb, N_HEADS, HEAD_DIM), F32), jax.ShapeDtypeStruct((nb, N_HEADS, LANES), F32))
    return pl.pallas_call(
        functools.partial(_sample_scores_kernel, n_steps=n_steps, past_len=past_len),
        grid_spec=grid_spec, out_shape=out_shape,
        compiler_params=_params("parallel", "arbitrary"), name="sample_scores",
    )(page_table, qt, q, k_new, *([cache_kt] * _PAGES_PER_STEP))


_SLABS_PER_HEAD = MOBA_TOPK * _PAGES_PER_BLOCK


def _sample_values_kernel(phys_ref, logical_ref, p_ref, pself_ref, vnew_ref, *refs):
    n_slabs = N_HEADS * _SLABS_PER_HEAD
    v_refs = refs[:n_slabs]
    out_ref = refs[n_slabs]
    d = pl.program_id(0)
    cols = []
    for hd in range(N_HEADS):
        acc = None
        for k in range(_SLABS_PER_HEAD):
            i = hd * _SLABS_PER_HEAD + k
            start = pl.multiple_of(logical_ref[d, i] * PAGE, PAGE)
            term = v_refs[i][...] * p_ref[hd:hd + 1, pl.ds(start, PAGE)]
            acc = term if acc is None else acc + term
        cols.append(jnp.sum(acc, axis=-1, keepdims=True))
    ctx_t = jnp.concatenate(cols + [jnp.zeros((HEAD_DIM, LANES - N_HEADS), F32)], axis=1)
    out_ref[...] = ctx_t.T[0:N_HEADS, :] + pself_ref[...] * vnew_ref[...]


def _sample_values(slab_phys, slab_logical, cache_vt, p, pself, v_new):
    nb, n_slabs = slab_phys.shape
    assert n_slabs == N_HEADS * _SLABS_PER_HEAD
    slab_spec = lambda i: pl.BlockSpec(
        (None, None, HEAD_DIM, PAGE), lambda d, ph, lg, i=i: (ph[d, i], i // _SLABS_PER_HEAD, 0, 0))
    per_seq = lambda shape: pl.BlockSpec((None,) + shape, lambda d, ph, lg: (d,) + (0,) * len(shape))
    head_rows = per_seq((N_HEADS, HEAD_DIM))
    grid_spec = pltpu.PrefetchScalarGridSpec(
        num_scalar_prefetch=2, grid=(nb,),
        in_specs=[per_seq((N_HEADS, p.shape[2])), head_rows, head_rows] + [slab_spec(i) for i in range(n_slabs)],
        out_specs=head_rows)
    return pl.pallas_call(
        _sample_values_kernel, grid_spec=grid_spec, out_shape=jax.ShapeDtypeStruct((nb, N_HEADS, HEAD_DIM), F32),
        compiler_params=_params("parallel"), name="sample_values",
    )(slab_phys, slab_logical, p, pself, v_new, *([cache_vt] * n_slabs))


def _prepare(norm1_g, w_in, q_norm_g, k_norm_g, gmlp_v_norm_g, gmlp_ws, gmlp_b, attn_out_norm_g, gmlp_out_norm_g,
             w_out, norm2_g, peer_wq, peer_keys, peer_u, peer_v):
    wq, wk, wv, wgu, wgv = jnp.split(w_in, [ATTN_W, 2 * ATTN_W, 3 * ATTN_W, 3 * ATTN_W + GMLP_W], axis=1)
    wk_pad = jnp.pad(wk.reshape(D_MODEL, N_HEADS, HEAD_DIM), ((0, 0), (0, 0), (0, LANES - HEAD_DIM)))
    feat = np.arange(ATTN_W)
    w = {
        "g1": norm1_g.reshape(1, D_MODEL),
        "win": w_in.astype(BF16),
        "wnat": jnp.concatenate([wk_pad.reshape(D_MODEL, N_HEADS * LANES), wk, wv, wgu, wgv], axis=1).astype(BF16),
        "wt": jnp.concatenate([wq, wv], axis=1).T.astype(BF16),
        "qg_col": q_norm_g.reshape(HEAD_DIM, 1),
        "kg_pad": jnp.pad(k_norm_g, (0, LANES - HEAD_DIM)).reshape(1, LANES),
        "qg_nat": jnp.tile(q_norm_g, N_HEADS).reshape(1, ATTN_W),
        "kg_nat": jnp.tile(k_norm_g, N_HEADS).reshape(1, ATTN_W),
        "gvg": gmlp_v_norm_g.reshape(1, GMLP_W),
        "bones": jnp.asarray((feat[:, None] // GROUP_DIM) == (feat[None, :] // GROUP_DIM), BF16),
        "wcat": jnp.transpose(gmlp_ws, (1, 0, 2)).reshape(GMLP_CHUNK, N_GROUPS * GMLP_CHUNK),
        "gbias": jnp.repeat(gmlp_b.T, GROUP_DIM, axis=1),
        "ws_one": jnp.repeat(gmlp_ws[:, 0, 0], GROUP_DIM).reshape(1, GMLP_W),
        "b_one": jnp.repeat(gmlp_b[:, 0], GROUP_DIM).reshape(1, GMLP_W),
        "gog": gmlp_out_norm_g.reshape(1, GMLP_W),
        "aog": attn_out_norm_g.reshape(1, ATTN_W),
        "wo_attn": w_out[:ATTN_W].astype(BF16),
        "wo_gmlp": w_out[ATTN_W:].astype(BF16),
        "g2": norm2_g.reshape(1, D_MODEL),
        "wks": _fold_keys(peer_keys.reshape(2 * PEER_HEADS, PEER_KEYS, PEER_HALF), peer_wq.T),
        "u": peer_u.astype(BF16),
        "vt": peer_v.T.astype(BF16),
        "cand_idx": _cand_index_table(),
    }
    return w


def _sample_attention(page_table, cache_k, cache_v, q, qt, k_new, v_new, past_len):
    nb = q.shape[0]
    heads = lambda a: a.reshape(nb, N_HEADS, HEAD_DIM)
    as_stored = lambda cache: jnp.transpose(cache, (0, 2, 3, 1))
    p, pself, picked = _sample_scores(page_table, as_stored(cache_k), qt, heads(q), heads(k_new), past_len)
    blocks = picked[:, :, :MOBA_TOPK].astype(jnp.int32)
    logical = (blocks[..., None] * _PAGES_PER_BLOCK + jnp.arange(_PAGES_PER_BLOCK, dtype=jnp.int32)).reshape(nb, -1)
    phys = jnp.take_along_axis(page_table, logical, axis=1)
    ctx = _sample_values(phys, logical.astype(jnp.int32), as_stored(cache_v), p, pself, heads(v_new))
    return ctx.reshape(nb, ATTN_W)


def _layer_tail(x, attn, gm, w, *, route_tile, peer_tile, attn_transposed):
    h, xn2, m1, c1, r2, e2 = _route(x, attn, gm, w, tt=route_tile, attn_transposed=attn_transposed)
    return _peer(xn2, h, m1, c1, r2, e2, w, tt=peer_tile)


def kernel(x_prompt, x_sample, cache_k, cache_v, page_table, norm1_g, w_in, q_norm_g, k_norm_g, gmlp_v_norm_g,
           gmlp_ws, gmlp_b, attn_out_norm_g, gmlp_out_norm_g, w_out, norm2_g, peer_wq, peer_keys, peer_u, peer_v):
    w = _prepare(norm1_g, w_in, q_norm_g, k_norm_g, gmlp_v_norm_g, gmlp_ws, gmlp_b, attn_out_norm_g,
                 gmlp_out_norm_g, w_out, norm2_g, peer_wq, peer_keys, peer_u, peer_v)
    b, t, _ = x_prompt.shape
    assert b == 1
    xp = x_prompt.reshape(t, D_MODEL)
    k_p, v_p, kaug, qaug, vt, gm_p, gv_last = _proj_prompt(xp, w)
    attn_t = _moba_prompt(kaug, qaug, vt)
    y_p = _layer_tail(xp, attn_t, gm_p, w, route_tile=512, peer_tile=512, attn_transposed=True)

    nb, ds, _ = x_sample.shape
    assert ds == 1
    xs = x_sample.reshape(nb, D_MODEL)
    past_len = page_table.shape[1] * PAGE
    assert past_len % MOBA_BLOCK == 0
    k_s, v_s, gv_s, gm_s, q_s, qt_s = _proj_sample(xs, w)
    attn_s = _sample_attention(page_table, cache_k, cache_v, q_s, qt_s, k_s, v_s, past_len)
    y_s = _layer_tail(xs, attn_s, gm_s, w, route_tile=nb, peer_tile=nb, attn_transposed=False)

    return (y_p.reshape(1, t, D_MODEL), y_s.reshape(nb, 1, D_MODEL),
            k_p.reshape(1, t, N_HEADS, HEAD_DIM), v_p.reshape(1, t, N_HEADS, HEAD_DIM),
            gv_last.reshape(1, GMLP_CHUNK, GMLP_W),
            k_s.reshape(nb, 1, N_HEADS, HEAD_DIM), v_s.reshape(nb, 1, N_HEADS, HEAD_DIM),
            gv_s.reshape(nb, 1, GMLP_W))
```
